```python
import jax, jax.numpy as jnp
from jax import lax
import numpy as np

D_MODEL = 4096
BATCH = 4
SEQ = 4096
DEPTH = 1
DEC_BATCH = 16
DEC_SEQ = 64
PAST_LEN = 2048

CHUNK = 64
MIX_WIDTH = D_MODEL
WIDTH_A = MIX_WIDTH // 2
N_HEADS_A = 8
DV_A = WIDTH_A // N_HEADS_A
DK_A = DV_A // 2
WIDTH_B = MIX_WIDTH - WIDTH_A
N_BLOCKS_B = 16
BLOCK_B = WIDTH_B // N_BLOCKS_B
CONV_W = 4
LRU_C = 8.0
D_FF = -(-8 * D_MODEL // (3 * 256)) * 256
EPS = 1e-6

Q_COLS = N_HEADS_A * DK_A
IN_SPLITS = (Q_COLS, 2 * Q_COLS, 2 * Q_COLS + WIDTH_A, 2 * Q_COLS + 2 * WIDTH_A,
             2 * Q_COLS + 2 * WIDTH_A + N_HEADS_A, 2 * Q_COLS + 2 * WIDTH_A + 2 * N_HEADS_A,
             2 * Q_COLS + 2 * WIDTH_A + 2 * N_HEADS_A + WIDTH_B)
IN_COLS = 2 * Q_COLS + 2 * WIDTH_A + 2 * N_HEADS_A + 2 * WIDTH_B

kernel_name = 'hymba_mlstm_rglru_adaln_stream_step'


def _rmsnorm(x, g):
    xf = x.astype(jnp.float32)
    y = xf * lax.rsqrt(jnp.mean(xf * xf, axis=-1, keepdims=True) + EPS)
    return (y * g.astype(jnp.float32)).astype(x.dtype)


def _block_diag(x, w, b):
    B, T, _ = x.shape
    xr = x.reshape(B, T, N_BLOCKS_B, BLOCK_B)
    return jnp.einsum('btni,nij->btnj', xr, w).reshape(B, T, WIDTH_B) + b


def _causal_conv(x, buf, w, b):
    T = x.shape[1]
    xp = jnp.concatenate([buf.astype(x.dtype), x], axis=1)
    y = xp[:, 0:T] * w[0]
    for j in range(1, CONV_W):
        y = y + xp[:, j:j + T] * w[j]
    return y + b, xp[:, T:]


def _lin_combine(left, right):
    a1, b1 = left
    a2, b2 = right
    return a1 * a2, a2 * b1 + b2


def _rg_lru(x, r, i, lam, h0):
    log_a = LRU_C * r * jax.nn.log_sigmoid(lam)
    a = jnp.exp(log_a)
    u = jnp.sqrt(-jnp.expm1(2.0 * log_a)) * (i * x)
    u = u.at[:, 0].add(a[:, 0] * h0)
    _, h = lax.associative_scan(_lin_combine, (a, u), axis=1)
    return h


def _mlstm_chunk(carry, xs):
    C, n, m = carry
    q, k, v, ig, lf = xs
    L = q.shape[2]
    bcum = jnp.cumsum(lf, axis=-1)
    causal = jnp.tril(jnp.ones((L, L), dtype=bool))
    dmat = jnp.where(causal, bcum[..., :, None] - bcum[..., None, :] + ig[..., None, :], -jnp.inf)
    m_inter = bcum + m[..., None]
    m_t = jnp.maximum(m_inter, jnp.max(dmat, axis=-1))
    w = jnp.exp(dmat - m_t[..., None])
    s_inter = jnp.exp(m_inter - m_t)
    s = jnp.einsum('bhtd,bhsd->bhts', q, k) * w
    num = s_inter[..., None] * jnp.einsum('bhtd,bhde->bhte', q, C) + jnp.einsum('bhts,bhse->bhte', s, v)
    den = s_inter * jnp.einsum('bhtd,bhd->bht', q, n) + jnp.sum(s, axis=-1)
    h = num / jnp.maximum(jnp.abs(den), jnp.exp(-m_t))[..., None]
    m_new = m_t[..., -1]
    decay = jnp.exp(bcum[..., -1] + m - m_new)
    w_last = jnp.exp(bcum[..., -1:] - bcum + ig - m_new[..., None])
    C_new = decay[..., None, None] * C + jnp.einsum('bhs,bhsd,bhse->bhde', w_last, k, v)
    n_new = decay[..., None] * n + jnp.einsum('bhs,bhsd->bhd', w_last, k)
    return (C_new, n_new, m_new), h


def _mlstm(q, k, v, ig, lf, C, n, m):
    B, H, T, _ = q.shape
    L = min(T, CHUNK)
    NC = T // L

    def to_chunks(a):
        return jnp.moveaxis(a.reshape(a.shape[:2] + (NC, L) + a.shape[3:]), 2, 0)

    (C, n, m), h = lax.scan(_mlstm_chunk, (C, n, m), (to_chunks(q), to_chunks(k), to_chunks(v), to_chunks(ig), to_chunks(lf)))
    h = jnp.moveaxis(h, 0, 2).reshape(B, H, T, DV_A)
    return h, (C, n, m)


def _mixer(h, state, p):
    C0, n0, m0, hl0, buf0 = state
    B, T, _ = h.shape
    f32 = jnp.float32
    proj = h @ p['w_in']
    q, k, v, o, ig, fg, xb, gb = jnp.split(proj, IN_SPLITS, axis=-1)

    def heads(a, d):
        return a.reshape(B, T, N_HEADS_A, d).transpose(0, 2, 1, 3).astype(f32)
    q = heads(q, DK_A)
    k = heads(k, DK_A) * (DK_A ** -0.5)
    v = heads(v, DV_A)
    bg = p['b_gates_a'].astype(f32)
    ig = (ig.astype(f32) + bg[0]).transpose(0, 2, 1)
    lf = jax.nn.log_sigmoid(fg.astype(f32) + bg[1]).transpose(0, 2, 1)
    ha, (C1, n1, m1) = _mlstm(q, k, v, ig, lf, C0.astype(f32), n0.astype(f32), m0.astype(f32))
    ha = _rmsnorm(ha, p['head_norm_g'][:, None, :]).transpose(0, 2, 1, 3).reshape(B, T, WIDTH_A)
    ya = ha.astype(h.dtype) * jax.nn.sigmoid(o)

    xc, buf1 = _causal_conv(xb, buf0, p['conv_w'], p['conv_b'])
    r = jax.nn.sigmoid(_block_diag(xc, p['w_r'], p['b_r']))
    i = jax.nn.sigmoid(_block_diag(xc, p['w_i'], p['b_i']))
    hl = _rg_lru(xc.astype(f32), r.astype(f32), i.astype(f32), p['lru_lambda'].astype(f32), hl0.astype(f32))
    yb = hl.astype(h.dtype) * jax.nn.gelu(gb)

    out = jnp.concatenate([ya, yb], axis=-1) @ p['w_out']
    return out, (C1, n1, m1, hl[:, -1], buf1)


def _layer(x, c, state, p):
    ada = jax.nn.silu(c) @ p['w_ada'] + p['b_ada']
    sh1, sc1, g1, sh2, sc2, g2 = jnp.split(ada, 6, axis=-1)
    h = _rmsnorm(x, p['norm1_g']) * (1.0 + sc1[:, None]) + sh1[:, None]
    mix, new_state = _mixer(h, state, p)
    x = x + g1[:, None] * mix
    h = _rmsnorm(x, p['norm2_g']) * (1.0 + sc2[:, None]) + sh2[:, None]
    gate, up = jnp.split(h @ p['w_gu'], 2, axis=-1)
    x = x + g2[:, None] * ((jax.nn.silu(gate) * up) @ p['w_down'])
    return x, new_state


def setup_inputs(seed: int = 0) -> dict:
    key = jax.random.key(seed)
    ks = jax.random.split(key, 32)
    f32 = jnp.float32

    def nrm(k, shape, s):
        return jax.random.normal(k, shape, f32) * s

    a0 = jax.random.uniform(ks[23], (DEPTH, WIDTH_B), f32, 0.9, 0.999)
    forget_b = jnp.linspace(3.0, 6.0, N_HEADS_A, dtype=f32)[None] + nrm(ks[15], (DEPTH, N_HEADS_A), 0.1)
    return {
        'x_prompt': nrm(ks[0], (BATCH, SEQ, D_MODEL), 1.0),
        'x_sample': nrm(ks[1], (DEC_BATCH, DEC_SEQ, D_MODEL), 1.0),
        'c_prompt': nrm(ks[2], (BATCH, D_MODEL), 1.0),
        'c_sample': nrm(ks[3], (DEC_BATCH, D_MODEL), 1.0),
        'state_mlstm_C': nrm(ks[4], (DEPTH, DEC_BATCH, N_HEADS_A, DK_A, DV_A), 0.5),
        'state_mlstm_n': nrm(ks[5], (DEPTH, DEC_BATCH, N_HEADS_A, DK_A), 0.5),
        'state_mlstm_m': jax.random.uniform(ks[6], (DEPTH, DEC_BATCH, N_HEADS_A), f32, -1.0, 1.0),
        'state_lru_h': nrm(ks[7], (DEPTH, DEC_BATCH, WIDTH_B), 0.5),
        'state_conv': nrm(ks[8], (DEPTH, DEC_BATCH, CONV_W - 1, WIDTH_B), 1.0),
        'w_ada': nrm(ks[9], (DEPTH, D_MODEL, 6 * D_MODEL), 0.5 * D_MODEL ** -0.5),
        'b_ada': nrm(ks[10], (DEPTH, 6 * D_MODEL), 0.02),
        'norm1_g': 1.0 + nrm(ks[11], (DEPTH, D_MODEL), 0.02),
        'norm2_g': 1.0 + nrm(ks[12], (DEPTH, D_MODEL), 0.02),
        'w_in': nrm(ks[13], (DEPTH, D_MODEL, IN_COLS), D_MODEL ** -0.5),
        'b_gates_a': jnp.stack([nrm(ks[14], (DEPTH, N_HEADS_A), 0.1), forget_b], axis=1),
        'head_norm_g': 1.0 + nrm(ks[16], (DEPTH, N_HEADS_A, DV_A), 0.02),
        'conv_w': nrm(ks[17], (DEPTH, CONV_W, WIDTH_B), CONV_W ** -0.5),
        'conv_b': nrm(ks[18], (DEPTH, WIDTH_B), 0.02),
        'w_r': nrm(ks[19], (DEPTH, N_BLOCKS_B, BLOCK_B, BLOCK_B), BLOCK_B ** -0.5),
        'b_r': nrm(ks[20], (DEPTH, WIDTH_B), 0.02),
        'w_i': nrm(ks[21], (DEPTH, N_BLOCKS_B, BLOCK_B, BLOCK_B), BLOCK_B ** -0.5),
        'b_i': nrm(ks[22], (DEPTH, WIDTH_B), 0.02),
        'lru_lambda': jnp.log(a0) - jnp.log1p(-a0),
        'w_out': nrm(ks[24], (DEPTH, MIX_WIDTH, D_MODEL), MIX_WIDTH ** -0.5),
        'w_gu': nrm(ks[25], (DEPTH, D_MODEL, 2 * D_FF), D_MODEL ** -0.5),
        'w_down': nrm(ks[26], (DEPTH, D_FF, D_MODEL), D_FF ** -0.5),
        'normf_g': 1.0 + nrm(ks[27], (D_MODEL,), 0.02),
    }


def reference(x_prompt, x_sample, c_prompt, c_sample, state_mlstm_C, state_mlstm_n, state_mlstm_m,
              state_lru_h, state_conv, w_ada, b_ada, norm1_g, norm2_g, w_in, b_gates_a, head_norm_g,
              conv_w, conv_b, w_r, b_r, w_i, b_i, lru_lambda, w_out, w_gu, w_down, normf_g):
    f32 = jnp.float32
    xp, xs = x_prompt, x_sample
    Bp = xp.shape[0]
    p_states = []
    s_states = []
    for l in range(DEPTH):
        p = {'w_ada': w_ada[l], 'b_ada': b_ada[l], 'norm1_g': norm1_g[l], 'norm2_g': norm2_g[l],
             'w_in': w_in[l], 'b_gates_a': b_gates_a[l], 'head_norm_g': head_norm_g[l],
             'conv_w': conv_w[l], 'conv_b': conv_b[l], 'w_r': w_r[l], 'b_r': b_r[l],
             'w_i': w_i[l], 'b_i': b_i[l], 'lru_lambda': lru_lambda[l], 'w_out': w_out[l],
             'w_gu': w_gu[l], 'w_down': w_down[l]}
        zero_state = (jnp.zeros((Bp, N_HEADS_A, DK_A, DV_A), f32),
                      jnp.zeros((Bp, N_HEADS_A, DK_A), f32),
                      jnp.zeros((Bp, N_HEADS_A), f32),
                      jnp.zeros((Bp, WIDTH_B), f32),
                      jnp.zeros((Bp, CONV_W - 1, WIDTH_B), xp.dtype))
        xp, st_p = _layer(xp, c_prompt, zero_state, p)
        cache_state = (state_mlstm_C[l], state_mlstm_n[l], state_mlstm_m[l], state_lru_h[l], state_conv[l])
        xs, st_s = _layer(xs, c_sample, cache_state, p)
        p_states.append(st_p)
        s_states.append(st_s)
    y_prompt = _rmsnorm(xp, normf_g)
    y_sample = _rmsnorm(xs, normf_g)
    p_C = jnp.stack([s[0] for s in p_states])
    p_n = jnp.stack([s[1] for s in p_states])
    p_m = jnp.stack([s[2] for s in p_states])
    p_h = jnp.stack([s[3] for s in p_states])
    p_conv = jnp.stack([s[4] for s in p_states])
    s_C = jnp.stack([s[0] for s in s_states])
    s_n = jnp.stack([s[1] for s in s_states])
    s_m = jnp.stack([s[2] for s in s_states])
    s_h = jnp.stack([s[3] for s in s_states])
    s_conv = jnp.stack([s[4] for s in s_states])
    return (y_prompt, y_sample, p_C, p_n, p_m, p_h, p_conv, s_C, s_n, s_m, s_h, s_conv)
```

```python
import functools

import jax
import jax.numpy as jnp
from jax import lax
from jax.experimental import pallas as pl
from jax.experimental.pallas import tpu as pltpu

F32 = jnp.float32
BF16 = jnp.bfloat16
EPS = 1e-6
LRU_C = 8.0
LANES = 128
SUBLANES = 8
VMEM_LIMIT_BYTES = 58 * 1024 * 1024


def _cparams(*sem):
    return pltpu.CompilerParams(dimension_semantics=sem, vmem_limit_bytes=VMEM_LIMIT_BYTES)


def _pick(total, target, quantum):
    if total <= target:
        return total
    best = None
    for cand in range(quantum, target + 1, quantum):
        if total % cand == 0:
            best = cand
    assert best is not None, (total, target, quantum)
    return best


def _row_tiling(B, T, target_rows):
    if T >= target_rows:
        return 1, _pick(T, target_rows, 16)
    bb = _pick(B, max(1, target_rows // T), 1)
    return bb, T


def _log_sigmoid(x):
    return jnp.minimum(x, 0.0) - jnp.log1p(jnp.exp(-jnp.abs(x)))


def _gelu_tanh(x):
    c = 0.7978845608028654
    return 0.5 * x * (1.0 + jnp.tanh(c * (x + 0.044715 * (x * x * x))))


def _dot(a, b):
    return jnp.dot(a, b, preferred_element_type=F32)


def _ada_kernel(c_ref, w_ref, b_ref, o_ref):
    c = c_ref[...]
    s = (c * jax.nn.sigmoid(c)).astype(BF16)
    o_ref[...] = _dot(s, w_ref[...].astype(BF16)) + b_ref[...]


def _ada(c, w, b):
    R, D = c.shape
    N = w.shape[1]
    tn = _pick(N, 512, LANES)
    return pl.pallas_call(
        _ada_kernel,
        grid=(N // tn,),
        in_specs=[pl.BlockSpec((R, D), lambda j: (0, 0)),
                  pl.BlockSpec((D, tn), lambda j: (0, j)),
                  pl.BlockSpec((1, tn), lambda j: (0, j))],
        out_specs=pl.BlockSpec((R, tn), lambda j: (0, j)),
        out_shape=jax.ShapeDtypeStruct((R, N), F32),
        compiler_params=_cparams("arbitrary"),
        name="ada",
    )(c, w, b)


def _normmod(x, g, sc, sh):
    y = x * lax.rsqrt(jnp.mean(x * x, axis=-1, keepdims=True) + EPS) * g
    return y * (1.0 + sc) + sh


def _normmod_gates_kernel(x_ref, g_ref, sc_ref, sh_ref, wg_ref, bg_ref, h_ref, gr_ref, *, n_heads):
    bb, tt, D = x_ref.shape
    hb = _normmod(x_ref[...], g_ref[...], sc_ref[...], sh_ref[...]).astype(BF16)
    h_ref[...] = hb
    pre = _dot(hb.reshape(bb * tt, D), wg_ref[...]) + bg_ref[...]
    lane = lax.broadcasted_iota(jnp.int32, pre.shape, 1)
    gates = jnp.where(lane < n_heads, pre, _log_sigmoid(pre))
    rows = gr_ref.shape[1]
    for i in range(bb):
        gr_ref[i] = gates[i * tt:(i + 1) * tt, :].T[:rows, :]


def _normmod_gates(x, g, sc, sh, wg, bg, n_heads):
    B, T, D = x.shape
    bb, tt = _row_tiling(B, T, 512)
    GR = -(-2 * n_heads // SUBLANES) * SUBLANES
    return pl.pallas_call(
        functools.partial(_normmod_gates_kernel, n_heads=n_heads),
        grid=(B // bb, T // tt),
        in_specs=[pl.BlockSpec((bb, tt, D), lambda b, t: (b, t, 0)),
                  pl.BlockSpec((1, 1, D), lambda b, t: (0, 0, 0)),
                  pl.BlockSpec((bb, 1, D), lambda b, t: (b, 0, 0)),
                  pl.BlockSpec((bb, 1, D), lambda b, t: (b, 0, 0)),
                  pl.BlockSpec((D, LANES), lambda b, t: (0, 0)),
                  pl.BlockSpec((1, LANES), lambda b, t: (0, 0))],
        out_specs=[pl.BlockSpec((bb, tt, D), lambda b, t: (b, t, 0)),
                   pl.BlockSpec((bb, GR, tt), lambda b, t: (b, 0, t))],
        out_shape=[jax.ShapeDtypeStruct((B, T, D), BF16),
                   jax.ShapeDtypeStruct((B, GR, T), F32)],
        compiler_params=_cparams("arbitrary", "arbitrary"),
        name="normmod_gates",
    )(x, g, sc, sh, wg, bg)


def _mm_kernel(a_ref, w_ref, o_ref):
    o_ref[...] = _dot(a_ref[...], w_ref[...]).astype(o_ref.dtype)


def _matmul(a, w, out_dtype):
    M, K = a.shape
    N = w.shape[1]
    tm = _pick(M, 1024, 16)
    tn = _pick(N, 1024, LANES)
    return pl.pallas_call(
        _mm_kernel,
        grid=(M // tm, N // tn),
        in_specs=[pl.BlockSpec((tm, K), lambda i, j: (i, 0)),
                  pl.BlockSpec((K, tn), lambda i, j: (0, j))],
        out_specs=pl.BlockSpec((tm, tn), lambda i, j: (i, j)),
        out_shape=jax.ShapeDtypeStruct((M, N), out_dtype),
        compiler_params=_cparams("arbitrary", "arbitrary"),
        name="inproj",
    )(a, w)


def _mlstm_kernel(q_ref, k_ref, v_ref, o_ref, gr_ref, c0_ref, n0_ref, m0_ref, hg_ref,
                  y_ref, c_ref, n_ref, m_ref, *, chunk, n_heads):
    hd = pl.program_id(1)
    tb = pl.program_id(2)
    L = chunk
    Tb, dk = q_ref.shape[1], q_ref.shape[2]
    nc = Tb // L

    @pl.when(tb == 0)
    def _():
        c_ref[...] = c0_ref[...]
        n_ref[...] = n0_ref[...]
        m_ref[...] = m0_ref[...]

    row = lax.broadcasted_iota(jnp.int32, (L, L), 0)
    col = lax.broadcasted_iota(jnp.int32, (L, L), 1)
    tril = col <= row
    eye = col == row
    k_scale = dk ** -0.5
    hg = hg_ref[0]

    def chunk_step(c, carry):
        r0 = pl.multiple_of(c * L, L)
        sl = pl.ds(r0, L)
        q = q_ref[0, sl, :]
        k = k_ref[0, sl, :] * k_scale
        v = v_ref[0, sl, :]
        ci = tb * nc + c
        ig_row = gr_ref[0, hd, pl.ds(ci, 1), :]
        lf_row = gr_ref[0, n_heads + hd, pl.ds(ci, 1), :]
        lf_col = jnp.sum(jnp.where(eye, lf_row, 0.0), axis=1, keepdims=True)
        ig_col = jnp.sum(jnp.where(eye, ig_row, 0.0), axis=1, keepdims=True)
        bcum_col = jnp.sum(jnp.where(tril, lf_row, 0.0), axis=1, keepdims=True)
        bcum_row = jnp.sum(jnp.where(row <= col, lf_col, 0.0), axis=0, keepdims=True)
        a_row = ig_row - bcum_row
        a_col = ig_col - bcum_col
        m_prev = m_ref[0, 0]
        g = jnp.maximum(m_prev, jnp.max(jnp.where(tril, a_row, -jnp.inf), axis=1, keepdims=True))
        w = jnp.where(tril, jnp.exp(jnp.minimum(a_row - g, 0.0)), 0.0)
        s_inter = jnp.exp(m_prev - g)

        qb, kb, vb = q.astype(BF16), k.astype(BF16), v.astype(BF16)
        s = lax.dot_general(qb, kb, (((1,), (1,)), ((), ())), preferred_element_type=F32) * w
        c_prev = c_ref[0, 0]
        n_prev = n_ref[0, 0]
        num = s_inter * _dot(qb, c_prev.astype(BF16)) + _dot(s.astype(BF16), vb)
        den = s_inter * jnp.sum(q * n_prev, axis=1, keepdims=True) + jnp.sum(s, axis=1, keepdims=True)
        h = num / jnp.maximum(jnp.abs(den), jnp.exp(-(bcum_col + g)))
        hn = h * lax.rsqrt(jnp.mean(h * h, axis=1, keepdims=True) + EPS) * hg
        y_ref[0, sl, :] = (hn * jax.nn.sigmoid(o_ref[0, sl, :])).astype(y_ref.dtype)

        g_last = g[L - 1:L, :]
        decay = jnp.exp(m_prev - g_last)
        w_last = jnp.exp(a_col - g_last)
        kv = lax.dot_general(kb, (w_last * v).astype(BF16), (((0,), (0,)), ((), ())),
                             preferred_element_type=F32)
        c_ref[0, 0] = decay * c_prev + kv
        n_ref[0, 0] = decay * n_prev + jnp.sum(w_last * k, axis=0, keepdims=True)
        m_ref[0, 0] = bcum_col[L - 1:L, :] + g_last
        return carry

    lax.fori_loop(0, nc, chunk_step, 0)


def _mlstm(proj, gates_row, c0, n0, m0, head_g, *, n_heads, dk, dv):
    B, T, _ = proj.shape
    H = n_heads
    L = _pick(T, 128, 16)
    Tb = _pick(T, 1024, L)
    GR = gates_row.shape[1]
    gates4 = gates_row.reshape(B, GR, T // L, L)
    k_blk0 = (H * dk) // dk
    assert (2 * H * dk) % dv == 0
    v_blk0 = (2 * H * dk) // dv
    o_blk0 = v_blk0 + H
    n0 = n0.reshape(B, H, 1, dk)
    m0 = m0.reshape(B, H, 1, 1)
    outs = pl.pallas_call(
        functools.partial(_mlstm_kernel, chunk=L, n_heads=H),
        grid=(B, H, T // Tb),
        in_specs=[pl.BlockSpec((1, Tb, dk), lambda b, h, t: (b, t, h)),
                  pl.BlockSpec((1, Tb, dk), lambda b, h, t: (b, t, k_blk0 + h)),
                  pl.BlockSpec((1, Tb, dv), lambda b, h, t: (b, t, v_blk0 + h)),
                  pl.BlockSpec((1, Tb, dv), lambda b, h, t: (b, t, o_blk0 + h)),
                  pl.BlockSpec((1, GR, T // L, L), lambda b, h, t: (b, 0, 0, 0)),
                  pl.BlockSpec((1, 1, dk, dv), lambda b, h, t: (b, h, 0, 0)),
                  pl.BlockSpec((1, 1, 1, dk), lambda b, h, t: (b, h, 0, 0)),
                  pl.BlockSpec((1, 1, 1, 1), lambda b, h, t: (b, h, 0, 0)),
                  pl.BlockSpec((1, 1, dv), lambda b, h, t: (h, 0, 0))],
        out_specs=[pl.BlockSpec((1, Tb, dv), lambda b, h, t: (b, t, h)),
                   pl.BlockSpec((1, 1, dk, dv), lambda b, h, t: (b, h, 0, 0)),
                   pl.BlockSpec((1, 1, 1, dk), lambda b, h, t: (b, h, 0, 0)),
                   pl.BlockSpec((1, 1, 1, 1), lambda b, h, t: (b, h, 0, 0))],
        out_shape=[jax.ShapeDtypeStruct((B, T, H * dv), BF16),
                   jax.ShapeDtypeStruct((B, H, dk, dv), F32),
                   jax.ShapeDtypeStruct((B, H, 1, dk), F32),
                   jax.ShapeDtypeStruct((B, H, 1, 1), F32)],
        compiler_params=_cparams("arbitrary", "arbitrary", "arbitrary"),
        name="mlstm",
    )(proj, proj, proj, proj, gates4, c0, n0, m0, head_g.reshape(H, 1, dv))
    y, c1, n1, m1 = outs
    return y, c1, n1.reshape(B, H, dk), m1.reshape(B, H)


def _rglru_kernel(xb_ref, gb_ref, cv0_ref, h0_ref, cw_ref, cb_ref, wr_ref, wi_ref, br_ref, bi_ref,
                  lam_ref, y_ref, h_ref, cv_ref, xext, a_s, u_s):
    tb = pl.program_id(1)
    Tb, W = xb_ref.shape[1], xb_ref.shape[2]
    nb, bw = wr_ref.shape[0], wr_ref.shape[1]
    taps = cw_ref.shape[0]
    pad = SUBLANES

    @pl.when(tb == 0)
    def _():
        h_ref[...] = h0_ref[...]
        cv_ref[...] = cv0_ref[...]

    x = xb_ref[0]
    xext[pl.ds(pad, Tb), :] = x
    xext[pl.ds(pad - (taps - 1), taps - 1), :] = cv_ref[0]
    xc = cb_ref[...] + cw_ref[taps - 1:taps, :] * x
    for j in range(taps - 1):
        xc = xc + cw_ref[j:j + 1, :] * xext[pl.ds(pad - (taps - 1) + j, Tb), :]
    cv_ref[0] = xext[pl.ds(pad + Tb - (taps - 1), taps - 1), :]

    log_sig_lam = _log_sigmoid(lam_ref[...])
    for n in range(nb):
        cols = slice(n * bw, (n + 1) * bw)
        xn = xc[:, cols]
        xnb = xn.astype(BF16)
        r = jax.nn.sigmoid(_dot(xnb, wr_ref[n]) + br_ref[:, cols])
        i = jax.nn.sigmoid(_dot(xnb, wi_ref[n]) + bi_ref[:, cols])
        log_a = LRU_C * r * log_sig_lam[:, cols]
        a = jnp.exp(log_a)
        a_s[:, cols] = a
        u_s[:, cols] = jnp.sqrt(-jnp.tanh(log_a) * (1.0 + a * a)) * (i * xn)

    def scan_step(t, h):
        h = a_s[pl.ds(t, 1), :] * h + u_s[pl.ds(t, 1), :]
        u_s[pl.ds(t, 1), :] = h
        return h

    h_ref[0] = lax.fori_loop(0, Tb, scan_step, h_ref[0], unroll=8)
    y_ref[0] = (u_s[...] * _gelu_tanh(gb_ref[0])).astype(y_ref.dtype)


def _rglru(proj, xb_col0, cv0, h0, conv_w, conv_b, w_r, w_i, b_r, b_i, lam):
    B, T, _ = proj.shape
    W = conv_w.shape[1]
    taps = conv_w.shape[0]
    assert xb_col0 % W == 0 and T >= taps - 1
    xb_blk = xb_col0 // W
    Tb = _pick(T, 256, SUBLANES)
    nb, bw = w_r.shape[0], w_r.shape[1]
    row = lambda a: a.reshape(1, W)
    outs = pl.pallas_call(
        _rglru_kernel,
        grid=(B, T // Tb),
        in_specs=[pl.BlockSpec((1, Tb, W), lambda b, t: (b, t, xb_blk)),
                  pl.BlockSpec((1, Tb, W), lambda b, t: (b, t, xb_blk + 1)),
                  pl.BlockSpec((1, taps - 1, W), lambda b, t: (b, 0, 0)),
                  pl.BlockSpec((1, 1, W), lambda b, t: (b, 0, 0)),
                  pl.BlockSpec((taps, W), lambda b, t: (0, 0)),
                  pl.BlockSpec((1, W), lambda b, t: (0, 0)),
                  pl.BlockSpec((nb, bw, bw), lambda b, t: (0, 0, 0)),
                  pl.BlockSpec((nb, bw, bw), lambda b, t: (0, 0, 0)),
                  pl.BlockSpec((1, W), lambda b, t: (0, 0)),
                  pl.BlockSpec((1, W), lambda b, t: (0, 0)),
                  pl.BlockSpec((1, W), lambda b, t: (0, 0))],
        out_specs=[pl.BlockSpec((1, Tb, W), lambda b, t: (b, t, 0)),
                   pl.BlockSpec((1, 1, W), lambda b, t: (b, 0, 0)),
                   pl.BlockSpec((1, taps - 1, W), lambda b, t: (b, 0, 0))],
        out_shape=[jax.ShapeDtypeStruct((B, T, W), BF16),
                   jax.ShapeDtypeStruct((B, 1, W), F32),
                   jax.ShapeDtypeStruct((B, taps - 1, W), F32)],
        scratch_shapes=[pltpu.VMEM((Tb + SUBLANES, W), F32),
                        pltpu.VMEM((Tb, W), F32),
                        pltpu.VMEM((Tb, W), F32)],
        compiler_params=_cparams("arbitrary", "arbitrary"),
        name="rglru",
    )(proj, proj, cv0, h0.reshape(B, 1, W), conv_w, row(conv_b), w_r.astype(BF16), w_i.astype(BF16),
      row(b_r), row(b_i), row(lam))
    y, h1, cv1 = outs
    return y, h1.reshape(B, W), cv1


def _outproj_kernel(ya_ref, yb_ref, wt_ref, wb_ref, x_ref, g_ref, o_ref):
    bb, tt, wa = ya_ref.shape
    wb = yb_ref.shape[2]
    mix = (_dot(ya_ref[...].reshape(bb * tt, wa), wt_ref[...])
           + _dot(yb_ref[...].reshape(bb * tt, wb), wb_ref[...]))
    o_ref[...] = x_ref[...] + g_ref[...] * mix.reshape(bb, tt, mix.shape[-1])


def _outproj(ya, yb, w_out, x, g1):
    B, T, D = x.shape
    wa, wb = ya.shape[2], yb.shape[2]
    assert wa == wb
    bb, tt = _row_tiling(B, T, 1024)
    tn = _pick(D, 512, LANES)
    return pl.pallas_call(
        _outproj_kernel,
        grid=(B // bb, T // tt, D // tn),
        in_specs=[pl.BlockSpec((bb, tt, wa), lambda b, t, j: (b, t, 0)),
                  pl.BlockSpec((bb, tt, wb), lambda b, t, j: (b, t, 0)),
                  pl.BlockSpec((wa, tn), lambda b, t, j: (0, j)),
                  pl.BlockSpec((wb, tn), lambda b, t, j: (1, j)),
                  pl.BlockSpec((bb, tt, tn), lambda b, t, j: (b, t, j)),
                  pl.BlockSpec((bb, 1, tn), lambda b, t, j: (b, 0, j))],
        out_specs=pl.BlockSpec((bb, tt, tn), lambda b, t, j: (b, t, j)),
        out_shape=jax.ShapeDtypeStruct((B, T, D), F32),
        compiler_params=_cparams("arbitrary", "arbitrary", "arbitrary"),
        name="outproj",
    )(ya, yb, w_out, w_out, x, g1)


def _ffn_kernel(x_ref, ng_ref, sc_ref, sh_ref, g2_ref, wg_ref, wu_ref, wd_ref, gf_ref, o_ref, h_s,
                *, n_col_chunks, final_norm):
    j = pl.program_id(2)
    bb, tt, D = x_ref.shape

    @pl.when(j == 0)
    def _():
        h = _normmod(x_ref[...], ng_ref[...], sc_ref[...], sh_ref[...])
        h_s[...] = h.astype(BF16).reshape(bb * tt, D)
        o_ref[...] = jnp.zeros_like(o_ref)

    hb = h_s[...]
    gate = _dot(hb, wg_ref[...])
    up = _dot(hb, wu_ref[...])
    act = (gate * jax.nn.sigmoid(gate) * up).astype(BF16)
    cw = D // n_col_chunks
    for c in range(n_col_chunks):
        cols = slice(c * cw, (c + 1) * cw)
        o_ref[:, :, cols] += _dot(act, wd_ref[:, cols]).reshape(bb, tt, cw)

    @pl.when(j == pl.num_programs(2) - 1)
    def _():
        x2 = x_ref[...] + g2_ref[...] * o_ref[...]
        if final_norm:
            x2 = x2 * lax.rsqrt(jnp.mean(x2 * x2, axis=-1, keepdims=True) + EPS) * gf_ref[...]
        o_ref[...] = x2


def _ffn(x, ng, sc, sh, g2, w_gu, w_down, gf, final_norm):
    B, T, D = x.shape
    F = w_down.shape[0]
    bb, tt = _row_tiling(B, T, 512)
    tf = _pick(F, 256, LANES)
    nf = F // tf
    return pl.pallas_call(
        functools.partial(_ffn_kernel, n_col_chunks=max(1, D // 1024), final_norm=final_norm),
        grid=(B // bb, T // tt, nf),
        in_specs=[pl.BlockSpec((bb, tt, D), lambda b, t, j: (b, t, 0)),
                  pl.BlockSpec((1, 1, D), lambda b, t, j: (0, 0, 0)),
                  pl.BlockSpec((bb, 1, D), lambda b, t, j: (b, 0, 0)),
                  pl.BlockSpec((bb, 1, D), lambda b, t, j: (b, 0, 0)),
                  pl.BlockSpec((bb, 1, D), lambda b, t, j: (b, 0, 0)),
                  pl.BlockSpec((D, tf), lambda b, t, j: (0, j)),
                  pl.BlockSpec((D, tf), lambda b, t, j: (0, nf + j)),
                  pl.BlockSpec((tf, D), lambda b, t, j: (j, 0)),
                  pl.BlockSpec((1, 1, D), lambda b, t, j: (0, 0, 0))],
        out_specs=pl.BlockSpec((bb, tt, D), lambda b, t, j: (b, t, 0)),
        out_shape=jax.ShapeDtypeStruct((B, T, D), F32),
        scratch_shapes=[pltpu.VMEM((bb * tt, D), BF16)],
        compiler_params=_cparams("arbitrary", "arbitrary", "arbitrary"),
        name="ffn",
    )(x, ng, sc, sh, g2, w_gu, w_gu, w_down, gf)


def _layer(x, ada, state, p, dims):
    H, dk, dv, W = dims
    sh1, sc1, g1, sh2, sc2, g2 = ada
    c0, n0, m0, hl0, cv0 = state
    B, T, D = x.shape
    h1, gates_row = _normmod_gates(x, p["norm1_g"], sc1, sh1, p["w_gates"], p["b_gates"], H)
    proj = _matmul(h1.reshape(B * T, D), p["w_in_main"], F32).reshape(B, T, -1)
    ya, c1, n1, m1 = _mlstm(proj, gates_row, c0, n0, m0, p["head_norm_g"], n_heads=H, dk=dk, dv=dv)
    yb, hl1, cv1 = _rglru(proj, 2 * H * dk + 2 * H * dv, cv0, hl0, p["conv_w"], p["conv_b"],
                          p["w_r"], p["w_i"], p["b_r"], p["b_i"], p["lru_lambda"])
    x1 = _outproj(ya, yb, p["w_out"], x, g1)
    return x1, (sh2, sc2, g2), (c1, n1, m1, hl1, cv1)


def kernel(x_prompt, x_sample, c_prompt, c_sample, state_mlstm_C, state_mlstm_n, state_mlstm_m, state_lru_h, state_conv, w_ada, b_ada, norm1_g, norm2_g, w_in, b_gates_a, head_norm_g, conv_w, conv_b, w_r, b_r, w_i, b_i, lru_lambda, w_out, w_gu, w_down, normf_g):
    depth = w_in.shape[0]
    Bp, _, D = x_prompt.shape
    Bs = x_sample.shape[0]
    H, dk, dv = state_mlstm_C.shape[2:]
    W = conv_w.shape[2]
    taps = conv_w.shape[1]
    dims = (H, dk, dv, W)
    q_cols = H * dk
    gate_col0 = 2 * q_cols + 2 * H * dv

    xp, xs = x_prompt, x_sample
    p_states, s_states = [], []
    R = Bp + Bs
    Rpad = -(-R // SUBLANES) * SUBLANES
    c_all = jnp.concatenate([c_prompt, c_sample, jnp.zeros((Rpad - R, D), F32)], axis=0)
    gf = normf_g.reshape(1, 1, D)
    for l in range(depth):
        w_in_l = w_in[l]
        p = {
            "norm1_g": norm1_g[l].reshape(1, 1, D),
            "norm2_g": norm2_g[l].reshape(1, 1, D),
            "w_in_main": jnp.concatenate([w_in_l[:, :gate_col0], w_in_l[:, gate_col0 + 2 * H:]],
                                         axis=1).astype(BF16),
            "w_gates": jnp.pad(w_in_l[:, gate_col0:gate_col0 + 2 * H],
                               ((0, 0), (0, LANES - 2 * H))).astype(BF16),
            "b_gates": jnp.pad(b_gates_a[l].reshape(1, 2 * H), ((0, 0), (0, LANES - 2 * H))),
            "head_norm_g": head_norm_g[l],
            "conv_w": conv_w[l], "conv_b": conv_b[l],
            "w_r": w_r[l], "w_i": w_i[l], "b_r": b_r[l], "b_i": b_i[l],
            "lru_lambda": lru_lambda[l],
            "w_out": w_out[l].astype(BF16),
            "w_gu": w_gu[l].astype(BF16),
            "w_down": w_down[l].astype(BF16),
        }
        ada = _ada(c_all, w_ada[l], b_ada[l].reshape(1, -1))
        ada_p = [ada[:Bp, i * D:(i + 1) * D].reshape(Bp, 1, D) for i in range(6)]
        ada_s = [ada[Bp:R, i * D:(i + 1) * D].reshape(Bs, 1, D) for i in range(6)]

        zero_state = (jnp.zeros((Bp, H, dk, dv), F32), jnp.zeros((Bp, H, dk), F32),
                      jnp.zeros((Bp, H), F32), jnp.zeros((Bp, W), F32),
                      jnp.zeros((Bp, taps - 1, W), F32))
        cache_state = (state_mlstm_C[l], state_mlstm_n[l], state_mlstm_m[l], state_lru_h[l],
                       state_conv[l])
        last = l == depth - 1
        new = []
        for x, ada_g, st in ((xp, ada_p, zero_state), (xs, ada_s, cache_state)):
            x1, (sh2, sc2, g2), st1 = _layer(x, ada_g, st, p, dims)
            y = _ffn(x1, p["norm2_g"], sc2, sh2, g2, p["w_gu"], p["w_down"], gf, final_norm=last)
            new.append((y, st1))
        (xp, st_p), (xs, st_s) = new
        p_states.append(st_p)
        s_states.append(st_s)

    stack = lambda sts, i: jnp.stack([s[i] for s in sts])
    return (xp, xs) + tuple(stack(p_states, i) for i in range(5)) + tuple(stack(s_states, i) for i in range(5))
```

```python
import functools

import jax
import jax.numpy as jnp
from jax import lax
from jax.experimental import pallas as pl
from jax.experimental.pallas import tpu as pltpu

F32 = jnp.float32
BF16 = jnp.bfloat16
EPS = 1e-6
LRU_C = 8.0
LANES = 128
SUBLANES = 8
VMEM_LIMIT_BYTES = 58 * 1024 * 1024


def _cparams(*sem):
    return pltpu.CompilerParams(dimension_semantics=sem, vmem_limit_bytes=VMEM_LIMIT_BYTES)


def _pick(total, target, quantum):
    if total <= target:
        return total
    best = None
    for cand in range(quantum, target + 1, quantum):
        if total % cand == 0:
            best = cand
    assert best is not None, (total, target, quantum)
    return best


def _row_tiling(B, T, target_rows):
    if T >= target_rows:
        return 1, _pick(T, target_rows, 16)
    bb = _pick(B, max(1, target_rows // T), 1)
    return bb, T


def _mlstm_chunk(T):
    return _pick(T, LANES, 16)


def _log_sigmoid(x):
    return jnp.minimum(x, 0.0) - jnp.log1p(jnp.exp(-jnp.abs(x)))


def _sigmoid(x):
    return 0.5 * (1.0 + jnp.tanh(0.5 * x))


def _gelu_tanh(x):
    c = 0.7978845608028654
    return 0.5 * x * (1.0 + jnp.tanh(c * (x + 0.044715 * (x * x * x))))


def _dot(a, b):
    return jnp.dot(a, b, preferred_element_type=F32)


def _dot_nt(a, b):
    return lax.dot_general(a, b, (((1,), (1,)), ((), ())), preferred_element_type=F32)


def _dot_tn(a, b):
    return lax.dot_general(a, b, (((0,), (0,)), ((), ())), preferred_element_type=F32)


def _ada_kernel(c_ref, w_ref, b_ref, o_ref):
    c = c_ref[...]
    s = (c * _sigmoid(c)).astype(BF16)
    o_ref[...] = _dot(s, w_ref[...].astype(BF16)) + b_ref[...]


def _ada(c, w, b):
    R, D = c.shape
    N = w.shape[1]
    tn = _pick(N, 512, LANES)
    return pl.pallas_call(
        _ada_kernel,
        grid=(N // tn,),
        in_specs=[pl.BlockSpec((R, D), lambda j: (0, 0)),
                  pl.BlockSpec((D, tn), lambda j: (0, j)),
                  pl.BlockSpec((1, tn), lambda j: (0, j))],
        out_specs=pl.BlockSpec((R, tn), lambda j: (0, j)),
        out_shape=jax.ShapeDtypeStruct((R, N), F32),
        compiler_params=_cparams("arbitrary"),
        name="ada",
    )(c, w, b)


def _normmod(x, geff, sh):
    r = lax.rsqrt(jnp.mean(x * x, axis=-1, keepdims=True) + EPS)
    return x * r * geff + sh


def _normmod_gates_kernel(x_ref, g_ref, sc_ref, sh_ref, wg_ref, bg_ref, h_ref, gr_ref, *, n_heads, chunk):
    bb, tt, D = x_ref.shape
    hb = _normmod(x_ref[...], g_ref[...] * (1.0 + sc_ref[...]), sh_ref[...]).astype(BF16)
    h_ref[...] = hb
    pre = _dot(hb.reshape(bb * tt, D), wg_ref[...]) + bg_ref[...]
    lane = lax.broadcasted_iota(jnp.int32, pre.shape, 1)
    gates = jnp.where(lane < n_heads, pre, _log_sigmoid(pre))
    rows, lanes_out = gr_ref.shape[2], gr_ref.shape[3]
    if lanes_out > chunk:
        gr_ref[...] = jnp.zeros_like(gr_ref)
    for i in range(bb):
        for c in range(tt // chunk):
            r0 = i * tt + c * chunk
            gr_ref[i, c, :, 0:chunk] = gates[r0:r0 + chunk, :].T[:rows, :]


def _normmod_gates(x, g, sc, sh, wg, bg, n_heads):
    B, T, D = x.shape
    L = _mlstm_chunk(T)
    bb, tt = _row_tiling(B, T, 512)
    assert tt % L == 0
    GR = -(-2 * n_heads // SUBLANES) * SUBLANES
    LP = max(L, LANES)
    return pl.pallas_call(
        functools.partial(_normmod_gates_kernel, n_heads=n_heads, chunk=L),
        grid=(B // bb, T // tt),
        in_specs=[pl.BlockSpec((bb, tt, D), lambda b, t: (b, t, 0)),
                  pl.BlockSpec((1, 1, D), lambda b, t: (0, 0, 0)),
                  pl.BlockSpec((bb, 1, D), lambda b, t: (b, 0, 0)),
                  pl.BlockSpec((bb, 1, D), lambda b, t: (b, 0, 0)),
                  pl.BlockSpec((D, LANES), lambda b, t: (0, 0)),
                  pl.BlockSpec((1, LANES), lambda b, t: (0, 0))],
        out_specs=[pl.BlockSpec((bb, tt, D), lambda b, t: (b, t, 0)),
                   pl.BlockSpec((bb, tt // L, GR, LP), lambda b, t: (b, t, 0, 0))],
        out_shape=[jax.ShapeDtypeStruct((B, T, D), BF16),
                   jax.ShapeDtypeStruct((B, T // L, GR, LP), F32)],
        compiler_params=_cparams("arbitrary", "arbitrary"),
        name="normmod_gates",
    )(x, g, sc, sh, wg, bg)


def _inproj_kernel(a_ref, w_ref, *rest, epilogue):
    o_ref = rest[-1]
    acc = _dot(a_ref[...], w_ref[...])
    if epilogue == "colscale":
        acc = acc * rest[0][...]
    elif epilogue == "sigmoid":
        acc = _sigmoid(acc)
    elif epilogue == "gelu":
        acc = _gelu_tanh(acc)
    o_ref[...] = acc.astype(o_ref.dtype)


def _inproj(a, w, out_dtype, epilogue, colscale=None):
    M, K = a.shape
    N = w.shape[1]
    tm = _pick(M, 512, 16)
    in_specs = [pl.BlockSpec((tm, K), lambda i: (i, 0)),
                pl.BlockSpec((K, N), lambda i: (0, 0))]
    args = [a, w]
    if epilogue == "colscale":
        in_specs.append(pl.BlockSpec((1, N), lambda i: (0, 0)))
        args.append(colscale)
    return pl.pallas_call(
        functools.partial(_inproj_kernel, epilogue=epilogue),
        grid=(M // tm,),
        in_specs=in_specs,
        out_specs=pl.BlockSpec((tm, N), lambda i: (i, 0)),
        out_shape=jax.ShapeDtypeStruct((M, N), out_dtype),
        compiler_params=_cparams("arbitrary"),
        name="inproj_" + epilogue,
    )(*args)


def _lane_scan(x, op, lane, identity):
    d = 1
    while d < x.shape[1]:
        x = op(x, jnp.where(lane >= d, pltpu.roll(x, d, axis=1), identity))
        d *= 2
    return x


def _mlstm_kernel(qk_ref, v_ref, og_ref, gr_ref, c0_ref, n0_ref, m0_ref, hg_ref,
                  y_ref, c_ref, n_ref, m_ref, cn_s, m_s, *, chunk, n_heads, dk, dv):
    tb = pl.program_id(1)
    L, H = chunk, n_heads
    Tb = qk_ref.shape[1]
    nc = Tb // L

    eye_k = (lax.broadcasted_iota(jnp.int32, (dk, dk), 0) == lax.broadcasted_iota(jnp.int32, (dk, dk), 1))
    lane_k = lax.broadcasted_iota(jnp.int32, (dk, LANES), 1)

    @pl.when(tb == 0)
    def _():
        for h in range(H):
            cn_s[h, :, 0:dv] = c0_ref[0, h]
            n_col = jnp.sum(jnp.where(eye_k, n0_ref[0, h:h + 1, :], 0.0), axis=1, keepdims=True)
            cn_s[h, :, dv:dv + LANES] = jnp.where(lane_k == 0, n_col, 0.0)
        m_s[...] = jnp.broadcast_to(m0_ref[0], m_s.shape)

    row = lax.broadcasted_iota(jnp.int32, (L, L), 0)
    col = lax.broadcasted_iota(jnp.int32, (L, L), 1)
    tril = col <= row
    lane_g = lax.broadcasted_iota(jnp.int32, (H, gr_ref.shape[3]), 1)
    lane_l = lax.broadcasted_iota(jnp.int32, (L, LANES), 1)

    def chunk_step(c, carry):
        sl = pl.ds(pl.multiple_of(c * L, L), L)
        gates = gr_ref[0, tb * nc + c]
        ig, lf = gates[0:H], gates[H:2 * H]
        bcum = _lane_scan(lf, jnp.add, lane_g, 0.0)
        a = ig - bcum
        cm = _lane_scan(a, jnp.maximum, lane_g, -jnp.inf)
        cols = jnp.concatenate([a, cm, bcum], axis=0).T
        for h in range(H):
            a_row = a[h:h + 1, 0:L]
            a_col = cols[0:L, h:h + 1]
            cm_col = cols[0:L, H + h:H + h + 1]
            bcum_col = cols[0:L, 2 * H + h:2 * H + h + 1]
            m_prev = m_s[h:h + 1, 0:1]
            g = jnp.maximum(m_prev, cm_col)
            w = jnp.where(tril, jnp.exp(jnp.minimum(a_row - g, 0.0)), 0.0)
            s_inter = jnp.exp(m_prev - g)
            qb = qk_ref[0, sl, h * dk:(h + 1) * dk]
            kb = qk_ref[0, sl, (H + h) * dk:(H + h + 1) * dk]
            vb = v_ref[0, sl, h * dv:(h + 1) * dv]
            s = _dot_nt(qb, kb) * w
            cn = cn_s[h]
            qc = _dot(qb, cn.astype(BF16))
            num = s_inter * qc[:, 0:dv] + _dot(s.astype(BF16), vb)
            den = s_inter * qc[:, dv:dv + 1] + jnp.sum(s, axis=1, keepdims=True)
            hh = num / jnp.maximum(jnp.abs(den), jnp.exp(-(bcum_col + g)))
            hn = hh * lax.rsqrt(jnp.mean(hh * hh, axis=1, keepdims=True) + EPS) * hg_ref[h:h + 1, :]
            y_ref[0, sl, h * dv:(h + 1) * dv] = (hn * og_ref[0, sl, h * dv:(h + 1) * dv]).astype(y_ref.dtype)

            g_last = g[L - 1:L, :]
            w_last = jnp.exp(a_col - g_last)
            upd_rhs = jnp.concatenate([w_last * vb.astype(F32), jnp.where(lane_l == 0, w_last, 0.0)], axis=1)
            cn_s[h] = jnp.exp(m_prev - g_last) * cn + _dot_tn(kb, upd_rhs.astype(BF16))
            m_s[h:h + 1, :] = jnp.broadcast_to(bcum_col[L - 1:L, :] + g_last, (1, m_s.shape[1]))
        return carry

    lax.fori_loop(0, nc, chunk_step, 0)

    @pl.when(tb == pl.num_programs(1) - 1)
    def _():
        for h in range(H):
            c_ref[0, h] = cn_s[h, :, 0:dv]
            n_ref[0, h:h + 1, :] = jnp.sum(jnp.where(eye_k, cn_s[h, :, dv:dv + 1], 0.0), axis=0, keepdims=True)
        m_ref[0] = m_s[...]


def _mlstm(qk, v, og, gates, c0, n0, m0, head_g):
    B, T, _ = qk.shape
    _, H, dk, dv = c0.shape
    L = _mlstm_chunk(T)
    Tb = _pick(T, 512, L)
    _, NC, GR, LP = gates.shape
    y, c1, n1, m1 = pl.pallas_call(
        functools.partial(_mlstm_kernel, chunk=L, n_heads=H, dk=dk, dv=dv),
        grid=(B, T // Tb),
        in_specs=[pl.BlockSpec((1, Tb, 2 * H * dk), lambda b, t: (b, t, 0)),
                  pl.BlockSpec((1, Tb, H * dv), lambda b, t: (b, t, 0)),
                  pl.BlockSpec((1, Tb, H * dv), lambda b, t: (b, t, 0)),
                  pl.BlockSpec((1, NC, GR, LP), lambda b, t: (b, 0, 0, 0)),
                  pl.BlockSpec((1, H, dk, dv), lambda b, t: (b, 0, 0, 0)),
                  pl.BlockSpec((1, H, dk), lambda b, t: (b, 0, 0)),
                  pl.BlockSpec((1, H, 1), lambda b, t: (b, 0, 0)),
                  pl.BlockSpec((H, dv), lambda b, t: (0, 0))],
        out_specs=[pl.BlockSpec((1, Tb, H * dv), lambda b, t: (b, t, 0)),
                   pl.BlockSpec((1, H, dk, dv), lambda b, t: (b, 0, 0, 0)),
                   pl.BlockSpec((1, H, dk), lambda b, t: (b, 0, 0)),
                   pl.BlockSpec((1, H, LANES), lambda b, t: (b, 0, 0))],
        out_shape=[jax.ShapeDtypeStruct((B, T, H * dv), BF16),
                   jax.ShapeDtypeStruct((B, H, dk, dv), F32),
                   jax.ShapeDtypeStruct((B, H, dk), F32),
                   jax.ShapeDtypeStruct((B, H, LANES), F32)],
        scratch_shapes=[pltpu.VMEM((H, dk, dv + LANES), F32),
                        pltpu.VMEM((H, LANES), F32)],
        compiler_params=_cparams("arbitrary", "arbitrary"),
        name="mlstm",
    )(qk, v, og, gates, c0, n0, m0.reshape(B, H, 1), head_g)
    return y, c1, n1, m1[:, :, 0]


def _scan_pitch(rows):
    quads = -(-(-(-rows // SUBLANES)) // 4)
    return 4 * (quads + 1 - quads % 2)


def _rglru_kernel(xb_ref, gg_ref, cv0_ref, h0_ref, cw_ref, cb_ref, wr_ref, wi_ref, br_ref, bi_ref,
                  lam_ref, y_ref, h_ref, cv_ref, xext, a_s, u_s, *, pitch, group):
    tb = pl.program_id(1)
    Tb, W = xb_ref.shape[1], xb_ref.shape[2]
    nb, bw = wr_ref.shape[0], wr_ref.shape[1]
    taps = cw_ref.shape[0]
    rows = a_s.shape[1]
    pad = SUBLANES

    @pl.when(tb == 0)
    def _():
        h_ref[...] = h0_ref[...]
        cv_ref[...] = cv0_ref[...]

    x = xb_ref[0]
    xext[pl.ds(pad, Tb), :] = x
    xext[pl.ds(pad - (taps - 1), taps - 1), :] = cv_ref[0]
    xc = cb_ref[...] + cw_ref[taps - 1:taps, :] * x
    for j in range(taps - 1):
        xc = xc + cw_ref[j:j + 1, :] * xext[pl.ds(pad - (taps - 1) + j, Tb), :]
    cv_ref[0] = xext[pl.ds(pad + Tb - (taps - 1), taps - 1), :]

    log_sig_lam = _log_sigmoid(lam_ref[...])
    for n in range(nb):
        cols = slice(n * bw, (n + 1) * bw)
        xn = xc[:, cols]
        xnb = xn.astype(BF16)
        r = _sigmoid(_dot(xnb, wr_ref[n]) + br_ref[:, cols])
        i = _sigmoid(_dot(xnb, wi_ref[n]) + bi_ref[:, cols])
        log_a = LRU_C * r * log_sig_lam[:, cols]
        a = jnp.exp(log_a)
        a_s[n, 0:Tb, :] = a
        u_s[n, 0:Tb, :] = jnp.sqrt(-jnp.tanh(log_a) * (1.0 + a * a)) * (i * xn)
        if rows > Tb:
            a_s[n, Tb:rows, :] = jnp.ones((rows - Tb, bw), F32)
            u_s[n, Tb:rows, :] = jnp.zeros((rows - Tb, bw), F32)

    for n0 in range(0, nb, group):
        slabs = list(range(n0, min(n0 + group, nb)))

        def pass1(i, carry):
            idx = pl.ds(i, SUBLANES, stride=pitch)
            out = []
            for (hz, ac), n in zip(carry, slabs):
                av = a_s[n, idx, :]
                hz = av * hz + u_s[n, idx, :]
                ac = av * ac
                u_s[n, idx, :] = hz
                a_s[n, idx, :] = ac
                out.append((hz, ac))
            return tuple(out)

        init = tuple((jnp.zeros((SUBLANES, bw), F32), jnp.ones((SUBLANES, bw), F32)) for _ in slabs)
        ends = lax.fori_loop(0, pitch, pass1, init, unroll=2)

        starts = []
        for (hz, ac), n in zip(ends, slabs):
            cols = slice(n * bw, (n + 1) * bw)
            h = h_ref[0, :, cols]
            per_block = []
            for j in range(SUBLANES):
                per_block.append(h)
                h = hz[j:j + 1, :] + ac[j:j + 1, :] * h
            h_ref[0, :, cols] = h
            starts.append(jnp.concatenate(per_block, axis=0))

        def pass2(i, carry):
            idx = pl.ds(i, SUBLANES, stride=pitch)
            for st, n in zip(starts, slabs):
                u_s[n, idx, :] = u_s[n, idx, :] + a_s[n, idx, :] * st
            return carry

        lax.fori_loop(0, pitch, pass2, 0, unroll=2)

    for n in range(nb):
        cols = slice(n * bw, (n + 1) * bw)
        y_ref[0, :, cols] = (u_s[n, 0:Tb, :] * gg_ref[0, :, cols]).astype(y_ref.dtype)


def _rglru(xb, gg, cv0, h0, conv_w, conv_b, w_r, w_i, b_r, b_i, lam):
    B, T, W = xb.shape
    taps = conv_w.shape[0]
    nb, bw = w_r.shape[0], w_r.shape[1]
    assert T >= taps - 1 and bw == LANES and nb * bw == W
    Tb = _pick(T, 256, SUBLANES)
    pitch = _scan_pitch(Tb)
    row = lambda a: a.reshape(1, W)
    y, h1, cv1 = pl.pallas_call(
        functools.partial(_rglru_kernel, pitch=pitch, group=8),
        grid=(B, T // Tb),
        in_specs=[pl.BlockSpec((1, Tb, W), lambda b, t: (b, t, 0)),
                  pl.BlockSpec((1, Tb, W), lambda b, t: (b, t, 0)),
                  pl.BlockSpec((1, taps - 1, W), lambda b, t: (b, 0, 0)),
                  pl.BlockSpec((1, 1, W), lambda b, t: (b, 0, 0)),
                  pl.BlockSpec((taps, W), lambda b, t: (0, 0)),
                  pl.BlockSpec((1, W), lambda b, t: (0, 0)),
                  pl.BlockSpec((nb, bw, bw), lambda b, t: (0, 0, 0)),
                  pl.BlockSpec((nb, bw, bw), lambda b, t: (0, 0, 0)),
                  pl.BlockSpec((1, W), lambda b, t: (0, 0)),
                  pl.BlockSpec((1, W), lambda b, t: (0, 0)),
                  pl.BlockSpec((1, W), lambda b, t: (0, 0))],
        out_specs=[pl.BlockSpec((1, Tb, W), lambda b, t: (b, t, 0)),
                   pl.BlockSpec((1, 1, W), lambda b, t: (b, 0, 0)),
                   pl.BlockSpec((1, taps - 1, W), lambda b, t: (b, 0, 0))],
        out_shape=[jax.ShapeDtypeStruct((B, T, W), BF16),
                   jax.ShapeDtypeStruct((B, 1, W), F32),
                   jax.ShapeDtypeStruct((B, taps - 1, W), F32)],
        scratch_shapes=[pltpu.VMEM((Tb + SUBLANES, W), F32),
                        pltpu.VMEM((nb, SUBLANES * pitch, bw), F32),
                        pltpu.VMEM((nb, SUBLANES * pitch, bw), F32)],
        compiler_params=_cparams("arbitrary", "arbitrary"),
        name="rglru",
    )(xb, gg, cv0, h0.reshape(B, 1, W), conv_w, row(conv_b), w_r.astype(BF16), w_i.astype(BF16),
      row(b_r), row(b_i), row(lam))
    return y, h1.reshape(B, W), cv1


def _outproj_kernel(ya_ref, yb_ref, wt_ref, wb_ref, x_ref, g_ref, o_ref):
    bb, tt, wa = ya_ref.shape
    wb = yb_ref.shape[2]
    mix = (_dot(ya_ref[...].reshape(bb * tt, wa), wt_ref[...])
           + _dot(yb_ref[...].reshape(bb * tt, wb), wb_ref[...]))
    o_ref[...] = x_ref[...] + g_ref[...] * mix.reshape(bb, tt, mix.shape[-1])


def _outproj(ya, yb, w_out, x, g1):
    B, T, D = x.shape
    wa, wb = ya.shape[2], yb.shape[2]
    assert wa == wb
    bb, tt = _row_tiling(B, T, 1024)
    tn = _pick(D, 1024, LANES)
    return pl.pallas_call(
        _outproj_kernel,
        grid=(B // bb, T // tt, D // tn),
        in_specs=[pl.BlockSpec((bb, tt, wa), lambda b, t, j: (b, t, 0)),
                  pl.BlockSpec((bb, tt, wb), lambda b, t, j: (b, t, 0)),
                  pl.BlockSpec((wa, tn), lambda b, t, j: (0, j)),
                  pl.BlockSpec((wb, tn), lambda b, t, j: (1, j)),
                  pl.BlockSpec((bb, tt, tn), lambda b, t, j: (b, t, j)),
                  pl.BlockSpec((bb, 1, tn), lambda b, t, j: (b, 0, j))],
        out_specs=pl.BlockSpec((bb, tt, tn), lambda b, t, j: (b, t, j)),
        out_shape=jax.ShapeDtypeStruct((B, T, D), F32),
        compiler_params=_cparams("arbitrary", "arbitrary", "arbitrary"),
        name="outproj",
    )(ya, yb, w_out, w_out, x, g1)


def _ffn_kernel(x_ref, ng_ref, sc_ref, sh_ref, g2_ref, wgu_ref, wd_ref, gf_ref, o_ref, h_s,
                *, n_col_chunks, final_norm):
    j = pl.program_id(2)
    bb, tt, D = x_ref.shape
    tf = wd_ref.shape[0]

    @pl.when(j == 0)
    def _():
        h = _normmod(x_ref[...], ng_ref[...] * (1.0 + sc_ref[...]), sh_ref[...])
        h_s[...] = h.astype(BF16).reshape(bb * tt, D)
        o_ref[...] = jnp.zeros_like(o_ref)

    gu = _dot(h_s[...], wgu_ref[0])
    gate, up = gu[:, 0:tf], gu[:, tf:2 * tf]
    act = (gate * _sigmoid(gate) * up).astype(BF16)
    cw = D // n_col_chunks
    for c in range(n_col_chunks):
        cols = slice(c * cw, (c + 1) * cw)
        o_ref[:, :, cols] += _dot(act, wd_ref[:, cols]).reshape(bb, tt, cw)

    @pl.when(j == pl.num_programs(2) - 1)
    def _():
        x2 = x_ref[...] + g2_ref[...] * o_ref[...]
        if final_norm:
            x2 = x2 * lax.rsqrt(jnp.mean(x2 * x2, axis=-1, keepdims=True) + EPS) * gf_ref[...]
        o_ref[...] = x2


def _ffn_tile(F):
    return _pick(F, 256, LANES)


def _ffn(x, ng, sc, sh, g2, w_gu_tiles, w_down, gf, final_norm):
    B, T, D = x.shape
    nf, _, tf2 = w_gu_tiles.shape
    tf = tf2 // 2
    bb, tt = _row_tiling(B, T, 512)
    return pl.pallas_call(
        functools.partial(_ffn_kernel, n_col_chunks=max(1, D // 1024), final_norm=final_norm),
        grid=(B // bb, T // tt, nf),
        in_specs=[pl.BlockSpec((bb, tt, D), lambda b, t, j: (b, t, 0)),
                  pl.BlockSpec((1, 1, D), lambda b, t, j: (0, 0, 0)),
                  pl.BlockSpec((bb, 1, D), lambda b, t, j: (b, 0, 0)),
                  pl.BlockSpec((bb, 1, D), lambda b, t, j: (b, 0, 0)),
                  pl.BlockSpec((bb, 1, D), lambda b, t, j: (b, 0, 0)),
                  pl.BlockSpec((1, D, tf2), lambda b, t, j: (j, 0, 0)),
                  pl.BlockSpec((tf, D), lambda b, t, j: (j, 0)),
                  pl.BlockSpec((1, 1, D), lambda b, t, j: (0, 0, 0))],
        out_specs=pl.BlockSpec((bb, tt, D), lambda b, t, j: (b, t, 0)),
        out_shape=jax.ShapeDtypeStruct((B, T, D), F32),
        scratch_shapes=[pltpu.VMEM((bb * tt, D), BF16)],
        compiler_params=_cparams("arbitrary", "arbitrary", "arbitrary"),
        name="ffn",
    )(x, ng, sc, sh, g2, w_gu_tiles, w_down, gf)


def _mixer(x, ada, state, p):
    sh1, sc1, g1 = ada
    c0, n0, m0, hl0, cv0 = state
    B, T, D = x.shape
    H = c0.shape[1]
    h1, gates = _normmod_gates(x, p["norm1_g"], sc1, sh1, p["w_gates"], p["b_gates"], H)
    h1 = h1.reshape(B * T, D)
    proj = lambda w, dt, ep, cs=None: _inproj(h1, w, dt, ep, cs).reshape(B, T, -1)
    qk = proj(p["w_qk"], BF16, "colscale", p["qk_scale"])
    v = proj(p["w_v"], BF16, "none")
    og = proj(p["w_o"], F32, "sigmoid")
    xb = proj(p["w_xb"], F32, "none")
    gg = proj(p["w_gb"], F32, "gelu")
    ya, c1, n1, m1 = _mlstm(qk, v, og, gates, c0, n0, m0, p["head_norm_g"])
    yb, hl1, cv1 = _rglru(xb, gg, cv0, hl0, p["conv_w"], p["conv_b"],
                          p["w_r"], p["w_i"], p["b_r"], p["b_i"], p["lru_lambda"])
    return _outproj(ya, yb, p["w_out"], x, g1), (c1, n1, m1, hl1, cv1)


def kernel(x_prompt, x_sample, c_prompt, c_sample, state_mlstm_C, state_mlstm_n, state_mlstm_m, state_lru_h, state_conv, w_ada, b_ada, norm1_g, norm2_g, w_in, b_gates_a, head_norm_g, conv_w, conv_b, w_r, b_r, w_i, b_i, lru_lambda, w_out, w_gu, w_down, normf_g):
    depth = w_in.shape[0]
    Bp, _, D = x_prompt.shape
    Bs = x_sample.shape[0]
    H, dk, dv = state_mlstm_C.shape[2:]
    W = conv_w.shape[2]
    taps = conv_w.shape[1]
    F = w_down.shape[1]
    tf = _ffn_tile(F)
    c_v = 2 * H * dk
    c_o = c_v + H * dv
    c_g = c_o + H * dv
    c_x = c_g + 2 * H
    c_b = c_x + W

    xp, xs = x_prompt, x_sample
    p_states, s_states = [], []
    R = Bp + Bs
    Rpad = -(-R // SUBLANES) * SUBLANES
    c_all = jnp.concatenate([c_prompt, c_sample, jnp.zeros((Rpad - R, D), F32)], axis=0)
    gf = normf_g.reshape(1, 1, D)
    qk_scale = jnp.concatenate([jnp.ones((1, H * dk), F32), jnp.full((1, H * dk), dk ** -0.5, F32)], axis=1)
    for l in range(depth):
        w_in_l = w_in[l]
        w_gu_l = w_gu[l]
        gu_tiles = jnp.concatenate([w_gu_l[:, :F].reshape(D, F // tf, tf),
                                    w_gu_l[:, F:].reshape(D, F // tf, tf)], axis=2)
        p = {
            "norm1_g": norm1_g[l].reshape(1, 1, D),
            "w_qk": w_in_l[:, :c_v].astype(BF16),
            "qk_scale": qk_scale,
            "w_v": w_in_l[:, c_v:c_o].astype(BF16),
            "w_o": w_in_l[:, c_o:c_g].astype(BF16),
            "w_gates": jnp.pad(w_in_l[:, c_g:c_x], ((0, 0), (0, LANES - 2 * H))).astype(BF16),
            "b_gates": jnp.pad(b_gates_a[l].reshape(1, 2 * H), ((0, 0), (0, LANES - 2 * H))),
            "w_xb": w_in_l[:, c_x:c_b].astype(BF16),
            "w_gb": w_in_l[:, c_b:].astype(BF16),
            "head_norm_g": head_norm_g[l],
            "conv_w": conv_w[l], "conv_b": conv_b[l],
            "w_r": w_r[l], "w_i": w_i[l], "b_r": b_r[l], "b_i": b_i[l],
            "lru_lambda": lru_lambda[l],
            "w_out": w_out[l].astype(BF16),
        }
        w_gu_tiles = jnp.transpose(gu_tiles, (1, 0, 2)).astype(BF16)
        w_down_l = w_down[l].astype(BF16)
        norm2 = norm2_g[l].reshape(1, 1, D)
        ada = _ada(c_all, w_ada[l], b_ada[l].reshape(1, -1))
        ada_p = [ada[:Bp, i * D:(i + 1) * D].reshape(Bp, 1, D) for i in range(6)]
        ada_s = [ada[Bp:R, i * D:(i + 1) * D].reshape(Bs, 1, D) for i in range(6)]

        zero_state = (jnp.zeros((Bp, H, dk, dv), F32), jnp.zeros((Bp, H, dk), F32),
                      jnp.zeros((Bp, H), F32), jnp.zeros((Bp, W), F32),
                      jnp.zeros((Bp, taps - 1, W), F32))
        cache_state = (state_mlstm_C[l], state_mlstm_n[l], state_mlstm_m[l], state_lru_h[l],
                       state_conv[l])
        last = l == depth - 1
        new = []
        for x, (sh1, sc1, g1, sh2, sc2, g2), st in ((xp, ada_p, zero_state), (xs, ada_s, cache_state)):
            x1, st1 = _mixer(x, (sh1, sc1, g1), st, p)
            y = _ffn(x1, norm2, sc2, sh2, g2, w_gu_tiles, w_down_l, gf, final_norm=last)
            new.append((y, st1))
        (xp, st_p), (xs, st_s) = new
        p_states.append(st_p)
        s_states.append(st_s)

    stack = lambda sts, i: jnp.stack([s[i] for s in sts])
    return (xp, xs) + tuple(stack(p_states, i) for i in range(5)) + tuple(stack(s_states, i) for i in range(5))
```

```python
import functools

import jax
import jax.numpy as jnp
from jax import lax
from jax.experimental import pallas as pl
from jax.experimental.pallas import tpu as pltpu

F32 = jnp.float32
BF16 = jnp.bfloat16
EPS = 1e-6
LRU_C = 8.0
LANES = 128
SUBLANES = 8
VMEM_LIMIT_BYTES = 58 * 1024 * 1024


def _cparams(*sem):
    return pltpu.CompilerParams(dimension_semantics=sem, vmem_limit_bytes=VMEM_LIMIT_BYTES)


def _pick(total, target, quantum):
    if total <= target:
        return total
    best = None
    for cand in range(quantum, target + 1, quantum):
        if total % cand == 0:
            best = cand
    assert best is not None, (total, target, quantum)
    return best


def _row_tiling(B, T, target_rows):
    if T >= target_rows:
        return 1, _pick(T, target_rows, 16)
    bb = _pick(B, max(1, target_rows // T), 1)
    return bb, T


def _mlstm_chunk(T):
    return _pick(T, LANES, 16)


def _log_sigmoid(x):
    return jnp.minimum(x, 0.0) - jnp.log1p(jnp.exp(-jnp.abs(x)))


def _sigmoid(x):
    return 0.5 * (1.0 + jnp.tanh(0.5 * x))


def _gelu_tanh(x):
    c = 0.7978845608028654
    return 0.5 * x * (1.0 + jnp.tanh(c * (x + 0.044715 * (x * x * x))))


def _dot(a, b):
    return jnp.dot(a, b, preferred_element_type=F32)


def _dot_nt(a, b):
    return lax.dot_general(a, b, (((1,), (1,)), ((), ())), preferred_element_type=F32)


def _dot_tn(a, b):
    return lax.dot_general(a, b, (((0,), (0,)), ((), ())), preferred_element_type=F32)


def _split_bf16(x, terms):
    out = []
    for _ in range(terms - 1):
        t = x.astype(BF16)
        out.append(t)
        x = x - t.astype(F32)
    out.append(x.astype(BF16))
    return out


def _ada_kernel(c_ref, w_ref, b_ref, o_ref):
    c = c_ref[...]
    s = (c * _sigmoid(c)).astype(BF16)
    o_ref[...] = _dot(s, w_ref[...].astype(BF16)) + b_ref[...]


def _ada(c, w, b):
    R, D = c.shape
    N = w.shape[1]
    tn = _pick(N, 512, LANES)
    return pl.pallas_call(
        _ada_kernel,
        grid=(N // tn,),
        in_specs=[pl.BlockSpec((R, D), lambda j: (0, 0)),
                  pl.BlockSpec((D, tn), lambda j: (0, j)),
                  pl.BlockSpec((1, tn), lambda j: (0, j))],
        out_specs=pl.BlockSpec((R, tn), lambda j: (0, j)),
        out_shape=jax.ShapeDtypeStruct((R, N), F32),
        compiler_params=_cparams("arbitrary"),
        name="ada",
    )(c, w, b)


def _normmod(x, geff, sh):
    r = lax.rsqrt(jnp.mean(x * x, axis=-1, keepdims=True) + EPS)
    return x * r * geff + sh


def _normmod_gates_kernel(x_ref, g_ref, sc_ref, sh_ref, wg_ref, bg_ref, h_ref, gr_ref, *, n_heads, chunk):
    bb, tt, D = x_ref.shape
    hb = _normmod(x_ref[...], g_ref[...] * (1.0 + sc_ref[...]), sh_ref[...]).astype(BF16)
    h_ref[...] = hb
    pre = _dot(hb.reshape(bb * tt, D), wg_ref[...]) + bg_ref[...]
    lane = lax.broadcasted_iota(jnp.int32, pre.shape, 1)
    gates = jnp.where(lane < n_heads, pre, _log_sigmoid(pre))
    rows, lanes_out = gr_ref.shape[2], gr_ref.shape[3]
    if lanes_out > chunk:
        gr_ref[...] = jnp.zeros_like(gr_ref)
    for i in range(bb):
        for c in range(tt // chunk):
            r0 = i * tt + c * chunk
            gr_ref[i, c, :, 0:chunk] = gates[r0:r0 + chunk, :].T[:rows, :]


def _normmod_gates(x, g, sc, sh, wg, gate_blk, bg, n_heads):
    B, T, D = x.shape
    L = _mlstm_chunk(T)
    bb, tt = _row_tiling(B, T, 512)
    assert tt % L == 0
    GR = -(-2 * n_heads // SUBLANES) * SUBLANES
    LP = max(L, LANES)
    return pl.pallas_call(
        functools.partial(_normmod_gates_kernel, n_heads=n_heads, chunk=L),
        grid=(B // bb, T // tt),
        in_specs=[pl.BlockSpec((bb, tt, D), lambda b, t: (b, t, 0)),
                  pl.BlockSpec((1, 1, D), lambda b, t: (0, 0, 0)),
                  pl.BlockSpec((bb, 1, D), lambda b, t: (b, 0, 0)),
                  pl.BlockSpec((bb, 1, D), lambda b, t: (b, 0, 0)),
                  pl.BlockSpec((D, LANES), lambda b, t: (0, gate_blk)),
                  pl.BlockSpec((1, LANES), lambda b, t: (0, 0))],
        out_specs=[pl.BlockSpec((bb, tt, D), lambda b, t: (b, t, 0)),
                   pl.BlockSpec((bb, tt // L, GR, LP), lambda b, t: (b, t, 0, 0))],
        out_shape=[jax.ShapeDtypeStruct((B, T, D), BF16),
                   jax.ShapeDtypeStruct((B, T // L, GR, LP), F32)],
        compiler_params=_cparams("arbitrary", "arbitrary"),
        name="normmod_gates",
    )(x, g, sc, sh, wg, bg)


def _inproj_kernel(a_ref, w_ref, *rest, epilogue):
    o_ref = rest[-1]
    acc = _dot(a_ref[...], w_ref[...])
    if epilogue == "colscale":
        acc = acc * rest[0][...]
    elif epilogue == "sigmoid":
        acc = _sigmoid(acc)
    elif epilogue == "gelu":
        acc = _gelu_tanh(acc)
    o_ref[...] = acc.astype(o_ref.dtype)


def _inproj(a, w, col_block, N, out_dtype, epilogue, colscale=None):
    M, K = a.shape
    tm = _pick(M, 512, 16)
    in_specs = [pl.BlockSpec((tm, K), lambda i: (i, 0)),
                pl.BlockSpec((K, N), lambda i: (0, col_block))]
    args = [a, w]
    if epilogue == "colscale":
        in_specs.append(pl.BlockSpec((1, N), lambda i: (0, 0)))
        args.append(colscale)
    return pl.pallas_call(
        functools.partial(_inproj_kernel, epilogue=epilogue),
        grid=(M // tm,),
        in_specs=in_specs,
        out_specs=pl.BlockSpec((tm, N), lambda i: (i, 0)),
        out_shape=jax.ShapeDtypeStruct((M, N), out_dtype),
        compiler_params=_cparams("arbitrary"),
        name="inproj_" + epilogue,
    )(*args)


def _lane_scan(x, op, lane, identity):
    d = 1
    while d < x.shape[1]:
        x = op(x, jnp.where(lane >= d, pltpu.roll(x, d, axis=1), identity))
        d *= 2
    return x


def _mlstm_kernel(qk_ref, v_ref, og_ref, gr_ref, c0_ref, n0_ref, m0_ref, hg_ref,
                  y_ref, c_ref, n_ref, m_ref, cn_s, m_s, *, chunk, n_heads, dk, dv):
    tb = pl.program_id(1)
    L, H = chunk, n_heads
    Tb = qk_ref.shape[1]
    nc = Tb // L
    LP = gr_ref.shape[3]
    assert dv % LANES == 0 and L <= LANES
    rep = lambda x, n: jnp.concatenate([x] * n, axis=1)

    eye_k = (lax.broadcasted_iota(jnp.int32, (dk, dk), 0) == lax.broadcasted_iota(jnp.int32, (dk, dk), 1))

    @pl.when(tb == 0)
    def _():
        for h in range(H):
            cn_s[h, :, 0:dv] = c0_ref[0, h]
            n_col = jnp.sum(jnp.where(eye_k, n0_ref[0, h:h + 1, :], 0.0), axis=1, keepdims=True)
            cn_s[h, :, dv:dv + LANES] = jnp.broadcast_to(n_col, (dk, LANES))
        m_s[...] = jnp.broadcast_to(m0_ref[0], m_s.shape)

    row = lax.broadcasted_iota(jnp.int32, (L, L), 0)
    col = lax.broadcasted_iota(jnp.int32, (L, L), 1)
    tril = col <= row
    lane_g = lax.broadcasted_iota(jnp.int32, (H, LP), 1)
    nr = 3 * H
    spread = ((lax.broadcasted_iota(jnp.int32, (3 * nr, nr * LANES), 1) // LANES)
              == (lax.broadcasted_iota(jnp.int32, (3 * nr, nr * LANES), 0) % nr)).astype(BF16)
    ones_2l = jnp.ones((2 * L, LANES), BF16)
    mean_2v = jnp.full((2 * dv, LANES), 1.0 / dv, BF16)
    assert dv & (dv - 1) == 0

    def chunk_step(c, carry):
        sl = pl.ds(pl.multiple_of(c * L, L), L)
        gates = gr_ref[0, tb * nc + c]
        ig, lf = gates[0:H], gates[H:2 * H]
        bcum = _lane_scan(lf, jnp.add, lane_g, 0.0)
        a = ig - bcum
        cm = _lane_scan(a, jnp.maximum, lane_g, -jnp.inf)
        rows3 = jnp.concatenate(_split_bf16(jnp.concatenate([a, cm, bcum], axis=0), 3), axis=0)
        cols = _dot_tn(rows3, spread)
        blk = lambda r: cols[0:L, r * LANES:(r + 1) * LANES]
        m_all = m_s[...]
        nv = dv // LANES
        heads = range(H)
        qb = [qk_ref[0, sl, h * dk:(h + 1) * dk] for h in heads]
        kb = [qk_ref[0, sl, (H + h) * dk:(H + h + 1) * dk] for h in heads]
        vb = [v_ref[0, sl, h * dv:(h + 1) * dv] for h in heads]
        cn = [cn_s[h] for h in heads]
        qk = [_dot_nt(qb[h], kb[h]) for h in heads]
        qc = [_dot(qb[h], cn[h].astype(BF16)) for h in heads]

        g, s_inter, sv, rs = [], [], [], []
        for h in heads:
            g.append(jnp.maximum(m_all[h:h + 1, :], blk(H + h)))
            s_inter.append(jnp.exp(m_all[h:h + 1, :] - g[h]))
            w = jnp.where(tril, jnp.exp(jnp.minimum(a[h:h + 1, 0:L] - g[h][:, 0:L], 0.0)), 0.0)
            s2 = _split_bf16(qk[h] * w, 2)
            sv.append(_dot(s2[0], vb[h]))
            rs.append(_dot(jnp.concatenate(s2, axis=1), ones_2l))

        hh, ms = [], []
        for h in heads:
            num = rep(s_inter[h], nv) * qc[h][:, 0:dv] + sv[h]
            den = s_inter[h] * qc[h][:, dv:dv + LANES] + rs[h]
            inv = 1.0 / jnp.maximum(jnp.abs(den), jnp.exp(-(blk(2 * H + h) + g[h])))
            hh.append(num * rep(inv, nv))
            ms.append(_dot(jnp.concatenate(_split_bf16(hh[h] * hh[h], 2), axis=1), mean_2v))

        m_new = []
        for h in heads:
            hn = hh[h] * rep(lax.rsqrt(ms[h] + EPS), nv) * hg_ref[h:h + 1, :]
            y_ref[0, sl, h * dv:(h + 1) * dv] = (hn * og_ref[0, sl, h * dv:(h + 1) * dv]).astype(y_ref.dtype)
            g_last = g[h][L - 1:L, :]
            w_last = jnp.exp(blk(h) - g_last)
            upd_rhs = jnp.concatenate([rep(w_last, nv) * vb[h].astype(F32), w_last], axis=1)
            decay = jnp.exp(m_all[h:h + 1, :] - g_last)
            cn_s[h] = rep(decay, nv + 1) * cn[h] + _dot_tn(kb[h], upd_rhs.astype(BF16))
            m_new.append(blk(2 * H + h)[L - 1:L, :] + g_last)
        m_s[...] = jnp.concatenate(m_new, axis=0)
        return carry

    lax.fori_loop(0, nc, chunk_step, 0)

    @pl.when(tb == pl.num_programs(1) - 1)
    def _():
        for h in range(H):
            c_ref[0, h] = cn_s[h, :, 0:dv]
            n_ref[0, h:h + 1, :] = jnp.sum(jnp.where(eye_k, cn_s[h, :, dv:dv + 1], 0.0), axis=0, keepdims=True)
        m_ref[0] = m_s[...]


def _mlstm(qk, v, og, gates, c0, n0, m0, head_g):
    B, T, _ = qk.shape
    _, H, dk, dv = c0.shape
    L = _mlstm_chunk(T)
    Tb = _pick(T, 512, L)
    _, NC, GR, LP = gates.shape
    y, c1, n1, m1 = pl.pallas_call(
        functools.partial(_mlstm_kernel, chunk=L, n_heads=H, dk=dk, dv=dv),
        grid=(B, T // Tb),
        in_specs=[pl.BlockSpec((1, Tb, 2 * H * dk), lambda b, t: (b, t, 0)),
                  pl.BlockSpec((1, Tb, H * dv), lambda b, t: (b, t, 0)),
                  pl.BlockSpec((1, Tb, H * dv), lambda b, t: (b, t, 0)),
                  pl.BlockSpec((1, NC, GR, LP), lambda b, t: (b, 0, 0, 0)),
                  pl.BlockSpec((1, H, dk, dv), lambda b, t: (b, 0, 0, 0)),
                  pl.BlockSpec((1, H, dk), lambda b, t: (b, 0, 0)),
                  pl.BlockSpec((1, H, 1), lambda b, t: (b, 0, 0)),
                  pl.BlockSpec((H, dv), lambda b, t: (0, 0))],
        out_specs=[pl.BlockSpec((1, Tb, H * dv), lambda b, t: (b, t, 0)),
                   pl.BlockSpec((1, H, dk, dv), lambda b, t: (b, 0, 0, 0)),
                   pl.BlockSpec((1, H, dk), lambda b, t: (b, 0, 0)),
                   pl.BlockSpec((1, H, LANES), lambda b, t: (b, 0, 0))],
        out_shape=[jax.ShapeDtypeStruct((B, T, H * dv), BF16),
                   jax.ShapeDtypeStruct((B, H, dk, dv), F32),
                   jax.ShapeDtypeStruct((B, H, dk), F32),
                   jax.ShapeDtypeStruct((B, H, LANES), F32)],
        scratch_shapes=[pltpu.VMEM((H, dk, dv + LANES), F32),
                        pltpu.VMEM((H, LANES), F32)],
        compiler_params=_cparams("arbitrary", "arbitrary"),
        name="mlstm",
    )(qk, v, og, gates, c0, n0, m0.reshape(B, H, 1), head_g)
    return y, c1, n1, m1[:, :, 0]


def _scan_pitch(rows):
    quads = -(-(-(-rows // SUBLANES)) // 4)
    return 4 * (quads + 1 - quads % 2)


def _rglru_kernel(xb_ref, gg_ref, cv0_ref, h0_ref, cw_ref, cb_ref, wr_ref, wi_ref, br_ref, bi_ref,
                  lam_ref, y_ref, h_ref, cv_ref, xext, a_s, u_s, *, pitch, group):
    tb = pl.program_id(1)
    Tb, W = xb_ref.shape[1], xb_ref.shape[2]
    nb, bw = wr_ref.shape[0], wr_ref.shape[1]
    taps = cw_ref.shape[0]
    rows = a_s.shape[1]
    pad = SUBLANES

    @pl.when(tb == 0)
    def _():
        h_ref[...] = h0_ref[...]
        cv_ref[...] = cv0_ref[...]

    x = xb_ref[0]
    xext[pl.ds(pad, Tb), :] = x
    xext[pl.ds(pad - (taps - 1), taps - 1), :] = cv_ref[0]
    xc = cb_ref[...] + cw_ref[taps - 1:taps, :] * x
    for j in range(taps - 1):
        xc = xc + cw_ref[j:j + 1, :] * xext[pl.ds(pad - (taps - 1) + j, Tb), :]
    cv_ref[0] = xext[pl.ds(pad + Tb - (taps - 1), taps - 1), :]

    log_sig_lam = _log_sigmoid(lam_ref[...])
    for n in range(nb):
        cols = slice(n * bw, (n + 1) * bw)
        xn = xc[:, cols]
        xnb = xn.astype(BF16)
        r = _sigmoid(_dot(xnb, wr_ref[n]) + br_ref[:, cols])
        i = _sigmoid(_dot(xnb, wi_ref[n]) + bi_ref[:, cols])
        log_a = LRU_C * r * log_sig_lam[:, cols]
        a = jnp.exp(log_a)
        a_s[n, 0:Tb, :] = a
        u_s[n, 0:Tb, :] = jnp.sqrt(-jnp.tanh(log_a) * (1.0 + a * a)) * (i * xn)
        if rows > Tb:
            a_s[n, Tb:rows, :] = jnp.ones((rows - Tb, bw), F32)
            u_s[n, Tb:rows, :] = jnp.zeros((rows - Tb, bw), F32)

    for n0 in range(0, nb, group):
        slabs = list(range(n0, min(n0 + group, nb)))

        def pass1(i, carry):
            idx = pl.ds(i, SUBLANES, stride=pitch)
            out = []
            for (hz, ac), n in zip(carry, slabs):
                av = a_s[n, idx, :]
                hz = av * hz + u_s[n, idx, :]
                ac = av * ac
                u_s[n, idx, :] = hz
                a_s[n, idx, :] = ac
                out.append((hz, ac))
            return tuple(out)

        init = tuple((jnp.zeros((SUBLANES, bw), F32), jnp.ones((SUBLANES, bw), F32)) for _ in slabs)
        ends = lax.fori_loop(0, pitch, pass1, init, unroll=2)

        starts = []
        for (hz, ac), n in zip(ends, slabs):
            cols = slice(n * bw, (n + 1) * bw)
            h = h_ref[0, :, cols]
            per_block = []
            for j in range(SUBLANES):
                per_block.append(h)
                h = hz[j:j + 1, :] + ac[j:j + 1, :] * h
            h_ref[0, :, cols] = h
            starts.append(jnp.concatenate(per_block, axis=0))

        def pass2(i, carry):
            idx = pl.ds(i, SUBLANES, stride=pitch)
            for st, n in zip(starts, slabs):
                u_s[n, idx, :] = u_s[n, idx, :] + a_s[n, idx, :] * st
            return carry

        lax.fori_loop(0, pitch, pass2, 0, unroll=2)

    for n in range(nb):
        cols = slice(n * bw, (n + 1) * bw)
        y_ref[0, :, cols] = (u_s[n, 0:Tb, :] * gg_ref[0, :, cols]).astype(y_ref.dtype)


def _rglru(xb, gg, cv0, h0, conv_w, conv_b, w_r, w_i, b_r, b_i, lam):
    B, T, W = xb.shape
    taps = conv_w.shape[0]
    nb, bw = w_r.shape[0], w_r.shape[1]
    assert T >= taps - 1 and bw == LANES and nb * bw == W
    Tb = _pick(T, 256, SUBLANES)
    pitch = _scan_pitch(Tb)
    row = lambda a: a.reshape(1, W)
    y, h1, cv1 = pl.pallas_call(
        functools.partial(_rglru_kernel, pitch=pitch, group=8),
        grid=(B, T // Tb),
        in_specs=[pl.BlockSpec((1, Tb, W), lambda b, t: (b, t, 0)),
                  pl.BlockSpec((1, Tb, W), lambda b, t: (b, t, 0)),
                  pl.BlockSpec((1, taps - 1, W), lambda b, t: (b, 0, 0)),
                  pl.BlockSpec((1, 1, W), lambda b, t: (b, 0, 0)),
                  pl.BlockSpec((taps, W), lambda b, t: (0, 0)),
                  pl.BlockSpec((1, W), lambda b, t: (0, 0)),
                  pl.BlockSpec((nb, bw, bw), lambda b, t: (0, 0, 0)),
                  pl.BlockSpec((nb, bw, bw), lambda b, t: (0, 0, 0)),
                  pl.BlockSpec((1, W), lambda b, t: (0, 0)),
                  pl.BlockSpec((1, W), lambda b, t: (0, 0)),
                  pl.BlockSpec((1, W), lambda b, t: (0, 0))],
        out_specs=[pl.BlockSpec((1, Tb, W), lambda b, t: (b, t, 0)),
                   pl.BlockSpec((1, 1, W), lambda b, t: (b, 0, 0)),
                   pl.BlockSpec((1, taps - 1, W), lambda b, t: (b, 0, 0))],
        out_shape=[jax.ShapeDtypeStruct((B, T, W), BF16),
                   jax.ShapeDtypeStruct((B, 1, W), F32),
                   jax.ShapeDtypeStruct((B, taps - 1, W), F32)],
        scratch_shapes=[pltpu.VMEM((Tb + SUBLANES, W), F32),
                        pltpu.VMEM((nb, SUBLANES * pitch, bw), F32),
                        pltpu.VMEM((nb, SUBLANES * pitch, bw), F32)],
        compiler_params=_cparams("arbitrary", "arbitrary"),
        name="rglru",
    )(xb, gg, cv0, h0.reshape(B, 1, W), conv_w, row(conv_b), w_r.astype(BF16), w_i.astype(BF16),
      row(b_r), row(b_i), row(lam))
    return y, h1.reshape(B, W), cv1


def _outproj_kernel(ya_ref, yb_ref, wt_ref, wb_ref, x_ref, g_ref, o_ref):
    bb, tt, wa = ya_ref.shape
    wb = yb_ref.shape[2]
    mix = (_dot(ya_ref[...].reshape(bb * tt, wa), wt_ref[...])
           + _dot(yb_ref[...].reshape(bb * tt, wb), wb_ref[...]))
    o_ref[...] = x_ref[...] + g_ref[...] * mix.reshape(bb, tt, mix.shape[-1])


def _outproj(ya, yb, w_out, x, g1):
    B, T, D = x.shape
    wa, wb = ya.shape[2], yb.shape[2]
    assert wa == wb
    bb, tt = _row_tiling(B, T, 1024)
    tn = _pick(D, 1024, LANES)
    return pl.pallas_call(
        _outproj_kernel,
        grid=(B // bb, T // tt, D // tn),
        in_specs=[pl.BlockSpec((bb, tt, wa), lambda b, t, j: (b, t, 0)),
                  pl.BlockSpec((bb, tt, wb), lambda b, t, j: (b, t, 0)),
                  pl.BlockSpec((wa, tn), lambda b, t, j: (0, j)),
                  pl.BlockSpec((wb, tn), lambda b, t, j: (1, j)),
                  pl.BlockSpec((bb, tt, tn), lambda b, t, j: (b, t, j)),
                  pl.BlockSpec((bb, 1, tn), lambda b, t, j: (b, 0, j))],
        out_specs=pl.BlockSpec((bb, tt, tn), lambda b, t, j: (b, t, j)),
        out_shape=jax.ShapeDtypeStruct((B, T, D), F32),
        compiler_params=_cparams("arbitrary", "arbitrary", "arbitrary"),
        name="outproj",
    )(ya, yb, w_out, w_out, x, g1)


def _ffn_kernel(x_ref, ng_ref, sc_ref, sh_ref, g2_ref, wg_ref, wu_ref, wd_ref, gf_ref, o_ref, h_s,
                *, n_col_chunks, final_norm):
    j = pl.program_id(2)
    bb, tt, D = x_ref.shape

    @pl.when(j == 0)
    def _():
        h = _normmod(x_ref[...], ng_ref[...] * (1.0 + sc_ref[...]), sh_ref[...])
        h_s[...] = h.astype(BF16).reshape(bb * tt, D)
        o_ref[...] = jnp.zeros_like(o_ref)

    hb = h_s[...]
    gate = _dot(hb, wg_ref[...])
    up = _dot(hb, wu_ref[...])
    act = (gate * _sigmoid(gate) * up).astype(BF16)
    cw = D // n_col_chunks
    for c in range(n_col_chunks):
        cols = slice(c * cw, (c + 1) * cw)
        o_ref[:, :, cols] += _dot(act, wd_ref[:, cols]).reshape(bb, tt, cw)

    @pl.when(j == pl.num_programs(2) - 1)
    def _():
        x2 = x_ref[...] + g2_ref[...] * o_ref[...]
        if final_norm:
            x2 = x2 * lax.rsqrt(jnp.mean(x2 * x2, axis=-1, keepdims=True) + EPS) * gf_ref[...]
        o_ref[...] = x2


def _ffn(x, ng, sc, sh, g2, w_gu, w_down, gf, final_norm):
    B, T, D = x.shape
    F = w_down.shape[0]
    tf = _pick(F, 256, LANES)
    nf = F // tf
    bb, tt = _row_tiling(B, T, 512)
    return pl.pallas_call(
        functools.partial(_ffn_kernel, n_col_chunks=max(1, D // 1024), final_norm=final_norm),
        grid=(B // bb, T // tt, nf),
        in_specs=[pl.BlockSpec((bb, tt, D), lambda b, t, j: (b, t, 0)),
                  pl.BlockSpec((1, 1, D), lambda b, t, j: (0, 0, 0)),
                  pl.BlockSpec((bb, 1, D), lambda b, t, j: (b, 0, 0)),
                  pl.BlockSpec((bb, 1, D), lambda b, t, j: (b, 0, 0)),
                  pl.BlockSpec((bb, 1, D), lambda b, t, j: (b, 0, 0)),
                  pl.BlockSpec((D, tf), lambda b, t, j: (0, j)),
                  pl.BlockSpec((D, tf), lambda b, t, j: (0, nf + j)),
                  pl.BlockSpec((tf, D), lambda b, t, j: (j, 0)),
                  pl.BlockSpec((1, 1, D), lambda b, t, j: (0, 0, 0))],
        out_specs=pl.BlockSpec((bb, tt, D), lambda b, t, j: (b, t, 0)),
        out_shape=jax.ShapeDtypeStruct((B, T, D), F32),
        scratch_shapes=[pltpu.VMEM((bb * tt, D), BF16)],
        compiler_params=_cparams("arbitrary", "arbitrary", "arbitrary"),
        name="ffn",
    )(x, ng, sc, sh, g2, w_gu, w_gu, w_down, gf)


def _mixer(x, ada, state, p):
    sh1, sc1, g1 = ada
    c0, n0, m0, hl0, cv0 = state
    B, T, D = x.shape
    _, H, dk, dv = c0.shape
    W = cv0.shape[2]
    n_qk, n_v = 2 * H * dk, H * dv
    assert n_qk == n_v and (3 * n_v) % LANES == 0
    h1, gates = _normmod_gates(x, p["norm1_g"], sc1, sh1, p["w_in"], 3 * n_v // LANES, p["b_gates"], H)
    h1 = h1.reshape(B * T, D)
    proj = lambda w, blk, n, dt, ep, cs=None: _inproj(h1, w, blk, n, dt, ep, cs).reshape(B, T, n)
    qk = proj(p["w_in"], 0, n_qk, BF16, "colscale", p["qk_scale"])
    v = proj(p["w_in"], 1, n_v, BF16, "none")
    og = proj(p["w_in"], 2, n_v, F32, "sigmoid")
    xb = proj(p["w_xg"], 0, W, F32, "none")
    gg = proj(p["w_xg"], 1, W, F32, "gelu")
    ya, c1, n1, m1 = _mlstm(qk, v, og, gates, c0, n0, m0, p["head_norm_g"])
    yb, hl1, cv1 = _rglru(xb, gg, cv0, hl0, p["conv_w"], p["conv_b"],
                          p["w_r"], p["w_i"], p["b_r"], p["b_i"], p["lru_lambda"])
    return _outproj(ya, yb, p["w_out"], x, g1), (c1, n1, m1, hl1, cv1)


def kernel(x_prompt, x_sample, c_prompt, c_sample, state_mlstm_C, state_mlstm_n, state_mlstm_m, state_lru_h, state_conv, w_ada, b_ada, norm1_g, norm2_g, w_in, b_gates_a, head_norm_g, conv_w, conv_b, w_r, b_r, w_i, b_i, lru_lambda, w_out, w_gu, w_down, normf_g):
    depth = w_in.shape[0]
    Bp, _, D = x_prompt.shape
    Bs = x_sample.shape[0]
    H, dk, dv = state_mlstm_C.shape[2:]
    W = conv_w.shape[2]
    taps = conv_w.shape[1]
    c_x = 2 * H * dk + 2 * H * dv + 2 * H

    xp, xs = x_prompt, x_sample
    p_states, s_states = [], []
    R = Bp + Bs
    Rpad = -(-R // SUBLANES) * SUBLANES
    c_all = jnp.concatenate([c_prompt, c_sample, jnp.zeros((Rpad - R, D), F32)], axis=0)
    gf = normf_g.reshape(1, 1, D)
    qk_scale = jnp.concatenate([jnp.ones((1, H * dk), F32), jnp.full((1, H * dk), dk ** -0.5, F32)], axis=1)
    for l in range(depth):
        w_in_l = w_in[l]
        p = {
            "norm1_g": norm1_g[l].reshape(1, 1, D),
            "w_in": w_in_l.astype(BF16),
            "w_xg": w_in_l[:, c_x:].astype(BF16),
            "qk_scale": qk_scale,
            "b_gates": jnp.pad(b_gates_a[l].reshape(1, 2 * H), ((0, 0), (0, LANES - 2 * H))),
            "head_norm_g": head_norm_g[l],
            "conv_w": conv_w[l], "conv_b": conv_b[l],
            "w_r": w_r[l], "w_i": w_i[l], "b_r": b_r[l], "b_i": b_i[l],
            "lru_lambda": lru_lambda[l],
            "w_out": w_out[l].astype(BF16),
        }
        w_gu_l = w_gu[l].astype(BF16)
        w_down_l = w_down[l].astype(BF16)
        norm2 = norm2_g[l].reshape(1, 1, D)
        ada = _ada(c_all, w_ada[l], b_ada[l].reshape(1, -1))
        ada_p = [ada[:Bp, i * D:(i + 1) * D].reshape(Bp, 1, D) for i in range(6)]
        ada_s = [ada[Bp:R, i * D:(i + 1) * D].reshape(Bs, 1, D) for i in range(6)]

        zero_state = (jnp.zeros((Bp, H, dk, dv), F32), jnp.zeros((Bp, H, dk), F32),
                      jnp.zeros((Bp, H), F32), jnp.zeros((Bp, W), F32),
                      jnp.zeros((Bp, taps - 1, W), F32))
        cache_state = (state_mlstm_C[l], state_mlstm_n[l], state_mlstm_m[l], state_lru_h[l],
                       state_conv[l])
        last = l == depth - 1
        new = []
        for x, (sh1, sc1, g1, sh2, sc2, g2), st in ((xp, ada_p, zero_state), (xs, ada_s, cache_state)):
            x1, st1 = _mixer(x, (sh1, sc1, g1), st, p)
            y = _ffn(x1, norm2, sc2, sh2, g2, w_gu_l, w_down_l, gf, final_norm=last)
            new.append((y, st1))
        (xp, st_p), (xs, st_s) = new
        p_states.append(st_p)
        s_states.append(st_s)

    stack = lambda sts, i: jnp.stack([s[i] for s in sts])
    return (xp, xs) + tuple(stack(p_states, i) for i in range(5)) + tuple(stack(s_states, i) for i in range(5))
```

```python
import functools

import jax
import jax.numpy as jnp
from jax import lax
from jax.experimental import pallas as pl
from jax.experimental.pallas import tpu as pltpu

F32 = jnp.float32
BF16 = jnp.bfloat16
EPS = 1e-6
LRU_C = 8.0
LANES = 128
SUBLANES = 8
VMEM_LIMIT_BYTES = 58 * 1024 * 1024


def _cparams(*sem):
    return pltpu.CompilerParams(dimension_semantics=sem, vmem_limit_bytes=VMEM_LIMIT_BYTES)


def _pick(total, target, quantum):
    if total <= target:
        return total
    best = None
    for cand in range(quantum, target + 1, quantum):
        if total % cand == 0:
            best = cand
    assert best is not None, (total, target, quantum)
    return best


def _row_tiling(B, T, target_rows):
    if T >= target_rows:
        return 1, _pick(T, target_rows, 16)
    bb = _pick(B, max(1, target_rows // T), 1)
    return bb, T


def _mlstm_chunk(T):
    return _pick(T, LANES, 16)


def _log_sigmoid(x):
    return jnp.minimum(x, 0.0) - jnp.log1p(jnp.exp(-jnp.abs(x)))


def _sigmoid(x):
    return 0.5 * (1.0 + jnp.tanh(0.5 * x))


def _gelu_tanh(x):
    c = 0.7978845608028654
    return 0.5 * x * (1.0 + jnp.tanh(c * (x + 0.044715 * (x * x * x))))


def _dot(a, b):
    return jnp.dot(a, b, preferred_element_type=F32)


def _dot_nt(a, b):
    return lax.dot_general(a, b, (((1,), (1,)), ((), ())), preferred_element_type=F32)


def _dot_tn(a, b):
    return lax.dot_general(a, b, (((0,), (0,)), ((), ())), preferred_element_type=F32)


def _split_bf16(x, terms):
    out = []
    for _ in range(terms - 1):
        t = x.astype(BF16)
        out.append(t)
        x = x - t.astype(F32)
    out.append(x.astype(BF16))
    return out


def _ada_kernel(c_ref, w_ref, b_ref, o_ref):
    c = c_ref[...]
    s = (c * _sigmoid(c)).astype(BF16)
    o_ref[...] = _dot(s, w_ref[...].astype(BF16)) + b_ref[...]


def _ada(c, w, b):
    R, D = c.shape
    N = w.shape[1]
    tn = _pick(N, 512, LANES)
    return pl.pallas_call(
        _ada_kernel,
        grid=(N // tn,),
        in_specs=[pl.BlockSpec((R, D), lambda j: (0, 0)),
                  pl.BlockSpec((D, tn), lambda j: (0, j)),
                  pl.BlockSpec((1, tn), lambda j: (0, j))],
        out_specs=pl.BlockSpec((R, tn), lambda j: (0, j)),
        out_shape=jax.ShapeDtypeStruct((R, N), F32),
        compiler_params=_cparams("arbitrary"),
        name="ada",
    )(c, w, b)


def _normmod(x, geff, sh):
    r = lax.rsqrt(jnp.mean(x * x, axis=-1, keepdims=True) + EPS)
    return x * r * geff + sh


def _normmod_gates_kernel(x_ref, g_ref, sc_ref, sh_ref, wg_ref, bg_ref, h_ref, gr_ref, *, n_heads, chunk):
    bb, tt, D = x_ref.shape
    hb = _normmod(x_ref[...], g_ref[...] * (1.0 + sc_ref[...]), sh_ref[...]).astype(BF16)
    h_ref[...] = hb
    pre = _dot(hb.reshape(bb * tt, D), wg_ref[...]) + bg_ref[...]
    lane = lax.broadcasted_iota(jnp.int32, pre.shape, 1)
    gates = jnp.where(lane < n_heads, pre, _log_sigmoid(pre))
    rows, lanes_out = gr_ref.shape[2], gr_ref.shape[3]
    if lanes_out > chunk:
        gr_ref[...] = jnp.zeros_like(gr_ref)
    for i in range(bb):
        for c in range(tt // chunk):
            r0 = i * tt + c * chunk
            gr_ref[i, c, :, 0:chunk] = gates[r0:r0 + chunk, :].T[:rows, :]


def _normmod_gates(x, g, sc, sh, wg, gate_blk, bg, n_heads):
    B, T, D = x.shape
    L = _mlstm_chunk(T)
    bb, tt = _row_tiling(B, T, 512)
    assert tt % L == 0
    GR = -(-2 * n_heads // SUBLANES) * SUBLANES
    LP = max(L, LANES)
    return pl.pallas_call(
        functools.partial(_normmod_gates_kernel, n_heads=n_heads, chunk=L),
        grid=(B // bb, T // tt),
        in_specs=[pl.BlockSpec((bb, tt, D), lambda b, t: (b, t, 0)),
                  pl.BlockSpec((1, 1, D), lambda b, t: (0, 0, 0)),
                  pl.BlockSpec((bb, 1, D), lambda b, t: (b, 0, 0)),
                  pl.BlockSpec((bb, 1, D), lambda b, t: (b, 0, 0)),
                  pl.BlockSpec((D, LANES), lambda b, t: (0, gate_blk)),
                  pl.BlockSpec((1, LANES), lambda b, t: (0, 0))],
        out_specs=[pl.BlockSpec((bb, tt, D), lambda b, t: (b, t, 0)),
                   pl.BlockSpec((bb, tt // L, GR, LP), lambda b, t: (b, t, 0, 0))],
        out_shape=[jax.ShapeDtypeStruct((B, T, D), BF16),
                   jax.ShapeDtypeStruct((B, T // L, GR, LP), F32)],
        compiler_params=_cparams("arbitrary", "arbitrary"),
        name="normmod_gates",
    )(x, g, sc, sh, wg, bg)


def _inproj_kernel(a_ref, w_ref, *rest, epilogue, with_cast):
    rest = list(rest)
    if with_cast:
        cast_out = rest.pop()
        o_ref = rest.pop()
        cast_out[...] = rest.pop()[...].astype(BF16)
    else:
        o_ref = rest.pop()
    acc = _dot(a_ref[...], w_ref[...])
    if epilogue == "colscale":
        acc = acc * rest[0][...]
    elif epilogue == "sigmoid":
        acc = _sigmoid(acc)
    elif epilogue == "gelu":
        acc = _gelu_tanh(acc)
    o_ref[...] = acc.astype(o_ref.dtype)


def _inproj_tile(M):
    return _pick(M, 512, 16)


def _cast_job(w, steps, col_blocks=1, col_block=0):
    R, C = w.shape
    Cb = C // col_blocks
    if C % col_blocks or (col_blocks > 1 and Cb % LANES):
        return None
    if R % (steps * 2 * SUBLANES) == 0:
        return w, (R // steps, Cb), (lambda i: (i, col_block)), (lambda i: (i, 0)), (R, Cb)
    if col_blocks == 1 and C % (steps * LANES) == 0:
        return w, (R, C // steps), (lambda i: (0, i)), (lambda i: (0, i)), (R, C)
    return None


def _inproj(a, w, col_block, N, out_dtype, epilogue, colscale=None, cast=None):
    M, K = a.shape
    tm = _inproj_tile(M)
    in_specs = [pl.BlockSpec((tm, K), lambda i: (i, 0)),
                pl.BlockSpec((K, N), lambda i: (0, col_block), pipeline_mode=pl.Buffered(1))]
    args = [a, w]
    if epilogue == "colscale":
        in_specs.append(pl.BlockSpec((1, N), lambda i: (0, 0)))
        args.append(colscale)
    out_specs = [pl.BlockSpec((tm, N), lambda i: (i, 0))]
    out_shape = [jax.ShapeDtypeStruct((M, N), out_dtype)]
    if cast is not None:
        src, blk, src_map, dst_map, shape = cast
        in_specs.append(pl.BlockSpec(blk, src_map))
        args.append(src)
        out_specs.append(pl.BlockSpec(blk, dst_map))
        out_shape.append(jax.ShapeDtypeStruct(shape, BF16))
    outs = pl.pallas_call(
        functools.partial(_inproj_kernel, epilogue=epilogue, with_cast=cast is not None),
        grid=(M // tm,),
        in_specs=in_specs,
        out_specs=out_specs,
        out_shape=out_shape,
        compiler_params=_cparams("arbitrary"),
        name="inproj_" + epilogue,
    )(*args)
    return outs if cast is not None else outs[0]


def _lane_scan(x, op, lane, identity):
    d = 1
    while d < x.shape[1]:
        x = op(x, jnp.where(lane >= d, pltpu.roll(x, d, axis=1), identity))
        d *= 2
    return x


def _mlstm_kernel(qk_ref, v_ref, og_ref, gr_ref, c0_ref, n0_ref, m0_ref, hg_ref,
                  y_ref, c_ref, n_ref, m_ref, cn_s, m_s, *, chunk, n_heads, dk, dv, head_group):
    tb = pl.program_id(1)
    L, H = chunk, n_heads
    Tb = qk_ref.shape[1]
    nc = Tb // L
    LP = gr_ref.shape[3]
    assert dv % LANES == 0 and L <= LANES
    rep = lambda x, n: jnp.concatenate([x] * n, axis=1)

    eye_k = (lax.broadcasted_iota(jnp.int32, (dk, dk), 0) == lax.broadcasted_iota(jnp.int32, (dk, dk), 1))

    @pl.when(tb == 0)
    def _():
        for h in range(H):
            cn_s[h, :, 0:dv] = c0_ref[0, h]
            n_col = jnp.sum(jnp.where(eye_k, n0_ref[0, h:h + 1, :], 0.0), axis=1, keepdims=True)
            cn_s[h, :, dv:dv + LANES] = jnp.broadcast_to(n_col, (dk, LANES))
        m_s[...] = jnp.broadcast_to(m0_ref[0], m_s.shape)

    row = lax.broadcasted_iota(jnp.int32, (L, L), 0)
    col = lax.broadcasted_iota(jnp.int32, (L, L), 1)
    tril = col <= row
    lane_g = lax.broadcasted_iota(jnp.int32, (H, LP), 1)
    nr = 3 * H
    spread = ((lax.broadcasted_iota(jnp.int32, (3 * nr, nr * LANES), 1) // LANES)
              == (lax.broadcasted_iota(jnp.int32, (3 * nr, nr * LANES), 0) % nr)).astype(BF16)
    ones_2l = jnp.ones((2 * L, LANES), BF16)
    mean_2v = jnp.full((2 * dv, LANES), 1.0 / dv, BF16)
    assert dv & (dv - 1) == 0

    def chunk_step(c, carry):
        sl = pl.ds(pl.multiple_of(c * L, L), L)
        gates = gr_ref[0, tb * nc + c]
        ig, lf = gates[0:H], gates[H:2 * H]
        bcum = _lane_scan(lf, jnp.add, lane_g, 0.0)
        a = ig - bcum
        cm = _lane_scan(a, jnp.maximum, lane_g, -jnp.inf)
        rows3 = jnp.concatenate(_split_bf16(jnp.concatenate([a, cm, bcum], axis=0), 3), axis=0)
        cols = _dot_tn(rows3, spread)
        blk = lambda r: cols[0:L, r * LANES:(r + 1) * LANES]
        m_all = m_s[...]
        nv = dv // LANES
        m_new = {}
        for h0 in range(0, H, head_group):
            heads = range(h0, min(h0 + head_group, H))
            qb = {h: qk_ref[0, sl, h * dk:(h + 1) * dk] for h in heads}
            kb = {h: qk_ref[0, sl, (H + h) * dk:(H + h + 1) * dk] for h in heads}
            vb = {h: v_ref[0, sl, h * dv:(h + 1) * dv] for h in heads}
            cn = {h: cn_s[h] for h in heads}
            qk = {h: _dot_nt(qb[h], kb[h]) for h in heads}
            qc = {h: _dot(qb[h], cn[h].astype(BF16)) for h in heads}

            g, s_inter, sv, rs = {}, {}, {}, {}
            for h in heads:
                g[h] = jnp.maximum(m_all[h:h + 1, :], blk(H + h))
                s_inter[h] = jnp.exp(m_all[h:h + 1, :] - g[h])
                w = jnp.where(tril, jnp.exp(jnp.minimum(a[h:h + 1, 0:L] - g[h][:, 0:L], 0.0)), 0.0)
                s2 = _split_bf16(qk[h] * w, 2)
                sv[h] = _dot(s2[0], vb[h])
                rs[h] = _dot(jnp.concatenate(s2, axis=1), ones_2l)

            hh, ms = {}, {}
            for h in heads:
                num = rep(s_inter[h], nv) * qc[h][:, 0:dv] + sv[h]
                den = s_inter[h] * qc[h][:, dv:dv + LANES] + rs[h]
                inv = 1.0 / jnp.maximum(jnp.abs(den), jnp.exp(-(blk(2 * H + h) + g[h])))
                hh[h] = num * rep(inv, nv)
                ms[h] = _dot(jnp.concatenate(_split_bf16(hh[h] * hh[h], 2), axis=1), mean_2v)

            for h in heads:
                hn = hh[h] * rep(lax.rsqrt(ms[h] + EPS), nv) * hg_ref[h:h + 1, :]
                y_ref[0, sl, h * dv:(h + 1) * dv] = (hn * og_ref[0, sl, h * dv:(h + 1) * dv]).astype(y_ref.dtype)
                g_last = g[h][L - 1:L, :]
                w_last = jnp.exp(blk(h) - g_last)
                upd_rhs = jnp.concatenate([rep(w_last, nv) * vb[h].astype(F32), w_last], axis=1)
                decay = jnp.exp(m_all[h:h + 1, :] - g_last)
                cn_s[h] = rep(decay, nv + 1) * cn[h] + _dot_tn(kb[h], upd_rhs.astype(BF16))
                m_new[h] = blk(2 * H + h)[L - 1:L, :] + g_last
        m_s[...] = jnp.concatenate([m_new[h] for h in range(H)], axis=0)
        return carry

    lax.fori_loop(0, nc, chunk_step, 0)

    @pl.when(tb == pl.num_programs(1) - 1)
    def _():
        for h in range(H):
            c_ref[0, h] = cn_s[h, :, 0:dv]
            n_ref[0, h:h + 1, :] = jnp.sum(jnp.where(eye_k, cn_s[h, :, dv:dv + 1], 0.0), axis=0, keepdims=True)
        m_ref[0] = m_s[...]


def _mlstm(qk, v, og, gates, c0, n0, m0, head_g):
    B, T, _ = qk.shape
    _, H, dk, dv = c0.shape
    L = _mlstm_chunk(T)
    Tb = _pick(T, 512, L)
    _, NC, GR, LP = gates.shape
    y, c1, n1, m1 = pl.pallas_call(
        functools.partial(_mlstm_kernel, chunk=L, n_heads=H, dk=dk, dv=dv, head_group=4),
        grid=(B, T // Tb),
        in_specs=[pl.BlockSpec((1, Tb, 2 * H * dk), lambda b, t: (b, t, 0)),
                  pl.BlockSpec((1, Tb, H * dv), lambda b, t: (b, t, 0)),
                  pl.BlockSpec((1, Tb, H * dv), lambda b, t: (b, t, 0)),
                  pl.BlockSpec((1, NC, GR, LP), lambda b, t: (b, 0, 0, 0)),
                  pl.BlockSpec((1, H, dk, dv), lambda b, t: (b, 0, 0, 0)),
                  pl.BlockSpec((1, H, dk), lambda b, t: (b, 0, 0)),
                  pl.BlockSpec((1, H, 1), lambda b, t: (b, 0, 0)),
                  pl.BlockSpec((H, dv), lambda b, t: (0, 0))],
        out_specs=[pl.BlockSpec((1, Tb, H * dv), lambda b, t: (b, t, 0)),
                   pl.BlockSpec((1, H, dk, dv), lambda b, t: (b, 0, 0, 0)),
                   pl.BlockSpec((1, H, dk), lambda b, t: (b, 0, 0)),
                   pl.BlockSpec((1, H, LANES), lambda b, t: (b, 0, 0))],
        out_shape=[jax.ShapeDtypeStruct((B, T, H * dv), BF16),
                   jax.ShapeDtypeStruct((B, H, dk, dv), F32),
                   jax.ShapeDtypeStruct((B, H, dk), F32),
                   jax.ShapeDtypeStruct((B, H, LANES), F32)],
        scratch_shapes=[pltpu.VMEM((H, dk, dv + LANES), F32),
                        pltpu.VMEM((H, LANES), F32)],
        compiler_params=_cparams("arbitrary", "arbitrary"),
        name="mlstm",
    )(qk, v, og, gates, c0, n0, m0.reshape(B, H, 1), head_g)
    return y, c1, n1, m1[:, :, 0]


def _scan_pitch(rows):
    quads = -(-(-(-rows // SUBLANES)) // 4)
    return 4 * (quads + 1 - quads % 2)


def _rglru_kernel(xb_ref, gg_ref, cv0_ref, h0_ref, cw_ref, cb_ref, wr_ref, wi_ref, br_ref, bi_ref,
                  lam_ref, y_ref, h_ref, cv_ref, xext, a_s, u_s, *, pitch, group):
    tb = pl.program_id(1)
    Tb, W = xb_ref.shape[1], xb_ref.shape[2]
    nb, bw = wr_ref.shape[0], wr_ref.shape[1]
    taps = cw_ref.shape[0]
    rows = a_s.shape[1]
    pad = SUBLANES

    @pl.when(tb == 0)
    def _():
        h_ref[...] = h0_ref[...]
        cv_ref[...] = cv0_ref[...]

    x = xb_ref[0]
    xext[pl.ds(pad, Tb), :] = x
    xext[pl.ds(pad - (taps - 1), taps - 1), :] = cv_ref[0]
    xc = cb_ref[...] + cw_ref[taps - 1:taps, :] * x
    for j in range(taps - 1):
        xc = xc + cw_ref[j:j + 1, :] * xext[pl.ds(pad - (taps - 1) + j, Tb), :]
    cv_ref[0] = xext[pl.ds(pad + Tb - (taps - 1), taps - 1), :]

    log_sig_lam = _log_sigmoid(lam_ref[...])
    for n in range(nb):
        cols = slice(n * bw, (n + 1) * bw)
        xn = xc[:, cols]
        xnb = xn.astype(BF16)
        r = _sigmoid(_dot(xnb, wr_ref[n]) + br_ref[:, cols])
        i = _sigmoid(_dot(xnb, wi_ref[n]) + bi_ref[:, cols])
        log_a = LRU_C * r * log_sig_lam[:, cols]
        a = jnp.exp(log_a)
        a_s[n, 0:Tb, :] = a
        u_s[n, 0:Tb, :] = jnp.sqrt(-jnp.tanh(log_a) * (1.0 + a * a)) * (i * xn)
        if rows > Tb:
            a_s[n, Tb:rows, :] = jnp.ones((rows - Tb, bw), F32)
            u_s[n, Tb:rows, :] = jnp.zeros((rows - Tb, bw), F32)

    for n0 in range(0, nb, group):
        slabs = list(range(n0, min(n0 + group, nb)))

        def pass1(i, carry):
            idx = pl.ds(i, SUBLANES, stride=pitch)
            out = []
            for (hz, ac), n in zip(carry, slabs):
                av = a_s[n, idx, :]
                hz = av * hz + u_s[n, idx, :]
                ac = av * ac
                u_s[n, idx, :] = hz
                a_s[n, idx, :] = ac
                out.append((hz, ac))
            return tuple(out)

        init = tuple((jnp.zeros((SUBLANES, bw), F32), jnp.ones((SUBLANES, bw), F32)) for _ in slabs)
        ends = lax.fori_loop(0, pitch, pass1, init, unroll=2)

        starts = []
        for (hz, ac), n in zip(ends, slabs):
            cols = slice(n * bw, (n + 1) * bw)
            h = h_ref[0, :, cols]
            per_block = []
            for j in range(SUBLANES):
                per_block.append(h)
                h = hz[j:j + 1, :] + ac[j:j + 1, :] * h
            h_ref[0, :, cols] = h
            starts.append(jnp.concatenate(per_block, axis=0))

        def pass2(i, carry):
            idx = pl.ds(i, SUBLANES, stride=pitch)
            for st, n in zip(starts, slabs):
                u_s[n, idx, :] = u_s[n, idx, :] + a_s[n, idx, :] * st
            return carry

        lax.fori_loop(0, pitch, pass2, 0, unroll=2)

    for n in range(nb):
        cols = slice(n * bw, (n + 1) * bw)
        y_ref[0, :, cols] = (u_s[n, 0:Tb, :] * gg_ref[0, :, cols]).astype(y_ref.dtype)


def _rglru(xb, gg, cv0, h0, conv_w, conv_b, w_r, w_i, b_r, b_i, lam):
    B, T, W = xb.shape
    taps = conv_w.shape[0]
    nb, bw = w_r.shape[0], w_r.shape[1]
    assert T >= taps - 1 and bw == LANES and nb * bw == W
    Tb = _pick(T, 256, SUBLANES)
    pitch = _scan_pitch(Tb)
    row = lambda a: a.reshape(1, W)
    y, h1, cv1 = pl.pallas_call(
        functools.partial(_rglru_kernel, pitch=pitch, group=8),
        grid=(B, T // Tb),
        in_specs=[pl.BlockSpec((1, Tb, W), lambda b, t: (b, t, 0)),
                  pl.BlockSpec((1, Tb, W), lambda b, t: (b, t, 0)),
                  pl.BlockSpec((1, taps - 1, W), lambda b, t: (b, 0, 0)),
                  pl.BlockSpec((1, 1, W), lambda b, t: (b, 0, 0)),
                  pl.BlockSpec((taps, W), lambda b, t: (0, 0)),
                  pl.BlockSpec((1, W), lambda b, t: (0, 0)),
                  pl.BlockSpec((nb, bw, bw), lambda b, t: (0, 0, 0)),
                  pl.BlockSpec((nb, bw, bw), lambda b, t: (0, 0, 0)),
                  pl.BlockSpec((1, W), lambda b, t: (0, 0)),
                  pl.BlockSpec((1, W), lambda b, t: (0, 0)),
                  pl.BlockSpec((1, W), lambda b, t: (0, 0))],
        out_specs=[pl.BlockSpec((1, Tb, W), lambda b, t: (b, t, 0)),
                   pl.BlockSpec((1, 1, W), lambda b, t: (b, 0, 0)),
                   pl.BlockSpec((1, taps - 1, W), lambda b, t: (b, 0, 0))],
        out_shape=[jax.ShapeDtypeStruct((B, T, W), BF16),
                   jax.ShapeDtypeStruct((B, 1, W), F32),
                   jax.ShapeDtypeStruct((B, taps - 1, W), F32)],
        scratch_shapes=[pltpu.VMEM((Tb + SUBLANES, W), F32),
                        pltpu.VMEM((nb, SUBLANES * pitch, bw), F32),
                        pltpu.VMEM((nb, SUBLANES * pitch, bw), F32)],
        compiler_params=_cparams("arbitrary", "arbitrary"),
        name="rglru",
    )(xb, gg, cv0, h0.reshape(B, 1, W), conv_w, row(conv_b), w_r.astype(BF16), w_i.astype(BF16),
      row(b_r), row(b_i), row(lam))
    return y, h1.reshape(B, W), cv1


def _outproj_kernel(ya_ref, yb_ref, wt_ref, wb_ref, x_ref, g_ref, o_ref):
    bb, tt, wa = ya_ref.shape
    wb = yb_ref.shape[2]
    mix = (_dot(ya_ref[...].reshape(bb * tt, wa), wt_ref[...])
           + _dot(yb_ref[...].reshape(bb * tt, wb), wb_ref[...]))
    o_ref[...] = x_ref[...] + g_ref[...] * mix.reshape(bb, tt, mix.shape[-1])


def _outproj(ya, yb, w_out, x, g1):
    B, T, D = x.shape
    wa, wb = ya.shape[2], yb.shape[2]
    assert wa == wb
    bb, tt = _row_tiling(B, T, 1024)
    tn = _pick(D, 1024, LANES)
    return pl.pallas_call(
        _outproj_kernel,
        grid=(B // bb, T // tt, D // tn),
        in_specs=[pl.BlockSpec((bb, tt, wa), lambda b, t, j: (b, t, 0)),
                  pl.BlockSpec((bb, tt, wb), lambda b, t, j: (b, t, 0)),
                  pl.BlockSpec((wa, tn), lambda b, t, j: (0, j)),
                  pl.BlockSpec((wb, tn), lambda b, t, j: (1, j)),
                  pl.BlockSpec((bb, tt, tn), lambda b, t, j: (b, t, j)),
                  pl.BlockSpec((bb, 1, tn), lambda b, t, j: (b, 0, j))],
        out_specs=pl.BlockSpec((bb, tt, tn), lambda b, t, j: (b, t, j)),
        out_shape=jax.ShapeDtypeStruct((B, T, D), F32),
        compiler_params=_cparams("arbitrary", "arbitrary", "arbitrary"),
        name="outproj",
    )(ya, yb, w_out, w_out, x, g1)


def _ffn_kernel(x_ref, ng_ref, sc_ref, sh_ref, g2_ref, wg_ref, wu_ref, wd_ref, gf_ref, o_ref, h_s,
                *, n_col_chunks, final_norm):
    j = pl.program_id(2)
    bb, tt, D = x_ref.shape
    def row_loop(body, rc):
        assert tt % rc == 0
        for b in range(bb):
            for i in range(tt // rc):
                body(b, i * rc)

    @pl.when(j == 0)
    def _():
        rc = 2 * SUBLANES

        def half(b, r0):
            x = x_ref[b, pl.ds(r0, SUBLANES), :]
            r = lax.rsqrt(jnp.mean(x * x, axis=-1, keepdims=True) + EPS)
            return x * r * (ng_ref[0] * (1.0 + sc_ref[b])) + sh_ref[b]

        def body(b, r0):
            h = jnp.concatenate([half(b, r0), half(b, r0 + SUBLANES)], axis=0)
            h_s[pl.ds(b * tt + r0, rc), :] = h.astype(BF16)
            o_ref[b, pl.ds(r0, rc), :] = jnp.zeros((rc, D), F32)
        row_loop(body, rc)

    hb = h_s[...]
    gate = _dot(hb, wg_ref[...])
    up = _dot(hb, wu_ref[...])
    act = (gate * _sigmoid(gate) * up).astype(BF16)
    cw = D // n_col_chunks
    for c in range(n_col_chunks):
        cols = slice(c * cw, (c + 1) * cw)
        o_ref[:, :, cols] += _dot(act, wd_ref[:, cols]).reshape(bb, tt, cw)

    @pl.when(j == pl.num_programs(2) - 1)
    def _():
        def body(b, r0):
            rows = pl.ds(r0, SUBLANES)
            x2 = x_ref[b, rows, :] + g2_ref[b] * o_ref[b, rows, :]
            if final_norm:
                x2 = x2 * lax.rsqrt(jnp.mean(x2 * x2, axis=-1, keepdims=True) + EPS) * gf_ref[0]
            o_ref[b, rows, :] = x2
        row_loop(body, SUBLANES)


def _ffn(x, ng, sc, sh, g2, w_g, w_u, w_down, gf, final_norm):
    B, T, D = x.shape
    F = w_down.shape[0]
    tf = _pick(F, 256, LANES)
    nf = F // tf
    u_off = nf if w_u.shape[1] == 2 * F else 0
    bb, tt = _row_tiling(B, T, 512)
    return pl.pallas_call(
        functools.partial(_ffn_kernel, n_col_chunks=max(1, D // 1024), final_norm=final_norm),
        grid=(B // bb, T // tt, nf),
        in_specs=[pl.BlockSpec((bb, tt, D), lambda b, t, j: (b, t, 0)),
                  pl.BlockSpec((1, 1, D), lambda b, t, j: (0, 0, 0)),
                  pl.BlockSpec((bb, 1, D), lambda b, t, j: (b, 0, 0)),
                  pl.BlockSpec((bb, 1, D), lambda b, t, j: (b, 0, 0)),
                  pl.BlockSpec((bb, 1, D), lambda b, t, j: (b, 0, 0)),
                  pl.BlockSpec((D, tf), lambda b, t, j: (0, j)),
                  pl.BlockSpec((D, tf), lambda b, t, j: (0, u_off + j)),
                  pl.BlockSpec((tf, D), lambda b, t, j: (j, 0)),
                  pl.BlockSpec((1, 1, D), lambda b, t, j: (0, 0, 0))],
        out_specs=pl.BlockSpec((bb, tt, D), lambda b, t, j: (b, t, 0)),
        out_shape=jax.ShapeDtypeStruct((B, T, D), F32),
        scratch_shapes=[pltpu.VMEM((bb * tt, D), BF16)],
        compiler_params=_cparams("arbitrary", "arbitrary", "arbitrary"),
        name="ffn",
    )(x, ng, sc, sh, g2, w_g, w_u, w_down, gf)


def _mixer(x, ada, state, p, bf16_weights, cast_jobs):
    sh1, sc1, g1 = ada
    c0, n0, m0, hl0, cv0 = state
    B, T, D = x.shape
    _, H, dk, dv = c0.shape
    W = cv0.shape[2]
    n_qk, n_v = 2 * H * dk, H * dv
    assert n_qk == n_v and (3 * n_v) % LANES == 0
    h1, gates = _normmod_gates(x, p["norm1_g"], sc1, sh1, p["w_in"], 3 * n_v // LANES, p["b_gates"], H)
    h1 = h1.reshape(B * T, D)
    jobs = list(cast_jobs.items())

    def proj(w, blk, n, dt, ep, cs=None):
        name, job = jobs.pop() if jobs else (None, None)
        out = _inproj(h1, w, blk, n, dt, ep, cs, job)
        if job is not None:
            out, bf16_weights[name] = out
        return out.reshape(B, T, n)

    qk = proj(p["w_in"], 0, n_qk, BF16, "colscale", p["qk_scale"])
    v = proj(p["w_in"], 1, n_v, BF16, "none")
    og = proj(p["w_in"], 2, n_v, F32, "sigmoid")
    xb = proj(p["w_xg"], 0, W, F32, "none")
    gg = proj(p["w_xg"], 1, W, F32, "gelu")
    assert not jobs
    ya, c1, n1, m1 = _mlstm(qk, v, og, gates, c0, n0, m0, p["head_norm_g"])
    yb, hl1, cv1 = _rglru(xb, gg, cv0, hl0, p["conv_w"], p["conv_b"],
                          p["w_r"], p["w_i"], p["b_r"], p["b_i"], p["lru_lambda"])
    return _outproj(ya, yb, bf16_weights["w_out"], x, g1), (c1, n1, m1, hl1, cv1)


def kernel(x_prompt, x_sample, c_prompt, c_sample, state_mlstm_C, state_mlstm_n, state_mlstm_m, state_lru_h, state_conv, w_ada, b_ada, norm1_g, norm2_g, w_in, b_gates_a, head_norm_g, conv_w, conv_b, w_r, b_r, w_i, b_i, lru_lambda, w_out, w_gu, w_down, normf_g):
    depth = w_in.shape[0]
    Bp, Tp, D = x_prompt.shape
    Bs = x_sample.shape[0]
    H, dk, dv = state_mlstm_C.shape[2:]
    W = conv_w.shape[2]
    taps = conv_w.shape[1]
    c_x = 2 * H * dk + 2 * H * dv + 2 * H

    xp, xs = x_prompt, x_sample
    p_states, s_states = [], []
    R = Bp + Bs
    Rpad = -(-R // SUBLANES) * SUBLANES
    c_all = jnp.concatenate([c_prompt, c_sample, jnp.zeros((Rpad - R, D), F32)], axis=0)
    gf = normf_g.reshape(1, 1, D)
    qk_scale = jnp.concatenate([jnp.ones((1, H * dk), F32), jnp.full((1, H * dk), dk ** -0.5, F32)], axis=1)
    for l in range(depth):
        w_in_l = w_in[l]
        p = {
            "norm1_g": norm1_g[l].reshape(1, 1, D),
            "w_in": w_in_l.astype(BF16),
            "w_xg": w_in_l[:, c_x:].astype(BF16),
            "qk_scale": qk_scale,
            "b_gates": jnp.pad(b_gates_a[l].reshape(1, 2 * H), ((0, 0), (0, LANES - 2 * H))),
            "head_norm_g": head_norm_g[l],
            "conv_w": conv_w[l], "conv_b": conv_b[l],
            "w_r": w_r[l], "w_i": w_i[l], "b_r": b_r[l], "b_i": b_i[l],
            "lru_lambda": lru_lambda[l],
        }
        steps = (Bp * Tp) // _inproj_tile(Bp * Tp)
        planned = {"w_g": _cast_job(w_gu[l], steps, 2, 0), "w_u": _cast_job(w_gu[l], steps, 2, 1),
                   "w_down": _cast_job(w_down[l], steps), "w_out": _cast_job(w_out[l], steps)}
        cast_jobs = {k: j for k, j in planned.items() if j is not None}
        bf16_weights = {}
        if "w_g" not in cast_jobs or "w_u" not in cast_jobs:
            cast_jobs.pop("w_g", None), cast_jobs.pop("w_u", None)
            bf16_weights["w_g"] = bf16_weights["w_u"] = w_gu[l].astype(BF16)
        if "w_down" not in cast_jobs:
            bf16_weights["w_down"] = w_down[l].astype(BF16)
        if "w_out" not in cast_jobs:
            bf16_weights["w_out"] = w_out[l].astype(BF16)
        norm2 = norm2_g[l].reshape(1, 1, D)
        ada = _ada(c_all, w_ada[l], b_ada[l].reshape(1, -1))
        ada_p = [ada[:Bp, i * D:(i + 1) * D].reshape(Bp, 1, D) for i in range(6)]
        ada_s = [ada[Bp:R, i * D:(i + 1) * D].reshape(Bs, 1, D) for i in range(6)]

        zero_state = (jnp.zeros((Bp, H, dk, dv), F32), jnp.zeros((Bp, H, dk), F32),
                      jnp.zeros((Bp, H), F32), jnp.zeros((Bp, W), F32),
                      jnp.zeros((Bp, taps - 1, W), F32))
        cache_state = (state_mlstm_C[l], state_mlstm_n[l], state_mlstm_m[l], state_lru_h[l],
                       state_conv[l])
        last = l == depth - 1
        new = []
        for x, (sh1, sc1, g1, sh2, sc2, g2), st in ((xp, ada_p, zero_state), (xs, ada_s, cache_state)):
            x1, st1 = _mixer(x, (sh1, sc1, g1), st, p, bf16_weights, cast_jobs)
            cast_jobs = {}
            y = _ffn(x1, norm2, sc2, sh2, g2, bf16_weights["w_g"], bf16_weights["w_u"],
                     bf16_weights["w_down"], gf, final_norm=last)
            new.append((y, st1))
        (xp, st_p), (xs, st_s) = new
        p_states.append(st_p)
        s_states.append(st_s)

    stack = lambda sts, i: jnp.stack([s[i] for s in sts])
    return (xp, xs) + tuple(stack(p_states, i) for i in range(5)) + tuple(stack(s_states, i) for i in range(5))
```

```python
import functools

import jax
import jax.numpy as jnp
from jax import lax
from jax.experimental import pallas as pl
from jax.experimental.pallas import tpu as pltpu

F32 = jnp.float32
BF16 = jnp.bfloat16
EPS = 1e-6
LRU_C = 8.0
LANES = 128
SUBLANES = 8
VMEM_LIMIT_BYTES = 60 * 1024 * 1024


def _cparams(*sem):
    return pltpu.CompilerParams(dimension_semantics=sem, vmem_limit_bytes=VMEM_LIMIT_BYTES)


def _pick(total, target, quantum):
    if total <= target:
        return total
    best = None
    for cand in range(quantum, target + 1, quantum):
        if total % cand == 0:
            best = cand
    assert best is not None, (total, target, quantum)
    return best


def _row_tiling(B, T, target_rows):
    if T >= target_rows:
        return 1, _pick(T, target_rows, 16)
    bb = _pick(B, max(1, target_rows // T), 1)
    return bb, T


def _mlstm_chunk(T):
    return _pick(T, LANES, 16)


def _log_sigmoid(x):
    return jnp.minimum(x, 0.0) - jnp.log1p(jnp.exp(-jnp.abs(x)))


def _sigmoid(x):
    return 0.5 * (1.0 + jnp.tanh(0.5 * x))


def _gelu_tanh(x):
    c = 0.7978845608028654
    return 0.5 * x * (1.0 + jnp.tanh(c * (x + 0.044715 * (x * x * x))))


def _dot(a, b):
    return jnp.dot(a, b, preferred_element_type=F32)


def _dot_nt(a, b):
    return lax.dot_general(a, b, (((1,), (1,)), ((), ())), preferred_element_type=F32)


def _dot_tn(a, b):
    return lax.dot_general(a, b, (((0,), (0,)), ((), ())), preferred_element_type=F32)


def _split_bf16(x, terms):
    out = []
    for _ in range(terms - 1):
        t = x.astype(BF16)
        out.append(t)
        x = x - t.astype(F32)
    out.append(x.astype(BF16))
    return out


def _ada_kernel(c_ref, w_ref, b_ref, o_ref):
    c = c_ref[...]
    s = (c * _sigmoid(c)).astype(BF16)
    o_ref[...] = _dot(s, w_ref[...].astype(BF16)) + b_ref[...]


def _ada(c, w, b):
    R, D = c.shape
    N = w.shape[1]
    tn = _pick(N, 512, LANES)
    return pl.pallas_call(
        _ada_kernel,
        grid=(N // tn,),
        in_specs=[pl.BlockSpec((R, D), lambda j: (0, 0)),
                  pl.BlockSpec((D, tn), lambda j: (0, j)),
                  pl.BlockSpec((1, tn), lambda j: (0, j))],
        out_specs=pl.BlockSpec((R, tn), lambda j: (0, j)),
        out_shape=jax.ShapeDtypeStruct((R, N), F32),
        compiler_params=_cparams("arbitrary"),
        name="ada",
    )(c, w, b)


def _normmod(x, geff, sh):
    r = lax.rsqrt(jnp.mean(x * x, axis=-1, keepdims=True) + EPS)
    return x * r * geff + sh


def _normmod_gates_kernel(x_ref, g_ref, sc_ref, sh_ref, wg_ref, bg_ref, h_ref, gr_ref, *, n_heads, chunk):
    bb, tt, D = x_ref.shape
    hb = _normmod(x_ref[...], g_ref[...] * (1.0 + sc_ref[...]), sh_ref[...]).astype(BF16)
    h_ref[...] = hb
    pre = _dot(hb.reshape(bb * tt, D), wg_ref[...]) + bg_ref[...]
    lane = lax.broadcasted_iota(jnp.int32, pre.shape, 1)
    gates = jnp.where(lane < n_heads, pre, _log_sigmoid(pre))
    rows, lanes_out = gr_ref.shape[2], gr_ref.shape[3]
    if lanes_out > chunk:
        gr_ref[...] = jnp.zeros_like(gr_ref)
    for i in range(bb):
        for c in range(tt // chunk):
            r0 = i * tt + c * chunk
            gr_ref[i, c, :, 0:chunk] = gates[r0:r0 + chunk, :].T[:rows, :]


def _normmod_gates(x, g, sc, sh, wg, gate_blk, bg, n_heads):
    B, T, D = x.shape
    L = _mlstm_chunk(T)
    bb, tt = _row_tiling(B, T, 512)
    assert tt % L == 0
    GR = -(-2 * n_heads // SUBLANES) * SUBLANES
    LP = max(L, LANES)
    return pl.pallas_call(
        functools.partial(_normmod_gates_kernel, n_heads=n_heads, chunk=L),
        grid=(B // bb, T // tt),
        in_specs=[pl.BlockSpec((bb, tt, D), lambda b, t: (b, t, 0)),
                  pl.BlockSpec((1, 1, D), lambda b, t: (0, 0, 0)),
                  pl.BlockSpec((bb, 1, D), lambda b, t: (b, 0, 0)),
                  pl.BlockSpec((bb, 1, D), lambda b, t: (b, 0, 0)),
                  pl.BlockSpec((D, LANES), lambda b, t: (0, gate_blk)),
                  pl.BlockSpec((1, LANES), lambda b, t: (0, 0))],
        out_specs=[pl.BlockSpec((bb, tt, D), lambda b, t: (b, t, 0)),
                   pl.BlockSpec((bb, tt // L, GR, LP), lambda b, t: (b, t, 0, 0))],
        out_shape=[jax.ShapeDtypeStruct((B, T, D), BF16),
                   jax.ShapeDtypeStruct((B, T // L, GR, LP), F32)],
        compiler_params=_cparams("arbitrary", "arbitrary"),
        name="normmod_gates",
    )(x, g, sc, sh, wg, bg)


def _inproj_kernel(a_ref, w_ref, *rest, epilogue, with_cast):
    rest = list(rest)
    if with_cast:
        cast_out = rest.pop()
        o_ref = rest.pop()
        cast_out[...] = rest.pop()[...].astype(BF16)
    else:
        o_ref = rest.pop()
    acc = _dot(a_ref[...], w_ref[...])
    if epilogue == "colscale":
        acc = acc * rest[0][...]
    elif epilogue == "sigmoid":
        acc = _sigmoid(acc)
    elif epilogue == "gelu":
        acc = _gelu_tanh(acc)
    o_ref[...] = acc.astype(o_ref.dtype)


def _inproj_tile(M):
    return _pick(M, 512, 16)


def _cast_job(w, steps, col_blocks=1, col_block=0):
    R, C = w.shape
    Cb = C // col_blocks
    if C % col_blocks or (col_blocks > 1 and Cb % LANES):
        return None
    if R % (steps * 2 * SUBLANES) == 0:
        return w, (R // steps, Cb), (lambda i: (i, col_block)), (lambda i: (i, 0)), (R, Cb)
    if col_blocks == 1 and C % (steps * LANES) == 0:
        return w, (R, C // steps), (lambda i: (0, i)), (lambda i: (0, i)), (R, C)
    return None


def _inproj(a, w, col_block, N, out_dtype, epilogue, colscale=None, cast=None):
    M, K = a.shape
    tm = _inproj_tile(M)
    in_specs = [pl.BlockSpec((tm, K), lambda i: (i, 0)),
                pl.BlockSpec((K, N), lambda i: (0, col_block), pipeline_mode=pl.Buffered(1))]
    args = [a, w]
    if epilogue == "colscale":
        in_specs.append(pl.BlockSpec((1, N), lambda i: (0, 0)))
        args.append(colscale)
    out_specs = [pl.BlockSpec((tm, N), lambda i: (i, 0))]
    out_shape = [jax.ShapeDtypeStruct((M, N), out_dtype)]
    if cast is not None:
        src, blk, src_map, dst_map, shape = cast
        in_specs.append(pl.BlockSpec(blk, src_map))
        args.append(src)
        out_specs.append(pl.BlockSpec(blk, dst_map))
        out_shape.append(jax.ShapeDtypeStruct(shape, BF16))
    outs = pl.pallas_call(
        functools.partial(_inproj_kernel, epilogue=epilogue, with_cast=cast is not None),
        grid=(M // tm,),
        in_specs=in_specs,
        out_specs=out_specs,
        out_shape=out_shape,
        compiler_params=_cparams("arbitrary"),
        name="inproj_" + epilogue,
    )(*args)
    return outs if cast is not None else outs[0]


def _lane_scan(x, op, lane, identity):
    d = 1
    while d < x.shape[1]:
        x = op(x, jnp.where(lane >= d, pltpu.roll(x, d, axis=1), identity))
        d *= 2
    return x


def _mlstm_kernel(qk_ref, v_ref, og_ref, gr_ref, c0_ref, n0_ref, m0_ref, hg_ref,
                  y_ref, c_ref, n_ref, m_ref, cn_s, m_s, *, chunk, n_heads, dk, dv, head_group):
    tb = pl.program_id(1)
    L, H = chunk, n_heads
    Tb = qk_ref.shape[1]
    nc = Tb // L
    LP = gr_ref.shape[3]
    assert dv % LANES == 0 and L <= LANES
    rep = lambda x, n: jnp.concatenate([x] * n, axis=1)

    eye_k = (lax.broadcasted_iota(jnp.int32, (dk, dk), 0) == lax.broadcasted_iota(jnp.int32, (dk, dk), 1))

    @pl.when(tb == 0)
    def _():
        for h in range(H):
            cn_s[h, :, 0:dv] = c0_ref[0, h]
            n_col = jnp.sum(jnp.where(eye_k, n0_ref[0, h:h + 1, :], 0.0), axis=1, keepdims=True)
            cn_s[h, :, dv:dv + LANES] = jnp.broadcast_to(n_col, (dk, LANES))
        m_s[...] = jnp.broadcast_to(m0_ref[0], m_s.shape)

    row = lax.broadcasted_iota(jnp.int32, (L, L), 0)
    col = lax.broadcasted_iota(jnp.int32, (L, L), 1)
    tril = col <= row
    lane_g = lax.broadcasted_iota(jnp.int32, (H, LP), 1)
    nr = 3 * H
    spread = ((lax.broadcasted_iota(jnp.int32, (3 * nr, nr * LANES), 1) // LANES)
              == (lax.broadcasted_iota(jnp.int32, (3 * nr, nr * LANES), 0) % nr)).astype(BF16)
    ones_2l = jnp.ones((2 * L, LANES), BF16)
    mean_2v = jnp.full((2 * dv, LANES), 1.0 / dv, BF16)
    assert dv & (dv - 1) == 0

    def chunk_step(c, carry):
        sl = pl.ds(pl.multiple_of(c * L, L), L)
        gates = gr_ref[0, tb * nc + c]
        ig, lf = gates[0:H], gates[H:2 * H]
        bcum = _lane_scan(lf, jnp.add, lane_g, 0.0)
        a = ig - bcum
        cm = _lane_scan(a, jnp.maximum, lane_g, -jnp.inf)
        rows3 = jnp.concatenate(_split_bf16(jnp.concatenate([a, cm, bcum], axis=0), 3), axis=0)
        cols = _dot_tn(rows3, spread)
        blk = lambda r: cols[0:L, r * LANES:(r + 1) * LANES]
        m_all = m_s[...]
        nv = dv // LANES
        m_new = {}
        for h0 in range(0, H, head_group):
            heads = range(h0, min(h0 + head_group, H))
            qb = {h: qk_ref[0, sl, h * dk:(h + 1) * dk] for h in heads}
            kb = {h: qk_ref[0, sl, (H + h) * dk:(H + h + 1) * dk] for h in heads}
            vb = {h: v_ref[0, sl, h * dv:(h + 1) * dv] for h in heads}
            cn = {h: cn_s[h] for h in heads}
            qk = {h: _dot_nt(qb[h], kb[h]) for h in heads}
            qc = {h: _dot(qb[h], cn[h].astype(BF16)) for h in heads}

            g, s_inter, sv, rs = {}, {}, {}, {}
            for h in heads:
                g[h] = jnp.maximum(m_all[h:h + 1, :], blk(H + h))
                s_inter[h] = jnp.exp(m_all[h:h + 1, :] - g[h])
                w = jnp.where(tril, jnp.exp(jnp.minimum(a[h:h + 1, 0:L] - g[h][:, 0:L], 0.0)), 0.0)
                s2 = _split_bf16(qk[h] * w, 2)
                sv[h] = _dot(s2[0], vb[h])
                rs[h] = _dot(jnp.concatenate(s2, axis=1), ones_2l)

            hh, ms = {}, {}
            for h in heads:
                num = rep(s_inter[h], nv) * qc[h][:, 0:dv] + sv[h]
                den = s_inter[h] * qc[h][:, dv:dv + LANES] + rs[h]
                inv = 1.0 / jnp.maximum(jnp.abs(den), jnp.exp(-(blk(2 * H + h) + g[h])))
                hh[h] = num * rep(inv, nv)
                ms[h] = _dot(jnp.concatenate(_split_bf16(hh[h] * hh[h], 2), axis=1), mean_2v)

            for h in heads:
                hn = hh[h] * rep(lax.rsqrt(ms[h] + EPS), nv) * hg_ref[h:h + 1, :]
                y_ref[0, sl, h * dv:(h + 1) * dv] = (hn * og_ref[0, sl, h * dv:(h + 1) * dv]).astype(y_ref.dtype)
                g_last = g[h][L - 1:L, :]
                w_last = jnp.exp(blk(h) - g_last)
                upd_rhs = jnp.concatenate([rep(w_last, nv) * vb[h].astype(F32), w_last], axis=1)
                decay = jnp.exp(m_all[h:h + 1, :] - g_last)
                cn_s[h] = rep(decay, nv + 1) * cn[h] + _dot_tn(kb[h], upd_rhs.astype(BF16))
                m_new[h] = blk(2 * H + h)[L - 1:L, :] + g_last
        m_s[...] = jnp.concatenate([m_new[h] for h in range(H)], axis=0)
        return carry

    lax.fori_loop(0, nc, chunk_step, 0)

    @pl.when(tb == pl.num_programs(1) - 1)
    def _():
        for h in range(H):
            c_ref[0, h] = cn_s[h, :, 0:dv]
            n_ref[0, h:h + 1, :] = jnp.sum(jnp.where(eye_k, cn_s[h, :, dv:dv + 1], 0.0), axis=0, keepdims=True)
        m_ref[0] = m_s[...]


def _mlstm(qk, v, og, gates, c0, n0, m0, head_g):
    B, T, _ = qk.shape
    _, H, dk, dv = c0.shape
    L = _mlstm_chunk(T)
    Tb = _pick(T, 512, L)
    _, NC, GR, LP = gates.shape
    y, c1, n1, m1 = pl.pallas_call(
        functools.partial(_mlstm_kernel, chunk=L, n_heads=H, dk=dk, dv=dv, head_group=4),
        grid=(B, T // Tb),
        in_specs=[pl.BlockSpec((1, Tb, 2 * H * dk), lambda b, t: (b, t, 0)),
                  pl.BlockSpec((1, Tb, H * dv), lambda b, t: (b, t, 0)),
                  pl.BlockSpec((1, Tb, H * dv), lambda b, t: (b, t, 0)),
                  pl.BlockSpec((1, NC, GR, LP), lambda b, t: (b, 0, 0, 0)),
                  pl.BlockSpec((1, H, dk, dv), lambda b, t: (b, 0, 0, 0)),
                  pl.BlockSpec((1, H, dk), lambda b, t: (b, 0, 0)),
                  pl.BlockSpec((1, H, 1), lambda b, t: (b, 0, 0)),
                  pl.BlockSpec((H, dv), lambda b, t: (0, 0))],
        out_specs=[pl.BlockSpec((1, Tb, H * dv), lambda b, t: (b, t, 0)),
                   pl.BlockSpec((1, H, dk, dv), lambda b, t: (b, 0, 0, 0)),
                   pl.BlockSpec((1, H, dk), lambda b, t: (b, 0, 0)),
                   pl.BlockSpec((1, H, LANES), lambda b, t: (b, 0, 0))],
        out_shape=[jax.ShapeDtypeStruct((B, T, H * dv), BF16),
                   jax.ShapeDtypeStruct((B, H, dk, dv), F32),
                   jax.ShapeDtypeStruct((B, H, dk), F32),
                   jax.ShapeDtypeStruct((B, H, LANES), F32)],
        scratch_shapes=[pltpu.VMEM((H, dk, dv + LANES), F32),
                        pltpu.VMEM((H, LANES), F32)],
        compiler_params=_cparams("arbitrary", "arbitrary"),
        name="mlstm",
    )(qk, v, og, gates, c0, n0, m0.reshape(B, H, 1), head_g)
    return y, c1, n1, m1[:, :, 0]


def _scan_pitch(rows):
    quads = -(-(-(-rows // SUBLANES)) // 4)
    return 4 * (quads + 1 - quads % 2)


def _rglru_kernel(xb_ref, gg_ref, cv0_ref, h0_ref, cw_ref, cb_ref, wr_ref, wi_ref, br_ref, bi_ref,
                  lam_ref, y_ref, h_ref, cv_ref, xext, a_s, u_s, *, pitch, group):
    tb = pl.program_id(1)
    Tb, W = xb_ref.shape[1], xb_ref.shape[2]
    nb, bw = wr_ref.shape[0], wr_ref.shape[1]
    taps = cw_ref.shape[0]
    rows = a_s.shape[1]
    pad = SUBLANES

    @pl.when(tb == 0)
    def _():
        h_ref[...] = h0_ref[...]
        cv_ref[...] = cv0_ref[...]

    x = xb_ref[0]
    xext[pl.ds(pad, Tb), :] = x
    xext[pl.ds(pad - (taps - 1), taps - 1), :] = cv_ref[0]
    xc = cb_ref[...] + cw_ref[taps - 1:taps, :] * x
    for j in range(taps - 1):
        xc = xc + cw_ref[j:j + 1, :] * xext[pl.ds(pad - (taps - 1) + j, Tb), :]
    cv_ref[0] = xext[pl.ds(pad + Tb - (taps - 1), taps - 1), :]

    log_sig_lam = _log_sigmoid(lam_ref[...])
    for n in range(nb):
        cols = slice(n * bw, (n + 1) * bw)
        xn = xc[:, cols]
        xnb = xn.astype(BF16)
        r = _sigmoid(_dot(xnb, wr_ref[n]) + br_ref[:, cols])
        i = _sigmoid(_dot(xnb, wi_ref[n]) + bi_ref[:, cols])
        log_a = LRU_C * r * log_sig_lam[:, cols]
        a = jnp.exp(log_a)
        a_s[n, 0:Tb, :] = a
        u_s[n, 0:Tb, :] = jnp.sqrt(-jnp.tanh(log_a) * (1.0 + a * a)) * (i * xn)
        if rows > Tb:
            a_s[n, Tb:rows, :] = jnp.ones((rows - Tb, bw), F32)
            u_s[n, Tb:rows, :] = jnp.zeros((rows - Tb, bw), F32)

    for n0 in range(0, nb, group):
        slabs = list(range(n0, min(n0 + group, nb)))

        def pass1(i, carry):
            idx = pl.ds(i, SUBLANES, stride=pitch)
            out = []
            for (hz, ac), n in zip(carry, slabs):
                av = a_s[n, idx, :]
                hz = av * hz + u_s[n, idx, :]
                ac = av * ac
                u_s[n, idx, :] = hz
                a_s[n, idx, :] = ac
                out.append((hz, ac))
            return tuple(out)

        init = tuple((jnp.zeros((SUBLANES, bw), F32), jnp.ones((SUBLANES, bw), F32)) for _ in slabs)
        ends = lax.fori_loop(0, pitch, pass1, init, unroll=2)

        starts = []
        for (hz, ac), n in zip(ends, slabs):
            cols = slice(n * bw, (n + 1) * bw)
            h = h_ref[0, :, cols]
            per_block = []
            for j in range(SUBLANES):
                per_block.append(h)
                h = hz[j:j + 1, :] + ac[j:j + 1, :] * h
            h_ref[0, :, cols] = h
            starts.append(jnp.concatenate(per_block, axis=0))

        def pass2(i, carry):
            idx = pl.ds(i, SUBLANES, stride=pitch)
            for st, n in zip(starts, slabs):
                u_s[n, idx, :] = u_s[n, idx, :] + a_s[n, idx, :] * st
            return carry

        lax.fori_loop(0, pitch, pass2, 0, unroll=2)

    for n in range(nb):
        cols = slice(n * bw, (n + 1) * bw)
        y_ref[0, :, cols] = (u_s[n, 0:Tb, :] * gg_ref[0, :, cols]).astype(y_ref.dtype)


def _rglru(xb, gg, cv0, h0, conv_w, conv_b, w_r, w_i, b_r, b_i, lam):
    B, T, W = xb.shape
    taps = conv_w.shape[0]
    nb, bw = w_r.shape[0], w_r.shape[1]
    assert T >= taps - 1 and bw == LANES and nb * bw == W
    Tb = _pick(T, 256, SUBLANES)
    pitch = _scan_pitch(Tb)
    row = lambda a: a.reshape(1, W)
    y, h1, cv1 = pl.pallas_call(
        functools.partial(_rglru_kernel, pitch=pitch, group=8),
        grid=(B, T // Tb),
        in_specs=[pl.BlockSpec((1, Tb, W), lambda b, t: (b, t, 0)),
                  pl.BlockSpec((1, Tb, W), lambda b, t: (b, t, 0)),
                  pl.BlockSpec((1, taps - 1, W), lambda b, t: (b, 0, 0)),
                  pl.BlockSpec((1, 1, W), lambda b, t: (b, 0, 0)),
                  pl.BlockSpec((taps, W), lambda b, t: (0, 0)),
                  pl.BlockSpec((1, W), lambda b, t: (0, 0)),
                  pl.BlockSpec((nb, bw, bw), lambda b, t: (0, 0, 0)),
                  pl.BlockSpec((nb, bw, bw), lambda b, t: (0, 0, 0)),
                  pl.BlockSpec((1, W), lambda b, t: (0, 0)),
                  pl.BlockSpec((1, W), lambda b, t: (0, 0)),
                  pl.BlockSpec((1, W), lambda b, t: (0, 0))],
        out_specs=[pl.BlockSpec((1, Tb, W), lambda b, t: (b, t, 0)),
                   pl.BlockSpec((1, 1, W), lambda b, t: (b, 0, 0)),
                   pl.BlockSpec((1, taps - 1, W), lambda b, t: (b, 0, 0))],
        out_shape=[jax.ShapeDtypeStruct((B, T, W), BF16),
                   jax.ShapeDtypeStruct((B, 1, W), F32),
                   jax.ShapeDtypeStruct((B, taps - 1, W), F32)],
        scratch_shapes=[pltpu.VMEM((Tb + SUBLANES, W), F32),
                        pltpu.VMEM((nb, SUBLANES * pitch, bw), F32),
                        pltpu.VMEM((nb, SUBLANES * pitch, bw), F32)],
        compiler_params=_cparams("arbitrary", "arbitrary"),
        name="rglru",
    )(xb, gg, cv0, h0.reshape(B, 1, W), conv_w, row(conv_b), w_r.astype(BF16), w_i.astype(BF16),
      row(b_r), row(b_i), row(lam))
    return y, h1.reshape(B, W), cv1


def _outproj_kernel(ya_ref, yb_ref, wt_ref, wb_ref, x_ref, g_ref, o_ref):
    bb, tt, wa = ya_ref.shape
    wb = yb_ref.shape[2]
    mix = (_dot(ya_ref[...].reshape(bb * tt, wa), wt_ref[...])
           + _dot(yb_ref[...].reshape(bb * tt, wb), wb_ref[...]))
    o_ref[...] = x_ref[...] + g_ref[...] * mix.reshape(bb, tt, mix.shape[-1])


def _outproj(ya, yb, w_out, x, g1):
    B, T, D = x.shape
    wa, wb = ya.shape[2], yb.shape[2]
    assert wa == wb
    bb, tt = _row_tiling(B, T, 1024)
    tn = _pick(D, 1024, LANES)
    return pl.pallas_call(
        _outproj_kernel,
        grid=(B // bb, T // tt, D // tn),
        in_specs=[pl.BlockSpec((bb, tt, wa), lambda b, t, j: (b, t, 0)),
                  pl.BlockSpec((bb, tt, wb), lambda b, t, j: (b, t, 0)),
                  pl.BlockSpec((wa, tn), lambda b, t, j: (0, j)),
                  pl.BlockSpec((wb, tn), lambda b, t, j: (1, j)),
                  pl.BlockSpec((bb, tt, tn), lambda b, t, j: (b, t, j)),
                  pl.BlockSpec((bb, 1, tn), lambda b, t, j: (b, 0, j))],
        out_specs=pl.BlockSpec((bb, tt, tn), lambda b, t, j: (b, t, j)),
        out_shape=jax.ShapeDtypeStruct((B, T, D), F32),
        compiler_params=_cparams("arbitrary", "arbitrary", "arbitrary"),
        name="outproj",
    )(ya, yb, w_out, w_out, x, g1)


def _ffn_kernel(x_ref, ng_ref, sc_ref, sh_ref, g2_ref, *rest, n_col_chunks, final_norm, tiles_per_step, n_tiles):
    w_refs, (gf_ref, o_ref, h_s) = rest[:3 * tiles_per_step], rest[3 * tiles_per_step:]
    j = pl.program_id(2)
    last = pl.num_programs(2) - 1
    bb, tt, D = x_ref.shape

    def row_loop(body, rc):
        assert tt % rc == 0
        for b in range(bb):
            for i in range(tt // rc):
                body(b, i * rc)

    @pl.when(j == 0)
    def _():
        rc = 2 * SUBLANES

        def half(b, r0):
            x = x_ref[b, pl.ds(r0, SUBLANES), :]
            r = lax.rsqrt(jnp.mean(x * x, axis=-1, keepdims=True) + EPS)
            return x * r * (ng_ref[0] * (1.0 + sc_ref[b])) + sh_ref[b]

        def body(b, r0):
            h = jnp.concatenate([half(b, r0), half(b, r0 + SUBLANES)], axis=0)
            h_s[pl.ds(b * tt + r0, rc), :] = h.astype(BF16)
            o_ref[b, pl.ds(r0, rc), :] = jnp.zeros((rc, D), F32)
        row_loop(body, rc)

    def hidden_tiles(count):
        hb = h_s[...]
        cw = D // n_col_chunks
        for k in range(count):
            wg_ref, wu_ref, wd_ref = w_refs[3 * k:3 * k + 3]
            gate = _dot(hb, wg_ref[...])
            up = _dot(hb, wu_ref[...])
            act = (gate * _sigmoid(gate) * up).astype(BF16)
            for c in range(n_col_chunks):
                cols = slice(c * cw, (c + 1) * cw)
                o_ref[:, :, cols] += _dot(act, wd_ref[:, cols]).reshape(bb, tt, cw)

    tail = n_tiles - (pl.cdiv(n_tiles, tiles_per_step) - 1) * tiles_per_step
    if tail == tiles_per_step:
        hidden_tiles(tiles_per_step)
    else:
        @pl.when(j < last)
        def _():
            hidden_tiles(tiles_per_step)

        @pl.when(j == last)
        def _():
            hidden_tiles(tail)

    @pl.when(j == last)
    def _():
        def body(b, r0):
            rows = pl.ds(r0, SUBLANES)
            x2 = x_ref[b, rows, :] + g2_ref[b] * o_ref[b, rows, :]
            if final_norm:
                x2 = x2 * lax.rsqrt(jnp.mean(x2 * x2, axis=-1, keepdims=True) + EPS) * gf_ref[0]
            o_ref[b, rows, :] = x2
        row_loop(body, SUBLANES)


def _ffn(x, ng, sc, sh, g2, w_g, w_u, w_down, gf, final_norm):
    B, T, D = x.shape
    F = w_down.shape[0]
    tf = _pick(F, 256, LANES)
    nf = F // tf
    u_off = nf if w_u.shape[1] == 2 * F else 0
    bb, tt = _row_tiling(B, T, 512)
    tps = 2
    w_specs, w_args = [], []
    for k in range(tps):
        tile = lambda j, k=k: jnp.minimum(j * tps + k, nf - 1)
        w_specs += [pl.BlockSpec((D, tf), lambda b, t, j, tile=tile: (0, tile(j))),
                    pl.BlockSpec((D, tf), lambda b, t, j, tile=tile: (0, u_off + tile(j))),
                    pl.BlockSpec((tf, D), lambda b, t, j, tile=tile: (tile(j), 0))]
        w_args += [w_g, w_u, w_down]
    return pl.pallas_call(
        functools.partial(_ffn_kernel, n_col_chunks=max(1, D // 1024), final_norm=final_norm,
                          tiles_per_step=tps, n_tiles=nf),
        grid=(B // bb, T // tt, pl.cdiv(nf, tps)),
        in_specs=[pl.BlockSpec((bb, tt, D), lambda b, t, j: (b, t, 0), pipeline_mode=pl.Buffered(1)),
                  pl.BlockSpec((1, 1, D), lambda b, t, j: (0, 0, 0)),
                  pl.BlockSpec((bb, 1, D), lambda b, t, j: (b, 0, 0)),
                  pl.BlockSpec((bb, 1, D), lambda b, t, j: (b, 0, 0)),
                  pl.BlockSpec((bb, 1, D), lambda b, t, j: (b, 0, 0))] + w_specs + [
                  pl.BlockSpec((1, 1, D), lambda b, t, j: (0, 0, 0))],
        out_specs=pl.BlockSpec((bb, tt, D), lambda b, t, j: (b, t, 0)),
        out_shape=jax.ShapeDtypeStruct((B, T, D), F32),
        scratch_shapes=[pltpu.VMEM((bb * tt, D), BF16)],
        compiler_params=_cparams("arbitrary", "arbitrary", "arbitrary"),
        name="ffn",
    )(x, ng, sc, sh, g2, *w_args, gf)


def _mixer(x, ada, state, p, bf16_weights, cast_jobs):
    sh1, sc1, g1 = ada
    c0, n0, m0, hl0, cv0 = state
    B, T, D = x.shape
    _, H, dk, dv = c0.shape
    W = cv0.shape[2]
    n_qk, n_v = 2 * H * dk, H * dv
    assert n_qk == n_v and (3 * n_v) % LANES == 0
    h1, gates = _normmod_gates(x, p["norm1_g"], sc1, sh1, p["w_in"], 3 * n_v // LANES, p["b_gates"], H)
    h1 = h1.reshape(B * T, D)
    jobs = list(cast_jobs.items())

    def proj(w, blk, n, dt, ep, cs=None):
        name, job = jobs.pop() if jobs else (None, None)
        out = _inproj(h1, w, blk, n, dt, ep, cs, job)
        if job is not None:
            out, bf16_weights[name] = out
        return out.reshape(B, T, n)

    qk = proj(p["w_in"], 0, n_qk, BF16, "colscale", p["qk_scale"])
    v = proj(p["w_in"], 1, n_v, BF16, "none")
    og = proj(p["w_in"], 2, n_v, F32, "sigmoid")
    xb = proj(p["w_xg"], 0, W, F32, "none")
    gg = proj(p["w_xg"], 1, W, F32, "gelu")
    assert not jobs
    ya, c1, n1, m1 = _mlstm(qk, v, og, gates, c0, n0, m0, p["head_norm_g"])
    yb, hl1, cv1 = _rglru(xb, gg, cv0, hl0, p["conv_w"], p["conv_b"],
                          p["w_r"], p["w_i"], p["b_r"], p["b_i"], p["lru_lambda"])
    return _outproj(ya, yb, bf16_weights["w_out"], x, g1), (c1, n1, m1, hl1, cv1)


def kernel(x_prompt, x_sample, c_prompt, c_sample, state_mlstm_C, state_mlstm_n, state_mlstm_m, state_lru_h, state_conv, w_ada, b_ada, norm1_g, norm2_g, w_in, b_gates_a, head_norm_g, conv_w, conv_b, w_r, b_r, w_i, b_i, lru_lambda, w_out, w_gu, w_down, normf_g):
    depth = w_in.shape[0]
    Bp, Tp, D = x_prompt.shape
    Bs = x_sample.shape[0]
    H, dk, dv = state_mlstm_C.shape[2:]
    W = conv_w.shape[2]
    taps = conv_w.shape[1]
    c_x = 2 * H * dk + 2 * H * dv + 2 * H

    xp, xs = x_prompt, x_sample
    p_states, s_states = [], []
    R = Bp + Bs
    Rpad = -(-R // SUBLANES) * SUBLANES
    c_all = jnp.concatenate([c_prompt, c_sample, jnp.zeros((Rpad - R, D), F32)], axis=0)
    gf = normf_g.reshape(1, 1, D)
    qk_scale = jnp.concatenate([jnp.ones((1, H * dk), F32), jnp.full((1, H * dk), dk ** -0.5, F32)], axis=1)
    for l in range(depth):
        w_in_l = w_in[l]
        p = {
            "norm1_g": norm1_g[l].reshape(1, 1, D),
            "w_in": w_in_l.astype(BF16),
            "w_xg": w_in_l[:, c_x:].astype(BF16),
            "qk_scale": qk_scale,
            "b_gates": jnp.pad(b_gates_a[l].reshape(1, 2 * H), ((0, 0), (0, LANES - 2 * H))),
            "head_norm_g": head_norm_g[l],
            "conv_w": conv_w[l], "conv_b": conv_b[l],
            "w_r": w_r[l], "w_i": w_i[l], "b_r": b_r[l], "b_i": b_i[l],
            "lru_lambda": lru_lambda[l],
        }
        steps = (Bp * Tp) // _inproj_tile(Bp * Tp)
        planned = {"w_g": _cast_job(w_gu[l], steps, 2, 0), "w_u": _cast_job(w_gu[l], steps, 2, 1),
                   "w_down": _cast_job(w_down[l], steps), "w_out": _cast_job(w_out[l], steps)}
        cast_jobs = {k: j for k, j in planned.items() if j is not None}
        bf16_weights = {}
        if "w_g" not in cast_jobs or "w_u" not in cast_jobs:
            cast_jobs.pop("w_g", None), cast_jobs.pop("w_u", None)
            bf16_weights["w_g"] = bf16_weights["w_u"] = w_gu[l].astype(BF16)
        if "w_down" not in cast_jobs:
            bf16_weights["w_down"] = w_down[l].astype(BF16)
        if "w_out" not in cast_jobs:
            bf16_weights["w_out"] = w_out[l].astype(BF16)
        norm2 = norm2_g[l].reshape(1, 1, D)
        ada = _ada(c_all, w_ada[l], b_ada[l].reshape(1, -1))
        ada_p = [ada[:Bp, i * D:(i + 1) * D].reshape(Bp, 1, D) for i in range(6)]
        ada_s = [ada[Bp:R, i * D:(i + 1) * D].reshape(Bs, 1, D) for i in range(6)]

        zero_state = (jnp.zeros((Bp, H, dk, dv), F32), jnp.zeros((Bp, H, dk), F32),
                      jnp.zeros((Bp, H), F32), jnp.zeros((Bp, W), F32),
                      jnp.zeros((Bp, taps - 1, W), F32))
        cache_state = (state_mlstm_C[l], state_mlstm_n[l], state_mlstm_m[l], state_lru_h[l],
                       state_conv[l])
        last = l == depth - 1
        new = []
        for x, (sh1, sc1, g1, sh2, sc2, g2), st in ((xp, ada_p, zero_state), (xs, ada_s, cache_state)):
            x1, st1 = _mixer(x, (sh1, sc1, g1), st, p, bf16_weights, cast_jobs)
            cast_jobs = {}
            y = _ffn(x1, norm2, sc2, sh2, g2, bf16_weights["w_g"], bf16_weights["w_u"],
                     bf16_weights["w_down"], gf, final_norm=last)
            new.append((y, st1))
        (xp, st_p), (xs, st_s) = new
        p_states.append(st_p)
        s_states.append(st_s)

    stack = lambda sts, i: jnp.stack([s[i] for s in sts])
    return (xp, xs) + tuple(stack(p_states, i) for i in range(5)) + tuple(stack(s_states, i) for i in range(5))
```

```python
import functools

import jax
import jax.numpy as jnp
from jax import lax
from jax.experimental import pallas as pl
from jax.experimental.pallas import tpu as pltpu

F32 = jnp.float32
BF16 = jnp.bfloat16
EPS = 1e-6
LRU_C = 8.0
LANES = 128
SUBLANES = 8
VMEM_LIMIT_BYTES = 60 * 1024 * 1024


def _cparams(*sem):
    return pltpu.CompilerParams(dimension_semantics=sem, vmem_limit_bytes=VMEM_LIMIT_BYTES)


def _pick(total, target, quantum):
    if total <= target:
        return total
    best = None
    for cand in range(quantum, target + 1, quantum):
        if total % cand == 0:
            best = cand
    assert best is not None, (total, target, quantum)
    return best


def _row_tiling(B, T, target_rows):
    if T >= target_rows:
        return 1, _pick(T, target_rows, 16)
    bb = _pick(B, max(1, target_rows // T), 1)
    return bb, T


def _mlstm_chunk(T):
    return _pick(T, LANES, 16)


def _log_sigmoid(x):
    return jnp.minimum(x, 0.0) - jnp.log1p(jnp.exp(-jnp.abs(x)))


def _sigmoid(x):
    return 0.5 * (1.0 + jnp.tanh(0.5 * x))


def _gelu_tanh(x):
    c = 0.7978845608028654
    return 0.5 * x * (1.0 + jnp.tanh(c * (x + 0.044715 * (x * x * x))))


def _dot(a, b):
    return jnp.dot(a, b, preferred_element_type=F32)


def _dot_nt(a, b):
    return lax.dot_general(a, b, (((1,), (1,)), ((), ())), preferred_element_type=F32)


def _dot_tn(a, b):
    return lax.dot_general(a, b, (((0,), (0,)), ((), ())), preferred_element_type=F32)


def _split_bf16(x, terms):
    out = []
    for _ in range(terms - 1):
        t = x.astype(BF16)
        out.append(t)
        x = x - t.astype(F32)
    out.append(x.astype(BF16))
    return out


def _ada_kernel(c_ref, w_ref, b_ref, o_ref):
    c = c_ref[...]
    s = (c * _sigmoid(c)).astype(BF16)
    o_ref[...] = _dot(s, w_ref[...].astype(BF16)) + b_ref[...]


def _ada(c, w, b):
    R, D = c.shape
    N = w.shape[1]
    tn = _pick(N, 512, LANES)
    return pl.pallas_call(
        _ada_kernel,
        grid=(N // tn,),
        in_specs=[pl.BlockSpec((R, D), lambda j: (0, 0)),
                  pl.BlockSpec((D, tn), lambda j: (0, j)),
                  pl.BlockSpec((1, tn), lambda j: (0, j))],
        out_specs=pl.BlockSpec((R, tn), lambda j: (0, j)),
        out_shape=jax.ShapeDtypeStruct((R, N), F32),
        compiler_params=_cparams("arbitrary"),
        name="ada",
    )(c, w, b)


def _normmod(x, geff, sh):
    r = lax.rsqrt(jnp.mean(x * x, axis=-1, keepdims=True) + EPS)
    return x * r * geff + sh


def _normmod_gates_kernel(x_ref, g_ref, sc_ref, sh_ref, wg_ref, bg_ref, h_ref, gr_ref, gt_s, *, n_heads, chunk):
    bb, tt, D = x_ref.shape
    H = n_heads
    hb = _normmod(x_ref[...], g_ref[...] * (1.0 + sc_ref[...]), sh_ref[...]).astype(BF16)
    h_ref[...] = hb
    pre = _dot(hb.reshape(bb * tt, D), wg_ref[...]) + bg_ref[...]
    lane = lax.broadcasted_iota(jnp.int32, pre.shape, 1)
    gates = jnp.where(lane < H, pre, _log_sigmoid(pre))
    rows, lanes_out = gt_s.shape[2], gt_s.shape[3]
    if lanes_out > chunk:
        gt_s[...] = jnp.zeros_like(gt_s)
    blocks = [(i, c) for i in range(bb) for c in range(tt // chunk)]
    for i, c in blocks:
        r0 = i * tt + c * chunk
        gt_s[i, c, :, 0:chunk] = gates[r0:r0 + chunk, :].T[:rows, :]
    ig = jnp.concatenate([gt_s[i, c, 0:H, :] for i, c in blocks], axis=0)
    lf = jnp.concatenate([gt_s[i, c, H:2 * H, :] for i, c in blocks], axis=0)
    lane_t = lax.broadcasted_iota(jnp.int32, lf.shape, 1)
    bcum = _lane_scan(lf, jnp.add, lane_t, 0.0)
    a = ig - bcum
    cm = _lane_scan(a, jnp.maximum, lane_t, -jnp.inf)
    if gr_ref.shape[2] > 3 * H:
        gr_ref[...] = jnp.zeros_like(gr_ref)
    for k, (i, c) in enumerate(blocks):
        for r, val in enumerate((a, cm, bcum)):
            gr_ref[i, c, r * H:(r + 1) * H, :] = val[k * H:(k + 1) * H, :]


def _normmod_gates(x, g, sc, sh, wg, gate_blk, bg, n_heads):
    B, T, D = x.shape
    L = _mlstm_chunk(T)
    bb, tt = _row_tiling(B, T, 512)
    assert tt % L == 0
    GR = -(-2 * n_heads // SUBLANES) * SUBLANES
    GR3 = -(-3 * n_heads // SUBLANES) * SUBLANES
    LP = max(L, LANES)
    return pl.pallas_call(
        functools.partial(_normmod_gates_kernel, n_heads=n_heads, chunk=L),
        grid=(B // bb, T // tt),
        in_specs=[pl.BlockSpec((bb, tt, D), lambda b, t: (b, t, 0)),
                  pl.BlockSpec((1, 1, D), lambda b, t: (0, 0, 0)),
                  pl.BlockSpec((bb, 1, D), lambda b, t: (b, 0, 0)),
                  pl.BlockSpec((bb, 1, D), lambda b, t: (b, 0, 0)),
                  pl.BlockSpec((D, LANES), lambda b, t: (0, gate_blk)),
                  pl.BlockSpec((1, LANES), lambda b, t: (0, 0))],
        out_specs=[pl.BlockSpec((bb, tt, D), lambda b, t: (b, t, 0)),
                   pl.BlockSpec((bb, tt // L, GR3, LP), lambda b, t: (b, t, 0, 0))],
        out_shape=[jax.ShapeDtypeStruct((B, T, D), BF16),
                   jax.ShapeDtypeStruct((B, T // L, GR3, LP), F32)],
        scratch_shapes=[pltpu.VMEM((bb, tt // L, GR, LP), F32)],
        compiler_params=_cparams("arbitrary", "arbitrary"),
        name="normmod_gates",
    )(x, g, sc, sh, wg, bg)


def _inproj_kernel(a_ref, w_ref, *rest, epilogue, with_cast):
    rest = list(rest)
    if with_cast:
        cast_out = rest.pop()
        o_ref = rest.pop()
        cast_out[...] = rest.pop()[...].astype(BF16)
    else:
        o_ref = rest.pop()
    acc = _dot(a_ref[...], w_ref[...])
    if epilogue == "colscale":
        acc = acc * rest[0][...]
    elif epilogue == "sigmoid":
        acc = _sigmoid(acc)
    elif epilogue == "gelu":
        acc = _gelu_tanh(acc)
    o_ref[...] = acc.astype(o_ref.dtype)


def _inproj_tile(M):
    return _pick(M, 512, 16)


def _cast_job(w, steps, col_blocks=1, col_block=0):
    R, C = w.shape
    Cb = C // col_blocks
    if C % col_blocks or (col_blocks > 1 and Cb % LANES):
        return None
    if R % (steps * 2 * SUBLANES) == 0:
        return w, (R // steps, Cb), (lambda i: (i, col_block)), (lambda i: (i, 0)), (R, Cb)
    if col_blocks == 1 and C % (steps * LANES) == 0:
        return w, (R, C // steps), (lambda i: (0, i)), (lambda i: (0, i)), (R, C)
    return None


def _inproj(a, w, col_block, N, out_dtype, epilogue, colscale=None, cast=None):
    M, K = a.shape
    tm = _inproj_tile(M)
    in_specs = [pl.BlockSpec((tm, K), lambda i: (i, 0)),
                pl.BlockSpec((K, N), lambda i: (0, col_block), pipeline_mode=pl.Buffered(1))]
    args = [a, w]
    if epilogue == "colscale":
        in_specs.append(pl.BlockSpec((1, N), lambda i: (0, 0)))
        args.append(colscale)
    out_specs = [pl.BlockSpec((tm, N), lambda i: (i, 0))]
    out_shape = [jax.ShapeDtypeStruct((M, N), out_dtype)]
    if cast is not None:
        src, blk, src_map, dst_map, shape = cast
        in_specs.append(pl.BlockSpec(blk, src_map))
        args.append(src)
        out_specs.append(pl.BlockSpec(blk, dst_map))
        out_shape.append(jax.ShapeDtypeStruct(shape, BF16))
    outs = pl.pallas_call(
        functools.partial(_inproj_kernel, epilogue=epilogue, with_cast=cast is not None),
        grid=(M // tm,),
        in_specs=in_specs,
        out_specs=out_specs,
        out_shape=out_shape,
        compiler_params=_cparams("arbitrary"),
        name="inproj_" + epilogue,
    )(*args)
    return outs if cast is not None else outs[0]


def _lane_scan(x, op, lane, identity):
    d = 1
    while d < x.shape[1]:
        x = op(x, jnp.where(lane >= d, pltpu.roll(x, d, axis=1), identity))
        d *= 2
    return x


def _mlstm_kernel(qk_ref, v_ref, og_ref, gr_ref, c0_ref, n0_ref, m0_ref, hg_ref,
                  y_ref, c_ref, n_ref, m_ref, cn_s, m_s, *, chunk, n_heads, dk, dv, head_group):
    tb = pl.program_id(1)
    L, H = chunk, n_heads
    Tb = qk_ref.shape[1]
    nc = Tb // L
    LP = gr_ref.shape[3]
    assert dv % LANES == 0 and L <= LANES
    rep = lambda x, n: jnp.concatenate([x] * n, axis=1)

    eye_k = (lax.broadcasted_iota(jnp.int32, (dk, dk), 0) == lax.broadcasted_iota(jnp.int32, (dk, dk), 1))

    @pl.when(tb == 0)
    def _():
        for h in range(H):
            cn_s[h, :, 0:dv] = c0_ref[0, h]
            n_col = jnp.sum(jnp.where(eye_k, n0_ref[0, h:h + 1, :], 0.0), axis=1, keepdims=True)
            cn_s[h, :, dv:dv + LANES] = jnp.broadcast_to(n_col, (dk, LANES))
        m_s[...] = jnp.broadcast_to(m0_ref[0], m_s.shape)

    row = lax.broadcasted_iota(jnp.int32, (L, L), 0)
    col = lax.broadcasted_iota(jnp.int32, (L, L), 1)
    tril = col <= row
    nr = 3 * H
    spread = ((lax.broadcasted_iota(jnp.int32, (3 * nr, nr * LANES), 1) // LANES)
              == (lax.broadcasted_iota(jnp.int32, (3 * nr, nr * LANES), 0) % nr)).astype(BF16)
    ones_2l = jnp.ones((2 * L, LANES), BF16)
    mean_2v = jnp.full((2 * dv, LANES), 1.0 / dv, BF16)
    assert dv & (dv - 1) == 0

    def chunk_step(c, carry):
        sl = pl.ds(pl.multiple_of(c * L, L), L)
        rows = gr_ref[0, tb * nc + c][0:nr]
        a = rows[0:H]
        cols = _dot_tn(jnp.concatenate(_split_bf16(rows, 3), axis=0), spread)
        blk = lambda r: cols[0:L, r * LANES:(r + 1) * LANES]
        m_all = m_s[...]
        nv = dv // LANES
        m_new = {}
        for h0 in range(0, H, head_group):
            heads = range(h0, min(h0 + head_group, H))
            qb = {h: qk_ref[0, sl, h * dk:(h + 1) * dk] for h in heads}
            kb = {h: qk_ref[0, sl, (H + h) * dk:(H + h + 1) * dk] for h in heads}
            vb = {h: v_ref[0, sl, h * dv:(h + 1) * dv] for h in heads}
            cn = {h: cn_s[h] for h in heads}
            qk = {h: _dot_nt(qb[h], kb[h]) for h in heads}
            qc = {h: _dot(qb[h], cn[h].astype(BF16)) for h in heads}

            g, s_inter, sv, rs = {}, {}, {}, {}
            for h in heads:
                g[h] = jnp.maximum(m_all[h:h + 1, :], blk(H + h))
                s_inter[h] = jnp.exp(m_all[h:h + 1, :] - g[h])
                w = jnp.where(tril, jnp.exp(jnp.minimum(a[h:h + 1, 0:L] - g[h][:, 0:L], 0.0)), 0.0)
                s2 = _split_bf16(qk[h] * w, 2)
                sv[h] = _dot(s2[0], vb[h])
                rs[h] = _dot(jnp.concatenate(s2, axis=1), ones_2l)

            hh, ms = {}, {}
            for h in heads:
                num = rep(s_inter[h], nv) * qc[h][:, 0:dv] + sv[h]
                den = s_inter[h] * qc[h][:, dv:dv + LANES] + rs[h]
                inv = 1.0 / jnp.maximum(jnp.abs(den), jnp.exp(-(blk(2 * H + h) + g[h])))
                hh[h] = num * rep(inv, nv)
                ms[h] = _dot(jnp.concatenate(_split_bf16(hh[h] * hh[h], 2), axis=1), mean_2v)

            for h in heads:
                hn = hh[h] * rep(lax.rsqrt(ms[h] + EPS), nv) * hg_ref[h:h + 1, :]
                y_ref[0, sl, h * dv:(h + 1) * dv] = (hn * og_ref[0, sl, h * dv:(h + 1) * dv]).astype(y_ref.dtype)
                g_last = g[h][L - 1:L, :]
                w_last = jnp.exp(blk(h) - g_last)
                upd_rhs = jnp.concatenate([rep(w_last, nv) * vb[h].astype(F32), w_last], axis=1)
                decay = jnp.exp(m_all[h:h + 1, :] - g_last)
                cn_s[h] = rep(decay, nv + 1) * cn[h] + _dot_tn(kb[h], upd_rhs.astype(BF16))
                m_new[h] = blk(2 * H + h)[L - 1:L, :] + g_last
        m_s[...] = jnp.concatenate([m_new[h] for h in range(H)], axis=0)
        return carry

    lax.fori_loop(0, nc, chunk_step, 0)

    @pl.when(tb == pl.num_programs(1) - 1)
    def _():
        for h in range(H):
            c_ref[0, h] = cn_s[h, :, 0:dv]
            n_ref[0, h:h + 1, :] = jnp.sum(jnp.where(eye_k, cn_s[h, :, dv:dv + 1], 0.0), axis=0, keepdims=True)
        m_ref[0] = m_s[...]


def _mlstm(qk, v, og, gates, c0, n0, m0, head_g):
    B, T, _ = qk.shape
    _, H, dk, dv = c0.shape
    L = _mlstm_chunk(T)
    Tb = _pick(T, 512, L)
    _, NC, GR, LP = gates.shape
    y, c1, n1, m1 = pl.pallas_call(
        functools.partial(_mlstm_kernel, chunk=L, n_heads=H, dk=dk, dv=dv, head_group=4),
        grid=(B, T // Tb),
        in_specs=[pl.BlockSpec((1, Tb, 2 * H * dk), lambda b, t: (b, t, 0)),
                  pl.BlockSpec((1, Tb, H * dv), lambda b, t: (b, t, 0)),
                  pl.BlockSpec((1, Tb, H * dv), lambda b, t: (b, t, 0)),
                  pl.BlockSpec((1, NC, GR, LP), lambda b, t: (b, 0, 0, 0)),
                  pl.BlockSpec((1, H, dk, dv), lambda b, t: (b, 0, 0, 0)),
                  pl.BlockSpec((1, H, dk), lambda b, t: (b, 0, 0)),
                  pl.BlockSpec((1, H, 1), lambda b, t: (b, 0, 0)),
                  pl.BlockSpec((H, dv), lambda b, t: (0, 0))],
        out_specs=[pl.BlockSpec((1, Tb, H * dv), lambda b, t: (b, t, 0)),
                   pl.BlockSpec((1, H, dk, dv), lambda b, t: (b, 0, 0, 0)),
                   pl.BlockSpec((1, H, dk), lambda b, t: (b, 0, 0)),
                   pl.BlockSpec((1, H, LANES), lambda b, t: (b, 0, 0))],
        out_shape=[jax.ShapeDtypeStruct((B, T, H * dv), BF16),
                   jax.ShapeDtypeStruct((B, H, dk, dv), F32),
                   jax.ShapeDtypeStruct((B, H, dk), F32),
                   jax.ShapeDtypeStruct((B, H, LANES), F32)],
        scratch_shapes=[pltpu.VMEM((H, dk, dv + LANES), F32),
                        pltpu.VMEM((H, LANES), F32)],
        compiler_params=_cparams("arbitrary", "arbitrary"),
        name="mlstm",
    )(qk, v, og, gates, c0, n0, m0.reshape(B, H, 1), head_g)
    return y, c1, n1, m1[:, :, 0]


def _scan_pitch(rows):
    quads = -(-(-(-rows // SUBLANES)) // 4)
    return 4 * (quads + 1 - quads % 2)


def _rglru_kernel(xb_ref, gg_ref, cv0_ref, h0_ref, cw_ref, cb_ref, wr_ref, wi_ref, br_ref, bi_ref,
                  lam_ref, y_ref, h_ref, cv_ref, xext, a_s, u_s, *, pitch, group):
    tb = pl.program_id(1)
    Tb, W = xb_ref.shape[1], xb_ref.shape[2]
    nb, bw = wr_ref.shape[0], wr_ref.shape[1]
    taps = cw_ref.shape[0]
    rows = a_s.shape[1]
    pad = SUBLANES

    @pl.when(tb == 0)
    def _():
        h_ref[...] = h0_ref[...]
        cv_ref[...] = cv0_ref[...]

    x = xb_ref[0]
    xext[pl.ds(pad, Tb), :] = x
    xext[pl.ds(pad - (taps - 1), taps - 1), :] = cv_ref[0]
    xc = cb_ref[...] + cw_ref[taps - 1:taps, :] * x
    for j in range(taps - 1):
        xc = xc + cw_ref[j:j + 1, :] * xext[pl.ds(pad - (taps - 1) + j, Tb), :]
    cv_ref[0] = xext[pl.ds(pad + Tb - (taps - 1), taps - 1), :]

    log_sig_lam = _log_sigmoid(lam_ref[...])
    for n in range(nb):
        cols = slice(n * bw, (n + 1) * bw)
        xn = xc[:, cols]
        xnb = xn.astype(BF16)
        r = _sigmoid(_dot(xnb, wr_ref[n]) + br_ref[:, cols])
        i = _sigmoid(_dot(xnb, wi_ref[n]) + bi_ref[:, cols])
        log_a = LRU_C * r * log_sig_lam[:, cols]
        a = jnp.exp(log_a)
        a_s[n, 0:Tb, :] = a
        z = -jnp.tanh(log_a) * (1.0 + a * a)
        u_s[n, 0:Tb, :] = jnp.where(z > 0.0, z * lax.rsqrt(z), 0.0) * (i * xn)
        if rows > Tb:
            a_s[n, Tb:rows, :] = jnp.ones((rows - Tb, bw), F32)
            u_s[n, Tb:rows, :] = jnp.zeros((rows - Tb, bw), F32)

    for n0 in range(0, nb, group):
        slabs = list(range(n0, min(n0 + group, nb)))

        def pass1(i, carry):
            idx = pl.ds(i, SUBLANES, stride=pitch)
            out = []
            for (hz, ac), n in zip(carry, slabs):
                av = a_s[n, idx, :]
                hz = av * hz + u_s[n, idx, :]
                ac = av * ac
                u_s[n, idx, :] = hz
                a_s[n, idx, :] = ac
                out.append((hz, ac))
            return tuple(out)

        init = tuple((jnp.zeros((SUBLANES, bw), F32), jnp.ones((SUBLANES, bw), F32)) for _ in slabs)
        ends = lax.fori_loop(0, pitch, pass1, init, unroll=2)

        starts = []
        for (hz, ac), n in zip(ends, slabs):
            cols = slice(n * bw, (n + 1) * bw)
            h = h_ref[0, :, cols]
            per_block = []
            for j in range(SUBLANES):
                per_block.append(h)
                h = hz[j:j + 1, :] + ac[j:j + 1, :] * h
            h_ref[0, :, cols] = h
            starts.append(jnp.concatenate(per_block, axis=0))

        def pass2(i, carry):
            idx = pl.ds(i, SUBLANES, stride=pitch)
            for st, n in zip(starts, slabs):
                u_s[n, idx, :] = u_s[n, idx, :] + a_s[n, idx, :] * st
            return carry

        lax.fori_loop(0, pitch, pass2, 0, unroll=2)

    for n in range(nb):
        cols = slice(n * bw, (n + 1) * bw)
        y_ref[0, :, cols] = (u_s[n, 0:Tb, :] * gg_ref[0, :, cols]).astype(y_ref.dtype)


def _rglru(xb, gg, cv0, h0, conv_w, conv_b, w_r, w_i, b_r, b_i, lam):
    B, T, W = xb.shape
    taps = conv_w.shape[0]
    nb, bw = w_r.shape[0], w_r.shape[1]
    assert T >= taps - 1 and bw == LANES and nb * bw == W
    Tb = _pick(T, 256, SUBLANES)
    pitch = _scan_pitch(Tb)
    row = lambda a: a.reshape(1, W)
    y, h1, cv1 = pl.pallas_call(
        functools.partial(_rglru_kernel, pitch=pitch, group=8),
        grid=(B, T // Tb),
        in_specs=[pl.BlockSpec((1, Tb, W), lambda b, t: (b, t, 0)),
                  pl.BlockSpec((1, Tb, W), lambda b, t: (b, t, 0)),
                  pl.BlockSpec((1, taps - 1, W), lambda b, t: (b, 0, 0)),
                  pl.BlockSpec((1, 1, W), lambda b, t: (b, 0, 0)),
                  pl.BlockSpec((taps, W), lambda b, t: (0, 0)),
                  pl.BlockSpec((1, W), lambda b, t: (0, 0)),
                  pl.BlockSpec((nb, bw, bw), lambda b, t: (0, 0, 0)),
                  pl.BlockSpec((nb, bw, bw), lambda b, t: (0, 0, 0)),
                  pl.BlockSpec((1, W), lambda b, t: (0, 0)),
                  pl.BlockSpec((1, W), lambda b, t: (0, 0)),
                  pl.BlockSpec((1, W), lambda b, t: (0, 0))],
        out_specs=[pl.BlockSpec((1, Tb, W), lambda b, t: (b, t, 0)),
                   pl.BlockSpec((1, 1, W), lambda b, t: (b, 0, 0)),
                   pl.BlockSpec((1, taps - 1, W), lambda b, t: (b, 0, 0))],
        out_shape=[jax.ShapeDtypeStruct((B, T, W), BF16),
                   jax.ShapeDtypeStruct((B, 1, W), F32),
                   jax.ShapeDtypeStruct((B, taps - 1, W), F32)],
        scratch_shapes=[pltpu.VMEM((Tb + SUBLANES, W), F32),
                        pltpu.VMEM((nb, SUBLANES * pitch, bw), F32),
                        pltpu.VMEM((nb, SUBLANES * pitch, bw), F32)],
        compiler_params=_cparams("arbitrary", "arbitrary"),
        name="rglru",
    )(xb, gg, cv0, h0.reshape(B, 1, W), conv_w, row(conv_b), w_r.astype(BF16), w_i.astype(BF16),
      row(b_r), row(b_i), row(lam))
    return y, h1.reshape(B, W), cv1


def _outproj_kernel(ya_ref, yb_ref, wt_ref, wb_ref, x_ref, g_ref, o_ref):
    bb, tt, wa = ya_ref.shape
    wb = yb_ref.shape[2]
    mix = (_dot(ya_ref[...].reshape(bb * tt, wa), wt_ref[...])
           + _dot(yb_ref[...].reshape(bb * tt, wb), wb_ref[...]))
    o_ref[...] = x_ref[...] + g_ref[...] * mix.reshape(bb, tt, mix.shape[-1])


def _outproj(ya, yb, w_out, x, g1):
    B, T, D = x.shape
    wa, wb = ya.shape[2], yb.shape[2]
    assert wa == wb
    bb, tt = _row_tiling(B, T, 1024)
    tn = _pick(D, 1024, LANES)
    return pl.pallas_call(
        _outproj_kernel,
        grid=(B // bb, T // tt, D // tn),
        in_specs=[pl.BlockSpec((bb, tt, wa), lambda b, t, j: (b, t, 0)),
                  pl.BlockSpec((bb, tt, wb), lambda b, t, j: (b, t, 0)),
                  pl.BlockSpec((wa, tn), lambda b, t, j: (0, j)),
                  pl.BlockSpec((wb, tn), lambda b, t, j: (1, j)),
                  pl.BlockSpec((bb, tt, tn), lambda b, t, j: (b, t, j)),
                  pl.BlockSpec((bb, 1, tn), lambda b, t, j: (b, 0, j))],
        out_specs=pl.BlockSpec((bb, tt, tn), lambda b, t, j: (b, t, j)),
        out_shape=jax.ShapeDtypeStruct((B, T, D), F32),
        compiler_params=_cparams("arbitrary", "arbitrary", "arbitrary"),
        name="outproj",
    )(ya, yb, w_out, w_out, x, g1)


def _ffn_kernel(x_ref, ng_ref, sc_ref, sh_ref, g2_ref, *rest, n_col_chunks, final_norm, tiles_per_step, n_tiles):
    w_refs, (gf_ref, o_ref, h_s) = rest[:3 * tiles_per_step], rest[3 * tiles_per_step:]
    j = pl.program_id(2)
    last = pl.num_programs(2) - 1
    bb, tt, D = x_ref.shape

    def row_loop(body, rc):
        assert tt % rc == 0
        for b in range(bb):
            for i in range(tt // rc):
                body(b, i * rc)

    @pl.when(j == 0)
    def _():
        rc = 2 * SUBLANES

        def half(b, r0):
            x = x_ref[b, pl.ds(r0, SUBLANES), :]
            r = lax.rsqrt(jnp.mean(x * x, axis=-1, keepdims=True) + EPS)
            return x * r * (ng_ref[0] * (1.0 + sc_ref[b])) + sh_ref[b]

        def body(b, r0):
            h = jnp.concatenate([half(b, r0), half(b, r0 + SUBLANES)], axis=0)
            h_s[pl.ds(b * tt + r0, rc), :] = h.astype(BF16)
            o_ref[b, pl.ds(r0, rc), :] = jnp.zeros((rc, D), F32)
        row_loop(body, rc)

    def hidden_tiles(count):
        hb = h_s[...]
        cw = D // n_col_chunks
        for k in range(count):
            wg_ref, wu_ref, wd_ref = w_refs[3 * k:3 * k + 3]
            gate = _dot(hb, wg_ref[...])
            up = _dot(hb, wu_ref[...])
            act = (gate * _sigmoid(gate) * up).astype(BF16)
            for c in range(n_col_chunks):
                cols = slice(c * cw, (c + 1) * cw)
                o_ref[:, :, cols] += _dot(act, wd_ref[:, cols]).reshape(bb, tt, cw)

    tail = n_tiles - (pl.cdiv(n_tiles, tiles_per_step) - 1) * tiles_per_step
    if tail == tiles_per_step:
        hidden_tiles(tiles_per_step)
    else:
        @pl.when(j < last)
        def _():
            hidden_tiles(tiles_per_step)

        @pl.when(j == last)
        def _():
            hidden_tiles(tail)

    @pl.when(j == last)
    def _():
        def body(b, r0):
            rows = pl.ds(r0, SUBLANES)
            x2 = x_ref[b, rows, :] + g2_ref[b] * o_ref[b, rows, :]
            if final_norm:
                x2 = x2 * lax.rsqrt(jnp.mean(x2 * x2, axis=-1, keepdims=True) + EPS) * gf_ref[0]
            o_ref[b, rows, :] = x2
        row_loop(body, SUBLANES)


def _ffn(x, ng, sc, sh, g2, w_g, w_u, w_down, gf, final_norm):
    B, T, D = x.shape
    F = w_down.shape[0]
    tf = _pick(F, 256, LANES)
    nf = F // tf
    u_off = nf if w_u.shape[1] == 2 * F else 0
    bb, tt = _row_tiling(B, T, 512)
    tps = 2
    w_specs, w_args = [], []
    for k in range(tps):
        tile = lambda j, k=k: jnp.minimum(j * tps + k, nf - 1)
        w_specs += [pl.BlockSpec((D, tf), lambda b, t, j, tile=tile: (0, tile(j))),
                    pl.BlockSpec((D, tf), lambda b, t, j, tile=tile: (0, u_off + tile(j))),
                    pl.BlockSpec((tf, D), lambda b, t, j, tile=tile: (tile(j), 0))]
        w_args += [w_g, w_u, w_down]
    return pl.pallas_call(
        functools.partial(_ffn_kernel, n_col_chunks=max(1, D // 1024), final_norm=final_norm,
                          tiles_per_step=tps, n_tiles=nf),
        grid=(B // bb, T // tt, pl.cdiv(nf, tps)),
        in_specs=[pl.BlockSpec((bb, tt, D), lambda b, t, j: (b, t, 0), pipeline_mode=pl.Buffered(1)),
                  pl.BlockSpec((1, 1, D), lambda b, t, j: (0, 0, 0)),
                  pl.BlockSpec((bb, 1, D), lambda b, t, j: (b, 0, 0)),
                  pl.BlockSpec((bb, 1, D), lambda b, t, j: (b, 0, 0)),
                  pl.BlockSpec((bb, 1, D), lambda b, t, j: (b, 0, 0))] + w_specs + [
                  pl.BlockSpec((1, 1, D), lambda b, t, j: (0, 0, 0))],
        out_specs=pl.BlockSpec((bb, tt, D), lambda b, t, j: (b, t, 0)),
        out_shape=jax.ShapeDtypeStruct((B, T, D), F32),
        scratch_shapes=[pltpu.VMEM((bb * tt, D), BF16)],
        compiler_params=_cparams("arbitrary", "arbitrary", "arbitrary"),
        name="ffn",
    )(x, ng, sc, sh, g2, *w_args, gf)


def _mixer(x, ada, state, p, bf16_weights, cast_jobs):
    sh1, sc1, g1 = ada
    c0, n0, m0, hl0, cv0 = state
    B, T, D = x.shape
    _, H, dk, dv = c0.shape
    W = cv0.shape[2]
    n_qk, n_v = 2 * H * dk, H * dv
    assert n_qk == n_v and (3 * n_v) % LANES == 0
    h1, gates = _normmod_gates(x, p["norm1_g"], sc1, sh1, p["w_in"], 3 * n_v // LANES, p["b_gates"], H)
    h1 = h1.reshape(B * T, D)
    jobs = list(cast_jobs.items())

    def proj(w, blk, n, dt, ep, cs=None):
        name, job = jobs.pop() if jobs else (None, None)
        out = _inproj(h1, w, blk, n, dt, ep, cs, job)
        if job is not None:
            out, bf16_weights[name] = out
        return out.reshape(B, T, n)

    qk = proj(p["w_in"], 0, n_qk, BF16, "colscale", p["qk_scale"])
    v = proj(p["w_in"], 1, n_v, BF16, "none")
    og = proj(p["w_in"], 2, n_v, F32, "sigmoid")
    xb = proj(p["w_xg"], 0, W, F32, "none")
    gg = proj(p["w_xg"], 1, W, F32, "gelu")
    assert not jobs
    ya, c1, n1, m1 = _mlstm(qk, v, og, gates, c0, n0, m0, p["head_norm_g"])
    yb, hl1, cv1 = _rglru(xb, gg, cv0, hl0, p["conv_w"], p["conv_b"],
                          p["w_r"], p["w_i"], p["b_r"], p["b_i"], p["lru_lambda"])
    return _outproj(ya, yb, bf16_weights["w_out"], x, g1), (c1, n1, m1, hl1, cv1)


def kernel(x_prompt, x_sample, c_prompt, c_sample, state_mlstm_C, state_mlstm_n, state_mlstm_m, state_lru_h, state_conv, w_ada, b_ada, norm1_g, norm2_g, w_in, b_gates_a, head_norm_g, conv_w, conv_b, w_r, b_r, w_i, b_i, lru_lambda, w_out, w_gu, w_down, normf_g):
    depth = w_in.shape[0]
    Bp, Tp, D = x_prompt.shape
    Bs = x_sample.shape[0]
    H, dk, dv = state_mlstm_C.shape[2:]
    W = conv_w.shape[2]
    taps = conv_w.shape[1]
    c_x = 2 * H * dk + 2 * H * dv + 2 * H

    xp, xs = x_prompt, x_sample
    p_states, s_states = [], []
    R = Bp + Bs
    Rpad = -(-R // SUBLANES) * SUBLANES
    c_all = jnp.concatenate([c_prompt, c_sample, jnp.zeros((Rpad - R, D), F32)], axis=0)
    gf = normf_g.reshape(1, 1, D)
    qk_scale = jnp.concatenate([jnp.ones((1, H * dk), F32), jnp.full((1, H * dk), dk ** -0.5, F32)], axis=1)
    for l in range(depth):
        w_in_l = w_in[l]
        p = {
            "norm1_g": norm1_g[l].reshape(1, 1, D),
            "w_in": w_in_l.astype(BF16),
            "w_xg": w_in_l[:, c_x:].astype(BF16),
            "qk_scale": qk_scale,
            "b_gates": jnp.pad(b_gates_a[l].reshape(1, 2 * H), ((0, 0), (0, LANES - 2 * H))),
            "head_norm_g": head_norm_g[l],
            "conv_w": conv_w[l], "conv_b": conv_b[l],
            "w_r": w_r[l], "w_i": w_i[l], "b_r": b_r[l], "b_i": b_i[l],
            "lru_lambda": lru_lambda[l],
        }
        steps = (Bp * Tp) // _inproj_tile(Bp * Tp)
        planned = {"w_g": _cast_job(w_gu[l], steps, 2, 0), "w_u": _cast_job(w_gu[l], steps, 2, 1),
                   "w_down": _cast_job(w_down[l], steps), "w_out": _cast_job(w_out[l], steps)}
        cast_jobs = {k: j for k, j in planned.items() if j is not None}
        bf16_weights = {}
        if "w_g" not in cast_jobs or "w_u" not in cast_jobs:
            cast_jobs.pop("w_g", None), cast_jobs.pop("w_u", None)
            bf16_weights["w_g"] = bf16_weights["w_u"] = w_gu[l].astype(BF16)
        if "w_down" not in cast_jobs:
            bf16_weights["w_down"] = w_down[l].astype(BF16)
        if "w_out" not in cast_jobs:
            bf16_weights["w_out"] = w_out[l].astype(BF16)
        norm2 = norm2_g[l].reshape(1, 1, D)
        ada = _ada(c_all, w_ada[l], b_ada[l].reshape(1, -1))
        ada_p = [ada[:Bp, i * D:(i + 1) * D].reshape(Bp, 1, D) for i in range(6)]
        ada_s = [ada[Bp:R, i * D:(i + 1) * D].reshape(Bs, 1, D) for i in range(6)]

        zero_state = (jnp.zeros((Bp, H, dk, dv), F32), jnp.zeros((Bp, H, dk), F32),
                      jnp.zeros((Bp, H), F32), jnp.zeros((Bp, W), F32),
                      jnp.zeros((Bp, taps - 1, W), F32))
        cache_state = (state_mlstm_C[l], state_mlstm_n[l], state_mlstm_m[l], state_lru_h[l],
                       state_conv[l])
        last = l == depth - 1
        new = []
        for x, (sh1, sc1, g1, sh2, sc2, g2), st in ((xp, ada_p, zero_state), (xs, ada_s, cache_state)):
            x1, st1 = _mixer(x, (sh1, sc1, g1), st, p, bf16_weights, cast_jobs)
            cast_jobs = {}
            y = _ffn(x1, norm2, sc2, sh2, g2, bf16_weights["w_g"], bf16_weights["w_u"],
                     bf16_weights["w_down"], gf, final_norm=last)
            new.append((y, st1))
        (xp, st_p), (xs, st_s) = new
        p_states.append(st_p)
        s_states.append(st_s)

    stack = lambda sts, i: jnp.stack([s[i] for s in sts])
    return (xp, xs) + tuple(stack(p_states, i) for i in range(5)) + tuple(stack(s_states, i) for i in range(5))
```

```python
import functools

import jax
import jax.numpy as jnp
from jax import lax
from jax.experimental import pallas as pl
from jax.experimental.pallas import tpu as pltpu

F32 = jnp.float32
BF16 = jnp.bfloat16
EPS = 1e-6
LRU_C = 8.0
LANES = 128
SUBLANES = 8
VMEM_LIMIT_BYTES = 60 * 1024 * 1024


def _cparams(*sem):
    return pltpu.CompilerParams(dimension_semantics=sem, vmem_limit_bytes=VMEM_LIMIT_BYTES)


def _pick(total, target, quantum):
    if total <= target:
        return total
    best = None
    for cand in range(quantum, target + 1, quantum):
        if total % cand == 0:
            best = cand
    assert best is not None, (total, target, quantum)
    return best


def _row_tiling(B, T, target_rows):
    if T >= target_rows:
        return 1, _pick(T, target_rows, 16)
    bb = _pick(B, max(1, target_rows // T), 1)
    return bb, T


def _mlstm_chunk(T):
    return _pick(T, LANES, 16)


def _log_sigmoid(x):
    return jnp.minimum(x, 0.0) - jnp.log1p(jnp.exp(-jnp.abs(x)))


def _sigmoid(x):
    return 0.5 * (1.0 + jnp.tanh(0.5 * x))


def _gelu_tanh(x):
    c = 0.7978845608028654
    return 0.5 * x * (1.0 + jnp.tanh(c * (x + 0.044715 * (x * x * x))))


def _dot(a, b):
    return jnp.dot(a, b, preferred_element_type=F32)


def _dot_nt(a, b):
    return lax.dot_general(a, b, (((1,), (1,)), ((), ())), preferred_element_type=F32)


def _dot_tn(a, b):
    return lax.dot_general(a, b, (((0,), (0,)), ((), ())), preferred_element_type=F32)


def _split_bf16(x, terms):
    out = []
    for _ in range(terms - 1):
        t = x.astype(BF16)
        out.append(t)
        x = x - t.astype(F32)
    out.append(x.astype(BF16))
    return out


def _ada_kernel(c_ref, w_ref, b_ref, o_ref):
    c = c_ref[...]
    s = (c * _sigmoid(c)).astype(BF16)
    o_ref[...] = _dot(s, w_ref[...].astype(BF16)) + b_ref[...]


def _ada(c, w, b):
    R, D = c.shape
    N = w.shape[1]
    tn = _pick(N, 512, LANES)
    return pl.pallas_call(
        _ada_kernel,
        grid=(N // tn,),
        in_specs=[pl.BlockSpec((R, D), lambda j: (0, 0)),
                  pl.BlockSpec((D, tn), lambda j: (0, j)),
                  pl.BlockSpec((1, tn), lambda j: (0, j))],
        out_specs=pl.BlockSpec((R, tn), lambda j: (0, j)),
        out_shape=jax.ShapeDtypeStruct((R, N), F32),
        compiler_params=_cparams("arbitrary"),
        name="ada",
    )(c, w, b)


def _normmod(x, geff, sh):
    r = lax.rsqrt(jnp.mean(x * x, axis=-1, keepdims=True) + EPS)
    return x * r * geff + sh


def _normmod_gates_kernel(x_ref, g_ref, sc_ref, sh_ref, wg_ref, bg_ref, h_ref, gr_ref, gt_s, *, n_heads, chunk):
    bb, tt, D = x_ref.shape
    H = n_heads
    hb = _normmod(x_ref[...], g_ref[...] * (1.0 + sc_ref[...]), sh_ref[...]).astype(BF16)
    h_ref[...] = hb
    pre = _dot(hb.reshape(bb * tt, D), wg_ref[...]) + bg_ref[...]
    lane = lax.broadcasted_iota(jnp.int32, pre.shape, 1)
    gates = jnp.where(lane < H, pre, _log_sigmoid(pre))
    rows, lanes_out = gt_s.shape[2], gt_s.shape[3]
    if lanes_out > chunk:
        gt_s[...] = jnp.zeros_like(gt_s)
    blocks = [(i, c) for i in range(bb) for c in range(tt // chunk)]
    for i, c in blocks:
        r0 = i * tt + c * chunk
        gt_s[i, c, :, 0:chunk] = gates[r0:r0 + chunk, :].T[:rows, :]
    ig = jnp.concatenate([gt_s[i, c, 0:H, :] for i, c in blocks], axis=0)
    lf = jnp.concatenate([gt_s[i, c, H:2 * H, :] for i, c in blocks], axis=0)
    lane_t = lax.broadcasted_iota(jnp.int32, lf.shape, 1)
    bcum = _lane_scan(lf, jnp.add, lane_t, 0.0)
    a = ig - bcum
    cm = _lane_scan(a, jnp.maximum, lane_t, -jnp.inf)
    if gr_ref.shape[2] > 3 * H:
        gr_ref[...] = jnp.zeros_like(gr_ref)
    for k, (i, c) in enumerate(blocks):
        for r, val in enumerate((a, cm, bcum)):
            gr_ref[i, c, r * H:(r + 1) * H, :] = val[k * H:(k + 1) * H, :]


def _normmod_gates(x, g, sc, sh, wg, gate_blk, bg, n_heads):
    B, T, D = x.shape
    L = _mlstm_chunk(T)
    bb, tt = _row_tiling(B, T, 512)
    assert tt % L == 0
    GR = -(-2 * n_heads // SUBLANES) * SUBLANES
    GR3 = -(-3 * n_heads // SUBLANES) * SUBLANES
    LP = max(L, LANES)
    return pl.pallas_call(
        functools.partial(_normmod_gates_kernel, n_heads=n_heads, chunk=L),
        grid=(B // bb, T // tt),
        in_specs=[pl.BlockSpec((bb, tt, D), lambda b, t: (b, t, 0)),
                  pl.BlockSpec((1, 1, D), lambda b, t: (0, 0, 0)),
                  pl.BlockSpec((bb, 1, D), lambda b, t: (b, 0, 0)),
                  pl.BlockSpec((bb, 1, D), lambda b, t: (b, 0, 0)),
                  pl.BlockSpec((D, LANES), lambda b, t: (0, gate_blk)),
                  pl.BlockSpec((1, LANES), lambda b, t: (0, 0))],
        out_specs=[pl.BlockSpec((bb, tt, D), lambda b, t: (b, t, 0)),
                   pl.BlockSpec((bb, tt // L, GR3, LP), lambda b, t: (b, t, 0, 0))],
        out_shape=[jax.ShapeDtypeStruct((B, T, D), BF16),
                   jax.ShapeDtypeStruct((B, T // L, GR3, LP), F32)],
        scratch_shapes=[pltpu.VMEM((bb, tt // L, GR, LP), F32)],
        compiler_params=_cparams("arbitrary", "arbitrary"),
        name="normmod_gates",
    )(x, g, sc, sh, wg, bg)


def _inproj_kernel(a_ref, w_ref, *rest, epilogue, with_cast):
    rest = list(rest)
    if with_cast:
        cast_out = rest.pop()
        o_ref = rest.pop()
        cast_out[...] = rest.pop()[...].astype(BF16)
    else:
        o_ref = rest.pop()
    acc = _dot(a_ref[...], w_ref[...])
    if epilogue == "colscale":
        acc = acc * rest[0][...]
    elif epilogue == "sigmoid":
        acc = _sigmoid(acc)
    elif epilogue == "gelu":
        acc = _gelu_tanh(acc)
    o_ref[...] = acc.astype(o_ref.dtype)


def _inproj_tile(M):
    return _pick(M, 512, 16)


def _cast_job(w, steps, col_blocks=1, col_block=0):
    R, C = w.shape
    Cb = C // col_blocks
    if C % col_blocks or (col_blocks > 1 and Cb % LANES):
        return None
    if R % (steps * 2 * SUBLANES) == 0:
        return w, (R // steps, Cb), (lambda i: (i, col_block)), (lambda i: (i, 0)), (R, Cb)
    if col_blocks == 1 and C % (steps * LANES) == 0:
        return w, (R, C // steps), (lambda i: (0, i)), (lambda i: (0, i)), (R, C)
    return None


def _inproj(a, w, col_block, N, out_dtype, epilogue, colscale=None, cast=None):
    M, K = a.shape
    tm = _inproj_tile(M)
    in_specs = [pl.BlockSpec((tm, K), lambda i: (i, 0)),
                pl.BlockSpec((K, N), lambda i: (0, col_block), pipeline_mode=pl.Buffered(1))]
    args = [a, w]
    if epilogue == "colscale":
        in_specs.append(pl.BlockSpec((1, N), lambda i: (0, 0)))
        args.append(colscale)
    out_specs = [pl.BlockSpec((tm, N), lambda i: (i, 0))]
    out_shape = [jax.ShapeDtypeStruct((M, N), out_dtype)]
    if cast is not None:
        src, blk, src_map, dst_map, shape = cast
        in_specs.append(pl.BlockSpec(blk, src_map))
        args.append(src)
        out_specs.append(pl.BlockSpec(blk, dst_map))
        out_shape.append(jax.ShapeDtypeStruct(shape, BF16))
    outs = pl.pallas_call(
        functools.partial(_inproj_kernel, epilogue=epilogue, with_cast=cast is not None),
        grid=(M // tm,),
        in_specs=in_specs,
        out_specs=out_specs,
        out_shape=out_shape,
        compiler_params=_cparams("arbitrary"),
        name="inproj_" + epilogue,
    )(*args)
    return outs if cast is not None else outs[0]


def _lane_scan(x, op, lane, identity):
    d = 1
    while d < x.shape[1]:
        x = op(x, jnp.where(lane >= d, pltpu.roll(x, d, axis=1), identity))
        d *= 2
    return x


def _mlstm_kernel(qk_ref, v_ref, og_ref, gr_ref, c0_ref, n0_ref, m0_ref, hg_ref,
                  y_ref, c_ref, n_ref, m_ref, cn_s, m_s, *, chunk, n_heads, dk, dv, head_group):
    tb = pl.program_id(1)
    L, H = chunk, n_heads
    Tb = qk_ref.shape[1]
    nc = Tb // L
    LP = gr_ref.shape[3]
    assert dv % LANES == 0 and L <= LANES
    rep = lambda x, n: jnp.concatenate([x] * n, axis=1)

    eye_k = (lax.broadcasted_iota(jnp.int32, (dk, dk), 0) == lax.broadcasted_iota(jnp.int32, (dk, dk), 1))

    @pl.when(tb == 0)
    def _():
        for h in range(H):
            cn_s[h, :, 0:dv] = c0_ref[0, h]
            n_col = jnp.sum(jnp.where(eye_k, n0_ref[0, h:h + 1, :], 0.0), axis=1, keepdims=True)
            cn_s[h, :, dv:dv + LANES] = jnp.broadcast_to(n_col, (dk, LANES))
        m_s[...] = jnp.broadcast_to(m0_ref[0], m_s.shape)

    row = lax.broadcasted_iota(jnp.int32, (L, L), 0)
    col = lax.broadcasted_iota(jnp.int32, (L, L), 1)
    tril = col <= row
    nr = 3 * H
    spread = ((lax.broadcasted_iota(jnp.int32, (3 * nr, nr * LANES), 1) // LANES)
              == (lax.broadcasted_iota(jnp.int32, (3 * nr, nr * LANES), 0) % nr)).astype(BF16)
    ones_2l = jnp.ones((2 * L, LANES), BF16)
    mean_2v = jnp.full((2 * dv, LANES), 1.0 / dv, BF16)
    assert dv & (dv - 1) == 0

    def chunk_step(c, carry):
        sl = pl.ds(pl.multiple_of(c * L, L), L)
        rows = gr_ref[0, tb * nc + c][0:nr]
        a = rows[0:H]
        cols = _dot_tn(jnp.concatenate(_split_bf16(rows, 3), axis=0), spread)
        blk = lambda r: cols[0:L, r * LANES:(r + 1) * LANES]
        m_all = m_s[...]
        nv = dv // LANES
        m_new = {}
        for h0 in range(0, H, head_group):
            heads = range(h0, min(h0 + head_group, H))
            qb = {h: qk_ref[0, sl, h * dk:(h + 1) * dk] for h in heads}
            kb = {h: qk_ref[0, sl, (H + h) * dk:(H + h + 1) * dk] for h in heads}
            vb = {h: v_ref[0, sl, h * dv:(h + 1) * dv] for h in heads}
            cn = {h: cn_s[h] for h in heads}
            qk = {h: _dot_nt(qb[h], kb[h]) for h in heads}
            qc = {h: _dot(qb[h], cn[h].astype(BF16)) for h in heads}

            g, s_inter, sv, rs = {}, {}, {}, {}
            for h in heads:
                g[h] = jnp.maximum(m_all[h:h + 1, :], blk(H + h))
                s_inter[h] = jnp.exp(m_all[h:h + 1, :] - g[h])
                w = jnp.where(tril, jnp.exp(jnp.minimum(a[h:h + 1, 0:L] - g[h][:, 0:L], 0.0)), 0.0)
                s2 = _split_bf16(qk[h] * w, 2)
                sv[h] = _dot(s2[0], vb[h])
                rs[h] = _dot(jnp.concatenate(s2, axis=1), ones_2l)

            hh, ms = {}, {}
            for h in heads:
                num = rep(s_inter[h], nv) * qc[h][:, 0:dv] + sv[h]
                den = s_inter[h] * qc[h][:, dv:dv + LANES] + rs[h]
                inv = 1.0 / jnp.maximum(jnp.abs(den), jnp.exp(-(blk(2 * H + h) + g[h])))
                hh[h] = num * rep(inv, nv)
                ms[h] = _dot(jnp.concatenate(_split_bf16(hh[h] * hh[h], 2), axis=1), mean_2v)

            for h in heads:
                hn = hh[h] * rep(lax.rsqrt(ms[h] + EPS), nv) * hg_ref[h:h + 1, :]
                y_ref[0, sl, h * dv:(h + 1) * dv] = (hn * og_ref[0, sl, h * dv:(h + 1) * dv]).astype(y_ref.dtype)
                g_last = g[h][L - 1:L, :]
                w_last = jnp.exp(blk(h) - g_last)
                upd_rhs = jnp.concatenate([rep(w_last, nv) * vb[h].astype(F32), w_last], axis=1)
                decay = jnp.exp(m_all[h:h + 1, :] - g_last)
                cn_s[h] = rep(decay, nv + 1) * cn[h] + _dot_tn(kb[h], upd_rhs.astype(BF16))
                m_new[h] = blk(2 * H + h)[L - 1:L, :] + g_last
        m_s[...] = jnp.concatenate([m_new[h] for h in range(H)], axis=0)
        return carry

    lax.fori_loop(0, nc, chunk_step, 0)

    @pl.when(tb == pl.num_programs(1) - 1)
    def _():
        for h in range(H):
            c_ref[0, h] = cn_s[h, :, 0:dv]
            n_ref[0, h:h + 1, :] = jnp.sum(jnp.where(eye_k, cn_s[h, :, dv:dv + 1], 0.0), axis=0, keepdims=True)
        m_ref[0] = m_s[...]


def _mlstm(qk, v, og, gates, c0, n0, m0, head_g):
    B, T, _ = qk.shape
    _, H, dk, dv = c0.shape
    L = _mlstm_chunk(T)
    Tb = _pick(T, 512, L)
    _, NC, GR, LP = gates.shape
    y, c1, n1, m1 = pl.pallas_call(
        functools.partial(_mlstm_kernel, chunk=L, n_heads=H, dk=dk, dv=dv, head_group=4),
        grid=(B, T // Tb),
        in_specs=[pl.BlockSpec((1, Tb, 2 * H * dk), lambda b, t: (b, t, 0)),
                  pl.BlockSpec((1, Tb, H * dv), lambda b, t: (b, t, 0)),
                  pl.BlockSpec((1, Tb, H * dv), lambda b, t: (b, t, 0)),
                  pl.BlockSpec((1, NC, GR, LP), lambda b, t: (b, 0, 0, 0)),
                  pl.BlockSpec((1, H, dk, dv), lambda b, t: (b, 0, 0, 0)),
                  pl.BlockSpec((1, H, dk), lambda b, t: (b, 0, 0)),
                  pl.BlockSpec((1, H, 1), lambda b, t: (b, 0, 0)),
                  pl.BlockSpec((H, dv), lambda b, t: (0, 0))],
        out_specs=[pl.BlockSpec((1, Tb, H * dv), lambda b, t: (b, t, 0)),
                   pl.BlockSpec((1, H, dk, dv), lambda b, t: (b, 0, 0, 0)),
                   pl.BlockSpec((1, H, dk), lambda b, t: (b, 0, 0)),
                   pl.BlockSpec((1, H, LANES), lambda b, t: (b, 0, 0))],
        out_shape=[jax.ShapeDtypeStruct((B, T, H * dv), BF16),
                   jax.ShapeDtypeStruct((B, H, dk, dv), F32),
                   jax.ShapeDtypeStruct((B, H, dk), F32),
                   jax.ShapeDtypeStruct((B, H, LANES), F32)],
        scratch_shapes=[pltpu.VMEM((H, dk, dv + LANES), F32),
                        pltpu.VMEM((H, LANES), F32)],
        compiler_params=_cparams("arbitrary", "arbitrary"),
        name="mlstm",
    )(qk, v, og, gates, c0, n0, m0.reshape(B, H, 1), head_g)
    return y, c1, n1, m1[:, :, 0]


def _scan_pitch(rows):
    quads = -(-(-(-rows // SUBLANES)) // 4)
    return 4 * (quads + 1 - quads % 2)


def _rglru_kernel(xb_ref, gg_ref, cv0_ref, h0_ref, cw_ref, cb_ref, wr_ref, wi_ref, br_ref, bi_ref,
                  lam_ref, y_ref, h_ref, cv_ref, xext, a_s, u_s, *, pitch, group):
    tb = pl.program_id(1)
    Tb, W = xb_ref.shape[1], xb_ref.shape[2]
    nb, bw = wr_ref.shape[0], wr_ref.shape[1]
    taps = cw_ref.shape[0]
    rows = a_s.shape[1]
    pad = SUBLANES

    @pl.when(tb == 0)
    def _():
        h_ref[...] = h0_ref[...]
        cv_ref[...] = cv0_ref[...]

    x = xb_ref[0]
    xext[pl.ds(pad, Tb), :] = x
    xext[pl.ds(pad - (taps - 1), taps - 1), :] = cv_ref[0]
    xc = cb_ref[...] + cw_ref[taps - 1:taps, :] * x
    for j in range(taps - 1):
        xc = xc + cw_ref[j:j + 1, :] * xext[pl.ds(pad - (taps - 1) + j, Tb), :]
    cv_ref[0] = xext[pl.ds(pad + Tb - (taps - 1), taps - 1), :]

    log_sig_lam = _log_sigmoid(lam_ref[...])
    for n in range(nb):
        cols = slice(n * bw, (n + 1) * bw)
        xn = xc[:, cols]
        xnb = xn.astype(BF16)
        r = _sigmoid(_dot(xnb, wr_ref[n]) + br_ref[:, cols])
        i = _sigmoid(_dot(xnb, wi_ref[n]) + bi_ref[:, cols])
        log_a = LRU_C * r * log_sig_lam[:, cols]
        a = jnp.exp(log_a)
        a_s[n, 0:Tb, :] = a
        z = -jnp.tanh(log_a) * (1.0 + a * a)
        u_s[n, 0:Tb, :] = jnp.where(z > 0.0, z * lax.rsqrt(z), 0.0) * (i * xn)
        if rows > Tb:
            a_s[n, Tb:rows, :] = jnp.ones((rows - Tb, bw), F32)
            u_s[n, Tb:rows, :] = jnp.zeros((rows - Tb, bw), F32)

    for n0 in range(0, nb, group):
        slabs = list(range(n0, min(n0 + group, nb)))

        def pass1(i, carry):
            idx = pl.ds(i, SUBLANES, stride=pitch)
            out = []
            for (hz, ac), n in zip(carry, slabs):
                av = a_s[n, idx, :]
                hz = av * hz + u_s[n, idx, :]
                ac = av * ac
                u_s[n, idx, :] = hz
                a_s[n, idx, :] = ac
                out.append((hz, ac))
            return tuple(out)

        init = tuple((jnp.zeros((SUBLANES, bw), F32), jnp.ones((SUBLANES, bw), F32)) for _ in slabs)
        ends = lax.fori_loop(0, pitch, pass1, init, unroll=2)

        starts = []
        for (hz, ac), n in zip(ends, slabs):
            cols = slice(n * bw, (n + 1) * bw)
            h = h_ref[0, :, cols]
            per_block = []
            for j in range(SUBLANES):
                per_block.append(h)
                h = hz[j:j + 1, :] + ac[j:j + 1, :] * h
            h_ref[0, :, cols] = h
            starts.append(jnp.concatenate(per_block, axis=0))

        def pass2(i, carry):
            idx = pl.ds(i, SUBLANES, stride=pitch)
            for st, n in zip(starts, slabs):
                u_s[n, idx, :] = u_s[n, idx, :] + a_s[n, idx, :] * st
            return carry

        lax.fori_loop(0, pitch, pass2, 0, unroll=2)

    for n in range(nb):
        cols = slice(n * bw, (n + 1) * bw)
        y_ref[0, :, cols] = (u_s[n, 0:Tb, :] * gg_ref[0, :, cols]).astype(y_ref.dtype)


def _rglru(xb, gg, cv0, h0, conv_w, conv_b, w_r, w_i, b_r, b_i, lam):
    B, T, W = xb.shape
    taps = conv_w.shape[0]
    nb, bw = w_r.shape[0], w_r.shape[1]
    assert T >= taps - 1 and bw == LANES and nb * bw == W
    Tb = _pick(T, 256, SUBLANES)
    pitch = _scan_pitch(Tb)
    row = lambda a: a.reshape(1, W)
    y, h1, cv1 = pl.pallas_call(
        functools.partial(_rglru_kernel, pitch=pitch, group=8),
        grid=(B, T // Tb),
        in_specs=[pl.BlockSpec((1, Tb, W), lambda b, t: (b, t, 0)),
                  pl.BlockSpec((1, Tb, W), lambda b, t: (b, t, 0)),
                  pl.BlockSpec((1, taps - 1, W), lambda b, t: (b, 0, 0)),
                  pl.BlockSpec((1, 1, W), lambda b, t: (b, 0, 0)),
                  pl.BlockSpec((taps, W), lambda b, t: (0, 0)),
                  pl.BlockSpec((1, W), lambda b, t: (0, 0)),
                  pl.BlockSpec((nb, bw, bw), lambda b, t: (0, 0, 0)),
                  pl.BlockSpec((nb, bw, bw), lambda b, t: (0, 0, 0)),
                  pl.BlockSpec((1, W), lambda b, t: (0, 0)),
                  pl.BlockSpec((1, W), lambda b, t: (0, 0)),
                  pl.BlockSpec((1, W), lambda b, t: (0, 0))],
        out_specs=[pl.BlockSpec((1, Tb, W), lambda b, t: (b, t, 0)),
                   pl.BlockSpec((1, 1, W), lambda b, t: (b, 0, 0)),
                   pl.BlockSpec((1, taps - 1, W), lambda b, t: (b, 0, 0))],
        out_shape=[jax.ShapeDtypeStruct((B, T, W), BF16),
                   jax.ShapeDtypeStruct((B, 1, W), F32),
                   jax.ShapeDtypeStruct((B, taps - 1, W), F32)],
        scratch_shapes=[pltpu.VMEM((Tb + SUBLANES, W), F32),
                        pltpu.VMEM((nb, SUBLANES * pitch, bw), F32),
                        pltpu.VMEM((nb, SUBLANES * pitch, bw), F32)],
        compiler_params=_cparams("arbitrary", "arbitrary"),
        name="rglru",
    )(xb, gg, cv0, h0.reshape(B, 1, W), conv_w, row(conv_b), w_r.astype(BF16), w_i.astype(BF16),
      row(b_r), row(b_i), row(lam))
    return y, h1.reshape(B, W), cv1


def _outproj_kernel(ya_ref, yb_ref, wt_ref, wb_ref, x_ref, g_ref, o_ref):
    bb, tt, wa = ya_ref.shape
    wb = yb_ref.shape[2]
    mix = (_dot(ya_ref[...].reshape(bb * tt, wa), wt_ref[...])
           + _dot(yb_ref[...].reshape(bb * tt, wb), wb_ref[...]))
    o_ref[...] = x_ref[...] + g_ref[...] * mix.reshape(bb, tt, mix.shape[-1])


def _outproj(ya, yb, w_out, x, g1):
    B, T, D = x.shape
    wa, wb = ya.shape[2], yb.shape[2]
    assert wa == wb
    bb, tt = _row_tiling(B, T, 1024)
    tn = _pick(D, 1024, LANES)
    return pl.pallas_call(
        _outproj_kernel,
        grid=(B // bb, T // tt, D // tn),
        in_specs=[pl.BlockSpec((bb, tt, wa), lambda b, t, j: (b, t, 0)),
                  pl.BlockSpec((bb, tt, wb), lambda b, t, j: (b, t, 0)),
                  pl.BlockSpec((wa, tn), lambda b, t, j: (0, j)),
                  pl.BlockSpec((wb, tn), lambda b, t, j: (1, j)),
                  pl.BlockSpec((bb, tt, tn), lambda b, t, j: (b, t, j)),
                  pl.BlockSpec((bb, 1, tn), lambda b, t, j: (b, 0, j))],
        out_specs=pl.BlockSpec((bb, tt, tn), lambda b, t, j: (b, t, j)),
        out_shape=jax.ShapeDtypeStruct((B, T, D), F32),
        compiler_params=_cparams("arbitrary", "arbitrary", "arbitrary"),
        name="outproj",
    )(ya, yb, w_out, w_out, x, g1)


def _ffn_kernel(x_ref, ng_ref, sc_ref, sh_ref, g2_ref, *rest, n_col_chunks, final_norm, tiles_per_step, n_tiles):
    w_refs, (gf_ref, o_ref, h_s) = rest[:3 * tiles_per_step], rest[3 * tiles_per_step:]
    j = pl.program_id(2)
    last = pl.num_programs(2) - 1
    bb, tt, D = x_ref.shape

    def row_loop(body, rc):
        assert tt % rc == 0
        for b in range(bb):
            for i in range(tt // rc):
                body(b, i * rc)

    @pl.when(j == 0)
    def _():
        rc = 2 * SUBLANES

        def half(b, r0):
            x = x_ref[b, pl.ds(r0, SUBLANES), :]
            r = lax.rsqrt(jnp.mean(x * x, axis=-1, keepdims=True) + EPS)
            return x * r * (ng_ref[0] * (1.0 + sc_ref[b])) + sh_ref[b]

        def body(b, r0):
            h = jnp.concatenate([half(b, r0), half(b, r0 + SUBLANES)], axis=0)
            h_s[pl.ds(b * tt + r0, rc), :] = h.astype(BF16)
            o_ref[b, pl.ds(r0, rc), :] = jnp.zeros((rc, D), F32)
        row_loop(body, rc)

    def hidden_tiles(count):
        hb = h_s[...]
        cw = D // n_col_chunks
        acts = []
        for k in range(count):
            gate = _dot(hb, w_refs[3 * k][...])
            up = _dot(hb, w_refs[3 * k + 1][...])
            acts.append((gate * _sigmoid(gate) * up).astype(BF16))
        for c in range(n_col_chunks):
            cols = slice(c * cw, (c + 1) * cw)
            down = _dot(acts[0], w_refs[2][:, cols])
            for k in range(1, count):
                down = down + _dot(acts[k], w_refs[3 * k + 2][:, cols])
            o_ref[:, :, cols] += down.reshape(bb, tt, cw)

    tail = n_tiles - (pl.cdiv(n_tiles, tiles_per_step) - 1) * tiles_per_step
    if tail == tiles_per_step:
        hidden_tiles(tiles_per_step)
    else:
        @pl.when(j < last)
        def _():
            hidden_tiles(tiles_per_step)

        @pl.when(j == last)
        def _():
            hidden_tiles(tail)

    @pl.when(j == last)
    def _():
        def body(b, r0):
            rows = pl.ds(r0, SUBLANES)
            x2 = x_ref[b, rows, :] + g2_ref[b] * o_ref[b, rows, :]
            if final_norm:
                x2 = x2 * lax.rsqrt(jnp.mean(x2 * x2, axis=-1, keepdims=True) + EPS) * gf_ref[0]
            o_ref[b, rows, :] = x2
        row_loop(body, SUBLANES)


def _ffn(x, ng, sc, sh, g2, w_g, w_u, w_down, gf, final_norm):
    B, T, D = x.shape
    F = w_down.shape[0]
    tf = _pick(F, 256, LANES)
    nf = F // tf
    u_off = nf if w_u.shape[1] == 2 * F else 0
    bb, tt = _row_tiling(B, T, 512)
    tps = 2
    w_specs, w_args = [], []
    for k in range(tps):
        tile = lambda j, k=k: jnp.minimum(j * tps + k, nf - 1)
        w_specs += [pl.BlockSpec((D, tf), lambda b, t, j, tile=tile: (0, tile(j))),
                    pl.BlockSpec((D, tf), lambda b, t, j, tile=tile: (0, u_off + tile(j))),
                    pl.BlockSpec((tf, D), lambda b, t, j, tile=tile: (tile(j), 0))]
        w_args += [w_g, w_u, w_down]
    return pl.pallas_call(
        functools.partial(_ffn_kernel, n_col_chunks=max(1, D // 1024), final_norm=final_norm,
                          tiles_per_step=tps, n_tiles=nf),
        grid=(B // bb, T // tt, pl.cdiv(nf, tps)),
        in_specs=[pl.BlockSpec((bb, tt, D), lambda b, t, j: (b, t, 0), pipeline_mode=pl.Buffered(1)),
                  pl.BlockSpec((1, 1, D), lambda b, t, j: (0, 0, 0)),
                  pl.BlockSpec((bb, 1, D), lambda b, t, j: (b, 0, 0)),
                  pl.BlockSpec((bb, 1, D), lambda b, t, j: (b, 0, 0)),
                  pl.BlockSpec((bb, 1, D), lambda b, t, j: (b, 0, 0))] + w_specs + [
                  pl.BlockSpec((1, 1, D), lambda b, t, j: (0, 0, 0))],
        out_specs=pl.BlockSpec((bb, tt, D), lambda b, t, j: (b, t, 0)),
        out_shape=jax.ShapeDtypeStruct((B, T, D), F32),
        scratch_shapes=[pltpu.VMEM((bb * tt, D), BF16)],
        compiler_params=_cparams("arbitrary", "arbitrary", "arbitrary"),
        name="ffn",
    )(x, ng, sc, sh, g2, *w_args, gf)


def _mixer(x, ada, state, p, bf16_weights, cast_jobs):
    sh1, sc1, g1 = ada
    c0, n0, m0, hl0, cv0 = state
    B, T, D = x.shape
    _, H, dk, dv = c0.shape
    W = cv0.shape[2]
    n_qk, n_v = 2 * H * dk, H * dv
    assert n_qk == n_v and (3 * n_v) % LANES == 0
    h1, gates = _normmod_gates(x, p["norm1_g"], sc1, sh1, p["w_in"], 3 * n_v // LANES, p["b_gates"], H)
    h1 = h1.reshape(B * T, D)
    jobs = list(cast_jobs.items())

    def proj(w, blk, n, dt, ep, cs=None):
        name, job = jobs.pop() if jobs else (None, None)
        out = _inproj(h1, w, blk, n, dt, ep, cs, job)
        if job is not None:
            out, bf16_weights[name] = out
        return out.reshape(B, T, n)

    qk = proj(p["w_in"], 0, n_qk, BF16, "colscale", p["qk_scale"])
    v = proj(p["w_in"], 1, n_v, BF16, "none")
    og = proj(p["w_in"], 2, n_v, F32, "sigmoid")
    xb = proj(p["w_xg"], 0, W, F32, "none")
    gg = proj(p["w_xg"], 1, W, F32, "gelu")
    assert not jobs
    ya, c1, n1, m1 = _mlstm(qk, v, og, gates, c0, n0, m0, p["head_norm_g"])
    yb, hl1, cv1 = _rglru(xb, gg, cv0, hl0, p["conv_w"], p["conv_b"],
                          p["w_r"], p["w_i"], p["b_r"], p["b_i"], p["lru_lambda"])
    return _outproj(ya, yb, bf16_weights["w_out"], x, g1), (c1, n1, m1, hl1, cv1)


def kernel(x_prompt, x_sample, c_prompt, c_sample, state_mlstm_C, state_mlstm_n, state_mlstm_m, state_lru_h, state_conv, w_ada, b_ada, norm1_g, norm2_g, w_in, b_gates_a, head_norm_g, conv_w, conv_b, w_r, b_r, w_i, b_i, lru_lambda, w_out, w_gu, w_down, normf_g):
    depth = w_in.shape[0]
    Bp, Tp, D = x_prompt.shape
    Bs = x_sample.shape[0]
    H, dk, dv = state_mlstm_C.shape[2:]
    W = conv_w.shape[2]
    taps = conv_w.shape[1]
    c_x = 2 * H * dk + 2 * H * dv + 2 * H

    xp, xs = x_prompt, x_sample
    p_states, s_states = [], []
    R = Bp + Bs
    Rpad = -(-R // SUBLANES) * SUBLANES
    c_all = jnp.concatenate([c_prompt, c_sample, jnp.zeros((Rpad - R, D), F32)], axis=0)
    gf = normf_g.reshape(1, 1, D)
    qk_scale = jnp.concatenate([jnp.ones((1, H * dk), F32), jnp.full((1, H * dk), dk ** -0.5, F32)], axis=1)
    for l in range(depth):
        w_in_l = w_in[l]
        p = {
            "norm1_g": norm1_g[l].reshape(1, 1, D),
            "w_in": w_in_l.astype(BF16),
            "w_xg": w_in_l[:, c_x:].astype(BF16),
            "qk_scale": qk_scale,
            "b_gates": jnp.pad(b_gates_a[l].reshape(1, 2 * H), ((0, 0), (0, LANES - 2 * H))),
            "head_norm_g": head_norm_g[l],
            "conv_w": conv_w[l], "conv_b": conv_b[l],
            "w_r": w_r[l], "w_i": w_i[l], "b_r": b_r[l], "b_i": b_i[l],
            "lru_lambda": lru_lambda[l],
        }
        steps = (Bp * Tp) // _inproj_tile(Bp * Tp)
        planned = {"w_g": _cast_job(w_gu[l], steps, 2, 0), "w_u": _cast_job(w_gu[l], steps, 2, 1),
                   "w_down": _cast_job(w_down[l], steps), "w_out": _cast_job(w_out[l], steps)}
        cast_jobs = {k: j for k, j in planned.items() if j is not None}
        bf16_weights = {}
        if "w_g" not in cast_jobs or "w_u" not in cast_jobs:
            cast_jobs.pop("w_g", None), cast_jobs.pop("w_u", None)
            bf16_weights["w_g"] = bf16_weights["w_u"] = w_gu[l].astype(BF16)
        if "w_down" not in cast_jobs:
            bf16_weights["w_down"] = w_down[l].astype(BF16)
        if "w_out" not in cast_jobs:
            bf16_weights["w_out"] = w_out[l].astype(BF16)
        norm2 = norm2_g[l].reshape(1, 1, D)
        ada = _ada(c_all, w_ada[l], b_ada[l].reshape(1, -1))
        ada_p = [ada[:Bp, i * D:(i + 1) * D].reshape(Bp, 1, D) for i in range(6)]
        ada_s = [ada[Bp:R, i * D:(i + 1) * D].reshape(Bs, 1, D) for i in range(6)]

        zero_state = (jnp.zeros((Bp, H, dk, dv), F32), jnp.zeros((Bp, H, dk), F32),
                      jnp.zeros((Bp, H), F32), jnp.zeros((Bp, W), F32),
                      jnp.zeros((Bp, taps - 1, W), F32))
        cache_state = (state_mlstm_C[l], state_mlstm_n[l], state_mlstm_m[l], state_lru_h[l],
                       state_conv[l])
        last = l == depth - 1
        new = []
        for x, (sh1, sc1, g1, sh2, sc2, g2), st in ((xp, ada_p, zero_state), (xs, ada_s, cache_state)):
            x1, st1 = _mixer(x, (sh1, sc1, g1), st, p, bf16_weights, cast_jobs)
            cast_jobs = {}
            y = _ffn(x1, norm2, sc2, sh2, g2, bf16_weights["w_g"], bf16_weights["w_u"],
                     bf16_weights["w_down"], gf, final_norm=last)
            new.append((y, st1))
        (xp, st_p), (xs, st_s) = new
        p_states.append(st_p)
        s_states.append(st_s)

    stack = lambda sts, i: jnp.stack([s[i] for s in sts])
    return (xp, xs) + tuple(stack(p_states, i) for i in range(5)) + tuple(stack(s_states, i) for i in range(5))
```

```python
import functools

import jax
import jax.numpy as jnp
from jax import lax
from jax.experimental import pallas as pl
from jax.experimental.pallas import tpu as pltpu

F32 = jnp.float32
BF16 = jnp.bfloat16
EPS = 1e-6
LRU_C = 8.0
LANES = 128
SUBLANES = 8
VMEM_LIMIT_BYTES = 60 * 1024 * 1024


def _cparams(*sem):
    return pltpu.CompilerParams(dimension_semantics=sem, vmem_limit_bytes=VMEM_LIMIT_BYTES)


def _pick(total, target, quantum):
    if total <= target:
        return total
    best = None
    for cand in range(quantum, target + 1, quantum):
        if total % cand == 0:
            best = cand
    assert best is not None, (total, target, quantum)
    return best


def _row_tiling(B, T, target_rows):
    if T >= target_rows:
        return 1, _pick(T, target_rows, 16)
    bb = _pick(B, max(1, target_rows // T), 1)
    return bb, T


def _mlstm_chunk(T):
    return _pick(T, LANES, 16)


def _log_sigmoid(x):
    return jnp.minimum(x, 0.0) - jnp.log1p(jnp.exp(-jnp.abs(x)))


def _sigmoid(x):
    return 0.5 * (1.0 + jnp.tanh(0.5 * x))


def _gelu_tanh(x):
    c = 0.7978845608028654
    return 0.5 * x * (1.0 + jnp.tanh(c * (x + 0.044715 * (x * x * x))))


def _dot(a, b):
    return jnp.dot(a, b, preferred_element_type=F32)


def _dot_nt(a, b):
    return lax.dot_general(a, b, (((1,), (1,)), ((), ())), preferred_element_type=F32)


def _dot_tn(a, b):
    return lax.dot_general(a, b, (((0,), (0,)), ((), ())), preferred_element_type=F32)


def _split_bf16(x, terms):
    out = []
    for _ in range(terms - 1):
        t = x.astype(BF16)
        out.append(t)
        x = x - t.astype(F32)
    out.append(x.astype(BF16))
    return out


def _ada_kernel(c_ref, w_ref, b_ref, o_ref):
    c = c_ref[...]
    s = (c * _sigmoid(c)).astype(BF16)
    o_ref[...] = _dot(s, w_ref[...].astype(BF16)) + b_ref[...]


def _ada(c, w, b):
    R, D = c.shape
    N = w.shape[1]
    tn = _pick(N, 512, LANES)
    return pl.pallas_call(
        _ada_kernel,
        grid=(N // tn,),
        in_specs=[pl.BlockSpec((R, D), lambda j: (0, 0)),
                  pl.BlockSpec((D, tn), lambda j: (0, j)),
                  pl.BlockSpec((1, tn), lambda j: (0, j))],
        out_specs=pl.BlockSpec((R, tn), lambda j: (0, j)),
        out_shape=jax.ShapeDtypeStruct((R, N), F32),
        compiler_params=_cparams("arbitrary"),
        name="ada",
    )(c, w, b)


def _normmod(x, geff, sh):
    r = lax.rsqrt(jnp.mean(x * x, axis=-1, keepdims=True) + EPS)
    return x * r * geff + sh


def _normmod_gates_kernel(x_ref, g_ref, sc_ref, sh_ref, wg_ref, bg_ref, h_ref, gr_ref, gt_s, *, n_heads, chunk):
    bb, tt, D = x_ref.shape
    H = n_heads
    hb = _normmod(x_ref[...], g_ref[...] * (1.0 + sc_ref[...]), sh_ref[...]).astype(BF16)
    h_ref[...] = hb
    pre = _dot(hb.reshape(bb * tt, D), wg_ref[...]) + bg_ref[...]
    lane = lax.broadcasted_iota(jnp.int32, pre.shape, 1)
    gates = jnp.where(lane < H, pre, _log_sigmoid(pre))
    rows, lanes_out = gt_s.shape[2], gt_s.shape[3]
    if lanes_out > chunk:
        gt_s[...] = jnp.zeros_like(gt_s)
    blocks = [(i, c) for i in range(bb) for c in range(tt // chunk)]
    for i, c in blocks:
        r0 = i * tt + c * chunk
        gt_s[i, c, :, 0:chunk] = gates[r0:r0 + chunk, :].T[:rows, :]
    ig = jnp.concatenate([gt_s[i, c, 0:H, :] for i, c in blocks], axis=0)
    lf = jnp.concatenate([gt_s[i, c, H:2 * H, :] for i, c in blocks], axis=0)
    lane_t = lax.broadcasted_iota(jnp.int32, lf.shape, 1)
    bcum = _lane_scan(lf, jnp.add, lane_t, 0.0)
    a = ig - bcum
    cm = _lane_scan(a, jnp.maximum, lane_t, -jnp.inf)
    if gr_ref.shape[2] > 3 * H:
        gr_ref[...] = jnp.zeros_like(gr_ref)
    for k, (i, c) in enumerate(blocks):
        for r, val in enumerate((a, cm, bcum)):
            gr_ref[i, c, r * H:(r + 1) * H, :] = val[k * H:(k + 1) * H, :]


def _normmod_gates(x, g, sc, sh, wg, gate_blk, bg, n_heads):
    B, T, D = x.shape
    L = _mlstm_chunk(T)
    bb, tt = _row_tiling(B, T, 512)
    assert tt % L == 0
    GR = -(-2 * n_heads // SUBLANES) * SUBLANES
    GR3 = -(-3 * n_heads // SUBLANES) * SUBLANES
    LP = max(L, LANES)
    return pl.pallas_call(
        functools.partial(_normmod_gates_kernel, n_heads=n_heads, chunk=L),
        grid=(B // bb, T // tt),
        in_specs=[pl.BlockSpec((bb, tt, D), lambda b, t: (b, t, 0)),
                  pl.BlockSpec((1, 1, D), lambda b, t: (0, 0, 0)),
                  pl.BlockSpec((bb, 1, D), lambda b, t: (b, 0, 0)),
                  pl.BlockSpec((bb, 1, D), lambda b, t: (b, 0, 0)),
                  pl.BlockSpec((D, LANES), lambda b, t: (0, gate_blk)),
                  pl.BlockSpec((1, LANES), lambda b, t: (0, 0))],
        out_specs=[pl.BlockSpec((bb, tt, D), lambda b, t: (b, t, 0)),
                   pl.BlockSpec((bb, tt // L, GR3, LP), lambda b, t: (b, t, 0, 0))],
        out_shape=[jax.ShapeDtypeStruct((B, T, D), BF16),
                   jax.ShapeDtypeStruct((B, T // L, GR3, LP), F32)],
        scratch_shapes=[pltpu.VMEM((bb, tt // L, GR, LP), F32)],
        compiler_params=_cparams("arbitrary", "arbitrary"),
        name="normmod_gates",
    )(x, g, sc, sh, wg, bg)


def _wcast_kernel(a_ref, b_ref, oa_ref, ob_ref, *, off):
    oa_ref[...] = a_ref[...].astype(BF16)
    ob_ref[...] = b_ref[...][:, off:off + ob_ref.shape[1]].astype(BF16)


def _cast_w_in(w, c_gates, n_gates, n_tail):
    D, C = w.shape
    assert c_gates % LANES == 0 and n_gates <= LANES and C == c_gates + n_gates + n_tail
    if c_gates < n_gates + n_tail:
        return w[:, :c_gates + LANES].astype(BF16), w[:, c_gates + n_gates:].astype(BF16)
    wm = c_gates + LANES
    rt = _pick(D, 128, 2 * SUBLANES)
    return pl.pallas_call(
        functools.partial(_wcast_kernel, off=n_gates),
        grid=(D // rt,),
        in_specs=[pl.BlockSpec((rt, wm), lambda i: (i, 0)),
                  pl.BlockSpec((rt, c_gates), lambda i: (i, 1))],
        out_specs=[pl.BlockSpec((rt, wm), lambda i: (i, 0)),
                   pl.BlockSpec((rt, n_tail), lambda i: (i, 0))],
        out_shape=[jax.ShapeDtypeStruct((D, wm), BF16),
                   jax.ShapeDtypeStruct((D, n_tail), BF16)],
        compiler_params=_cparams("arbitrary"),
        name="wcast",
    )(w, w)


def _inproj_kernel(a_ref, w_ref, *rest, epilogue, with_cast):
    rest = list(rest)
    if with_cast:
        cast_out = rest.pop()
        o_ref = rest.pop()
        cast_out[...] = rest.pop()[...].astype(BF16)
    else:
        o_ref = rest.pop()
    acc = _dot(a_ref[...], w_ref[...])
    if epilogue == "colscale":
        acc = acc * rest[0][...]
    elif epilogue == "sigmoid":
        acc = _sigmoid(acc)
    elif epilogue == "gelu":
        acc = _gelu_tanh(acc)
    o_ref[...] = acc.astype(o_ref.dtype)


def _inproj_tile(M):
    return _pick(M, 512, 16)


def _cast_job(w, steps, col_blocks=1, col_block=0):
    R, C = w.shape
    Cb = C // col_blocks
    if C % col_blocks or (col_blocks > 1 and Cb % LANES):
        return None
    if R % (steps * 2 * SUBLANES) == 0:
        return w, (R // steps, Cb), (lambda i: (i, col_block)), (lambda i: (i, 0)), (R, Cb)
    if col_blocks == 1 and C % (steps * LANES) == 0:
        return w, (R, C // steps), (lambda i: (0, i)), (lambda i: (0, i)), (R, C)
    return None


def _inproj(a, w, col_block, N, out_dtype, epilogue, colscale=None, cast=None):
    M, K = a.shape
    tm = _inproj_tile(M)
    in_specs = [pl.BlockSpec((tm, K), lambda i: (i, 0)),
                pl.BlockSpec((K, N), lambda i: (0, col_block), pipeline_mode=pl.Buffered(1))]
    args = [a, w]
    if epilogue == "colscale":
        in_specs.append(pl.BlockSpec((1, N), lambda i: (0, 0)))
        args.append(colscale)
    out_specs = [pl.BlockSpec((tm, N), lambda i: (i, 0))]
    out_shape = [jax.ShapeDtypeStruct((M, N), out_dtype)]
    if cast is not None:
        src, blk, src_map, dst_map, shape = cast
        in_specs.append(pl.BlockSpec(blk, src_map))
        args.append(src)
        out_specs.append(pl.BlockSpec(blk, dst_map))
        out_shape.append(jax.ShapeDtypeStruct(shape, BF16))
    outs = pl.pallas_call(
        functools.partial(_inproj_kernel, epilogue=epilogue, with_cast=cast is not None),
        grid=(M // tm,),
        in_specs=in_specs,
        out_specs=out_specs,
        out_shape=out_shape,
        compiler_params=_cparams("arbitrary"),
        name="inproj_" + epilogue,
    )(*args)
    return outs if cast is not None else outs[0]


def _lane_scan(x, op, lane, identity):
    d = 1
    while d < x.shape[1]:
        x = op(x, jnp.where(lane >= d, pltpu.roll(x, d, axis=1), identity))
        d *= 2
    return x


def _mlstm_kernel(qk_ref, v_ref, og_ref, gr_ref, c0_ref, n0_ref, m0_ref, hg_ref,
                  y_ref, c_ref, n_ref, m_ref, cn_s, m_s, *, chunk, n_heads, dk, dv, head_group):
    tb = pl.program_id(1)
    L, H = chunk, n_heads
    Tb = qk_ref.shape[1]
    nc = Tb // L
    LP = gr_ref.shape[3]
    assert dv % LANES == 0 and L <= LANES
    rep = lambda x, n: jnp.concatenate([x] * n, axis=1)

    eye_k = (lax.broadcasted_iota(jnp.int32, (dk, dk), 0) == lax.broadcasted_iota(jnp.int32, (dk, dk), 1))

    @pl.when(tb == 0)
    def _():
        for h in range(H):
            cn_s[h, :, 0:dv] = c0_ref[0, h]
            n_col = jnp.sum(jnp.where(eye_k, n0_ref[0, h:h + 1, :], 0.0), axis=1, keepdims=True)
            cn_s[h, :, dv:dv + LANES] = jnp.broadcast_to(n_col, (dk, LANES))
        m_s[...] = jnp.broadcast_to(m0_ref[0], m_s.shape)

    row = lax.broadcasted_iota(jnp.int32, (L, L), 0)
    col = lax.broadcasted_iota(jnp.int32, (L, L), 1)
    tril = col <= row
    nr = 3 * H
    spread = ((lax.broadcasted_iota(jnp.int32, (3 * nr, nr * LANES), 1) // LANES)
              == (lax.broadcasted_iota(jnp.int32, (3 * nr, nr * LANES), 0) % nr)).astype(BF16)
    ones_2l = jnp.ones((2 * L, LANES), BF16)
    mean_2v = jnp.full((2 * dv, LANES), 1.0 / dv, BF16)
    assert dv & (dv - 1) == 0

    def chunk_step(c, carry):
        sl = pl.ds(pl.multiple_of(c * L, L), L)
        rows = gr_ref[0, tb * nc + c][0:nr]
        a = rows[0:H]
        cols = _dot_tn(jnp.concatenate(_split_bf16(rows, 3), axis=0), spread)
        blk = lambda r: cols[0:L, r * LANES:(r + 1) * LANES]
        m_all = m_s[...]
        nv = dv // LANES
        m_new = {}
        for h0 in range(0, H, head_group):
            heads = range(h0, min(h0 + head_group, H))
            qb = {h: qk_ref[0, sl, h * dk:(h + 1) * dk] for h in heads}
            kb = {h: qk_ref[0, sl, (H + h) * dk:(H + h + 1) * dk] for h in heads}
            vb = {h: v_ref[0, sl, h * dv:(h + 1) * dv] for h in heads}
            cn = {h: cn_s[h] for h in heads}
            qk = {h: _dot_nt(qb[h], kb[h]) for h in heads}
            qc = {h: _dot(qb[h], cn[h].astype(BF16)) for h in heads}

            g, s_inter, sv, rs = {}, {}, {}, {}
            for h in heads:
                g[h] = jnp.maximum(m_all[h:h + 1, :], blk(H + h))
                s_inter[h] = jnp.exp(m_all[h:h + 1, :] - g[h])
                w = jnp.where(tril, jnp.exp(jnp.minimum(a[h:h + 1, 0:L] - g[h][:, 0:L], 0.0)), 0.0)
                s2 = _split_bf16(qk[h] * w, 2)
                sv[h] = _dot(s2[0], vb[h])
                rs[h] = _dot(jnp.concatenate(s2, axis=1), ones_2l)

            hh, ms = {}, {}
            for h in heads:
                num = rep(s_inter[h], nv) * qc[h][:, 0:dv] + sv[h]
                den = s_inter[h] * qc[h][:, dv:dv + LANES] + rs[h]
                inv = 1.0 / jnp.maximum(jnp.abs(den), jnp.exp(-(blk(2 * H + h) + g[h])))
                hh[h] = num * rep(inv, nv)
                ms[h] = _dot(jnp.concatenate(_split_bf16(hh[h] * hh[h], 2), axis=1), mean_2v)

            for h in heads:
                hn = hh[h] * rep(lax.rsqrt(ms[h] + EPS), nv) * hg_ref[h:h + 1, :]
                y_ref[0, sl, h * dv:(h + 1) * dv] = (hn * og_ref[0, sl, h * dv:(h + 1) * dv]).astype(y_ref.dtype)
                g_last = g[h][L - 1:L, :]
                w_last = jnp.exp(blk(h) - g_last)
                upd_rhs = jnp.concatenate([rep(w_last, nv) * vb[h].astype(F32), w_last], axis=1)
                decay = jnp.exp(m_all[h:h + 1, :] - g_last)
                cn_s[h] = rep(decay, nv + 1) * cn[h] + _dot_tn(kb[h], upd_rhs.astype(BF16))
                m_new[h] = blk(2 * H + h)[L - 1:L, :] + g_last
        m_s[...] = jnp.concatenate([m_new[h] for h in range(H)], axis=0)
        return carry

    lax.fori_loop(0, nc, chunk_step, 0)

    @pl.when(tb == pl.num_programs(1) - 1)
    def _():
        for h in range(H):
            c_ref[0, h] = cn_s[h, :, 0:dv]
            n_ref[0, h:h + 1, :] = jnp.sum(jnp.where(eye_k, cn_s[h, :, dv:dv + 1], 0.0), axis=0, keepdims=True)
        m_ref[0] = m_s[...]


def _mlstm(qk, v, og, gates, c0, n0, m0, head_g):
    B, T, _ = qk.shape
    _, H, dk, dv = c0.shape
    L = _mlstm_chunk(T)
    Tb = _pick(T, 512, L)
    _, NC, GR, LP = gates.shape
    y, c1, n1, m1 = pl.pallas_call(
        functools.partial(_mlstm_kernel, chunk=L, n_heads=H, dk=dk, dv=dv, head_group=4),
        grid=(B, T // Tb),
        in_specs=[pl.BlockSpec((1, Tb, 2 * H * dk), lambda b, t: (b, t, 0)),
                  pl.BlockSpec((1, Tb, H * dv), lambda b, t: (b, t, 0)),
                  pl.BlockSpec((1, Tb, H * dv), lambda b, t: (b, t, 0)),
                  pl.BlockSpec((1, NC, GR, LP), lambda b, t: (b, 0, 0, 0)),
                  pl.BlockSpec((1, H, dk, dv), lambda b, t: (b, 0, 0, 0)),
                  pl.BlockSpec((1, H, dk), lambda b, t: (b, 0, 0)),
                  pl.BlockSpec((1, H, 1), lambda b, t: (b, 0, 0)),
                  pl.BlockSpec((H, dv), lambda b, t: (0, 0))],
        out_specs=[pl.BlockSpec((1, Tb, H * dv), lambda b, t: (b, t, 0)),
                   pl.BlockSpec((1, H, dk, dv), lambda b, t: (b, 0, 0, 0)),
                   pl.BlockSpec((1, H, dk), lambda b, t: (b, 0, 0)),
                   pl.BlockSpec((1, H, LANES), lambda b, t: (b, 0, 0))],
        out_shape=[jax.ShapeDtypeStruct((B, T, H * dv), BF16),
                   jax.ShapeDtypeStruct((B, H, dk, dv), F32),
                   jax.ShapeDtypeStruct((B, H, dk), F32),
                   jax.ShapeDtypeStruct((B, H, LANES), F32)],
        scratch_shapes=[pltpu.VMEM((H, dk, dv + LANES), F32),
                        pltpu.VMEM((H, LANES), F32)],
        compiler_params=_cparams("arbitrary", "arbitrary"),
        name="mlstm",
    )(qk, v, og, gates, c0, n0, m0.reshape(B, H, 1), head_g)
    return y, c1, n1, m1[:, :, 0]


def _scan_pitch(rows):
    quads = -(-(-(-rows // SUBLANES)) // 4)
    return 4 * (quads + 1 - quads % 2)


def _rglru_kernel(xb_ref, gg_ref, cv0_ref, h0_ref, cw_ref, cb_ref, wr_ref, wi_ref, br_ref, bi_ref,
                  lam_ref, y_ref, h_ref, cv_ref, xext, a_s, u_s, *, pitch, group):
    tb = pl.program_id(1)
    Tb, W = xb_ref.shape[1], xb_ref.shape[2]
    nb, bw = wr_ref.shape[0], wr_ref.shape[1]
    taps = cw_ref.shape[0]
    rows = a_s.shape[1]
    pad = SUBLANES

    @pl.when(tb == 0)
    def _():
        h_ref[...] = h0_ref[...]
        cv_ref[...] = cv0_ref[...]

    x = xb_ref[0]
    xext[pl.ds(pad, Tb), :] = x
    xext[pl.ds(pad - (taps - 1), taps - 1), :] = cv_ref[0]
    xc = cb_ref[...] + cw_ref[taps - 1:taps, :] * x
    for j in range(taps - 1):
        xc = xc + cw_ref[j:j + 1, :] * xext[pl.ds(pad - (taps - 1) + j, Tb), :]
    cv_ref[0] = xext[pl.ds(pad + Tb - (taps - 1), taps - 1), :]

    log_sig_lam = _log_sigmoid(lam_ref[...])
    for n in range(nb):
        cols = slice(n * bw, (n + 1) * bw)
        xn = xc[:, cols]
        xnb = xn.astype(BF16)
        r = _sigmoid(_dot(xnb, wr_ref[n]) + br_ref[:, cols])
        i = _sigmoid(_dot(xnb, wi_ref[n]) + bi_ref[:, cols])
        log_a = LRU_C * r * log_sig_lam[:, cols]
        a = jnp.exp(log_a)
        a_s[n, 0:Tb, :] = a
        z = -jnp.tanh(log_a) * (1.0 + a * a)
        u_s[n, 0:Tb, :] = jnp.where(z > 0.0, z * lax.rsqrt(z), 0.0) * (i * xn)
        if rows > Tb:
            a_s[n, Tb:rows, :] = jnp.ones((rows - Tb, bw), F32)
            u_s[n, Tb:rows, :] = jnp.zeros((rows - Tb, bw), F32)

    for n0 in range(0, nb, group):
        slabs = list(range(n0, min(n0 + group, nb)))

        def pass1(i, carry):
            idx = pl.ds(i, SUBLANES, stride=pitch)
            out = []
            for (hz, ac), n in zip(carry, slabs):
                av = a_s[n, idx, :]
                hz = av * hz + u_s[n, idx, :]
                ac = av * ac
                u_s[n, idx, :] = hz
                a_s[n, idx, :] = ac
                out.append((hz, ac))
            return tuple(out)

        init = tuple((jnp.zeros((SUBLANES, bw), F32), jnp.ones((SUBLANES, bw), F32)) for _ in slabs)
        ends = lax.fori_loop(0, pitch, pass1, init, unroll=2)

        starts = []
        for (hz, ac), n in zip(ends, slabs):
            cols = slice(n * bw, (n + 1) * bw)
            h = h_ref[0, :, cols]
            per_block = []
            for j in range(SUBLANES):
                per_block.append(h)
                h = hz[j:j + 1, :] + ac[j:j + 1, :] * h
            h_ref[0, :, cols] = h
            starts.append(jnp.concatenate(per_block, axis=0))

        def pass2(i, carry):
            idx = pl.ds(i, SUBLANES, stride=pitch)
            for st, n in zip(starts, slabs):
                u_s[n, idx, :] = u_s[n, idx, :] + a_s[n, idx, :] * st
            return carry

        lax.fori_loop(0, pitch, pass2, 0, unroll=2)

    for n in range(nb):
        cols = slice(n * bw, (n + 1) * bw)
        y_ref[0, :, cols] = (u_s[n, 0:Tb, :] * gg_ref[0, :, cols]).astype(y_ref.dtype)


def _rglru(xb, gg, cv0, h0, conv_w, conv_b, w_r, w_i, b_r, b_i, lam):
    B, T, W = xb.shape
    taps = conv_w.shape[0]
    nb, bw = w_r.shape[0], w_r.shape[1]
    assert T >= taps - 1 and bw == LANES and nb * bw == W
    Tb = _pick(T, 256, SUBLANES)
    pitch = _scan_pitch(Tb)
    row = lambda a: a.reshape(1, W)
    y, h1, cv1 = pl.pallas_call(
        functools.partial(_rglru_kernel, pitch=pitch, group=8),
        grid=(B, T // Tb),
        in_specs=[pl.BlockSpec((1, Tb, W), lambda b, t: (b, t, 0)),
                  pl.BlockSpec((1, Tb, W), lambda b, t: (b, t, 0)),
                  pl.BlockSpec((1, taps - 1, W), lambda b, t: (b, 0, 0)),
                  pl.BlockSpec((1, 1, W), lambda b, t: (b, 0, 0)),
                  pl.BlockSpec((taps, W), lambda b, t: (0, 0)),
                  pl.BlockSpec((1, W), lambda b, t: (0, 0)),
                  pl.BlockSpec((nb, bw, bw), lambda b, t: (0, 0, 0)),
                  pl.BlockSpec((nb, bw, bw), lambda b, t: (0, 0, 0)),
                  pl.BlockSpec((1, W), lambda b, t: (0, 0)),
                  pl.BlockSpec((1, W), lambda b, t: (0, 0)),
                  pl.BlockSpec((1, W), lambda b, t: (0, 0))],
        out_specs=[pl.BlockSpec((1, Tb, W), lambda b, t: (b, t, 0)),
                   pl.BlockSpec((1, 1, W), lambda b, t: (b, 0, 0)),
                   pl.BlockSpec((1, taps - 1, W), lambda b, t: (b, 0, 0))],
        out_shape=[jax.ShapeDtypeStruct((B, T, W), BF16),
                   jax.ShapeDtypeStruct((B, 1, W), F32),
                   jax.ShapeDtypeStruct((B, taps - 1, W), F32)],
        scratch_shapes=[pltpu.VMEM((Tb + SUBLANES, W), F32),
                        pltpu.VMEM((nb, SUBLANES * pitch, bw), F32),
                        pltpu.VMEM((nb, SUBLANES * pitch, bw), F32)],
        compiler_params=_cparams("arbitrary", "arbitrary"),
        name="rglru",
    )(xb, gg, cv0, h0.reshape(B, 1, W), conv_w, row(conv_b), w_r.astype(BF16), w_i.astype(BF16),
      row(b_r), row(b_i), row(lam))
    return y, h1.reshape(B, W), cv1


def _outproj_kernel(ya_ref, yb_ref, wt_ref, wb_ref, x_ref, g_ref, o_ref):
    bb, tt, wa = ya_ref.shape
    wb = yb_ref.shape[2]
    mix = (_dot(ya_ref[...].reshape(bb * tt, wa), wt_ref[...])
           + _dot(yb_ref[...].reshape(bb * tt, wb), wb_ref[...]))
    o_ref[...] = x_ref[...] + g_ref[...] * mix.reshape(bb, tt, mix.shape[-1])


def _outproj(ya, yb, w_out, x, g1):
    B, T, D = x.shape
    wa, wb = ya.shape[2], yb.shape[2]
    assert wa == wb
    bb, tt = _row_tiling(B, T, 1024)
    tn = _pick(D, 1024, LANES)
    return pl.pallas_call(
        _outproj_kernel,
        grid=(B // bb, T // tt, D // tn),
        in_specs=[pl.BlockSpec((bb, tt, wa), lambda b, t, j: (b, t, 0)),
                  pl.BlockSpec((bb, tt, wb), lambda b, t, j: (b, t, 0)),
                  pl.BlockSpec((wa, tn), lambda b, t, j: (0, j)),
                  pl.BlockSpec((wb, tn), lambda b, t, j: (1, j)),
                  pl.BlockSpec((bb, tt, tn), lambda b, t, j: (b, t, j)),
                  pl.BlockSpec((bb, 1, tn), lambda b, t, j: (b, 0, j))],
        out_specs=pl.BlockSpec((bb, tt, tn), lambda b, t, j: (b, t, j)),
        out_shape=jax.ShapeDtypeStruct((B, T, D), F32),
        compiler_params=_cparams("arbitrary", "arbitrary", "arbitrary"),
        name="outproj",
    )(ya, yb, w_out, w_out, x, g1)


def _ffn_kernel(x_ref, ng_ref, sc_ref, sh_ref, g2_ref, *rest, n_col_chunks, final_norm, tiles_per_step, n_tiles):
    w_refs, (gf_ref, o_ref, h_s) = rest[:3 * tiles_per_step], rest[3 * tiles_per_step:]
    j = pl.program_id(2)
    last = pl.num_programs(2) - 1
    bb, tt, D = x_ref.shape

    def row_loop(body, rc):
        assert tt % rc == 0
        for b in range(bb):
            for i in range(tt // rc):
                body(b, i * rc)

    @pl.when(j == 0)
    def _():
        rc = 2 * SUBLANES

        def half(b, r0):
            x = x_ref[b, pl.ds(r0, SUBLANES), :]
            r = lax.rsqrt(jnp.mean(x * x, axis=-1, keepdims=True) + EPS)
            return x * r * (ng_ref[0] * (1.0 + sc_ref[b])) + sh_ref[b]

        def body(b, r0):
            h = jnp.concatenate([half(b, r0), half(b, r0 + SUBLANES)], axis=0)
            h_s[pl.ds(b * tt + r0, rc), :] = h.astype(BF16)
            o_ref[b, pl.ds(r0, rc), :] = jnp.zeros((rc, D), F32)
        row_loop(body, rc)

    def hidden_tiles(count):
        hb = h_s[...]
        cw = D // n_col_chunks
        acts = []
        for k in range(count):
            gate = _dot(hb, w_refs[3 * k][...])
            up = _dot(hb, w_refs[3 * k + 1][...])
            acts.append((gate * _sigmoid(gate) * up).astype(BF16))
        for c in range(n_col_chunks):
            cols = slice(c * cw, (c + 1) * cw)
            down = _dot(acts[0], w_refs[2][:, cols])
            for k in range(1, count):
                down = down + _dot(acts[k], w_refs[3 * k + 2][:, cols])
            o_ref[:, :, cols] += down.reshape(bb, tt, cw)

    tail = n_tiles - (pl.cdiv(n_tiles, tiles_per_step) - 1) * tiles_per_step
    if tail == tiles_per_step:
        hidden_tiles(tiles_per_step)
    else:
        @pl.when(j < last)
        def _():
            hidden_tiles(tiles_per_step)

        @pl.when(j == last)
        def _():
            hidden_tiles(tail)

    @pl.when(j == last)
    def _():
        def body(b, r0):
            rows = pl.ds(r0, SUBLANES)
            x2 = x_ref[b, rows, :] + g2_ref[b] * o_ref[b, rows, :]
            if final_norm:
                x2 = x2 * lax.rsqrt(jnp.mean(x2 * x2, axis=-1, keepdims=True) + EPS) * gf_ref[0]
            o_ref[b, rows, :] = x2
        row_loop(body, SUBLANES)


def _ffn(x, ng, sc, sh, g2, w_g, w_u, w_down, gf, final_norm):
    B, T, D = x.shape
    F = w_down.shape[0]
    tf = _pick(F, 256, LANES)
    nf = F // tf
    u_off = nf if w_u.shape[1] == 2 * F else 0
    bb, tt = _row_tiling(B, T, 512)
    tps = 2
    w_specs, w_args = [], []
    for k in range(tps):
        tile = lambda j, k=k: jnp.minimum(j * tps + k, nf - 1)
        w_specs += [pl.BlockSpec((D, tf), lambda b, t, j, tile=tile: (0, tile(j))),
                    pl.BlockSpec((D, tf), lambda b, t, j, tile=tile: (0, u_off + tile(j))),
                    pl.BlockSpec((tf, D), lambda b, t, j, tile=tile: (tile(j), 0))]
        w_args += [w_g, w_u, w_down]
    return pl.pallas_call(
        functools.partial(_ffn_kernel, n_col_chunks=max(1, D // 1024), final_norm=final_norm,
                          tiles_per_step=tps, n_tiles=nf),
        grid=(B // bb, T // tt, pl.cdiv(nf, tps)),
        in_specs=[pl.BlockSpec((bb, tt, D), lambda b, t, j: (b, t, 0), pipeline_mode=pl.Buffered(1)),
                  pl.BlockSpec((1, 1, D), lambda b, t, j: (0, 0, 0)),
                  pl.BlockSpec((bb, 1, D), lambda b, t, j: (b, 0, 0)),
                  pl.BlockSpec((bb, 1, D), lambda b, t, j: (b, 0, 0)),
                  pl.BlockSpec((bb, 1, D), lambda b, t, j: (b, 0, 0))] + w_specs + [
                  pl.BlockSpec((1, 1, D), lambda b, t, j: (0, 0, 0))],
        out_specs=pl.BlockSpec((bb, tt, D), lambda b, t, j: (b, t, 0)),
        out_shape=jax.ShapeDtypeStruct((B, T, D), F32),
        scratch_shapes=[pltpu.VMEM((bb * tt, D), BF16)],
        compiler_params=_cparams("arbitrary", "arbitrary", "arbitrary"),
        name="ffn",
    )(x, ng, sc, sh, g2, *w_args, gf)


def _mixer(x, ada, state, p, bf16_weights, cast_jobs):
    sh1, sc1, g1 = ada
    c0, n0, m0, hl0, cv0 = state
    B, T, D = x.shape
    _, H, dk, dv = c0.shape
    W = cv0.shape[2]
    n_qk, n_v = 2 * H * dk, H * dv
    assert n_qk == n_v and (3 * n_v) % LANES == 0
    h1, gates = _normmod_gates(x, p["norm1_g"], sc1, sh1, p["w_in"], 3 * n_v // LANES, p["b_gates"], H)
    h1 = h1.reshape(B * T, D)
    jobs = list(cast_jobs.items())

    def proj(w, blk, n, dt, ep, cs=None):
        name, job = jobs.pop() if jobs else (None, None)
        out = _inproj(h1, w, blk, n, dt, ep, cs, job)
        if job is not None:
            out, bf16_weights[name] = out
        return out.reshape(B, T, n)

    qk = proj(p["w_in"], 0, n_qk, BF16, "colscale", p["qk_scale"])
    v = proj(p["w_in"], 1, n_v, BF16, "none")
    og = proj(p["w_in"], 2, n_v, F32, "sigmoid")
    xb = proj(p["w_xg"], 0, W, F32, "none")
    gg = proj(p["w_xg"], 1, W, F32, "gelu")
    assert not jobs
    ya, c1, n1, m1 = _mlstm(qk, v, og, gates, c0, n0, m0, p["head_norm_g"])
    yb, hl1, cv1 = _rglru(xb, gg, cv0, hl0, p["conv_w"], p["conv_b"],
                          p["w_r"], p["w_i"], p["b_r"], p["b_i"], p["lru_lambda"])
    return _outproj(ya, yb, bf16_weights["w_out"], x, g1), (c1, n1, m1, hl1, cv1)


def kernel(x_prompt, x_sample, c_prompt, c_sample, state_mlstm_C, state_mlstm_n, state_mlstm_m, state_lru_h, state_conv, w_ada, b_ada, norm1_g, norm2_g, w_in, b_gates_a, head_norm_g, conv_w, conv_b, w_r, b_r, w_i, b_i, lru_lambda, w_out, w_gu, w_down, normf_g):
    depth = w_in.shape[0]
    Bp, Tp, D = x_prompt.shape
    Bs = x_sample.shape[0]
    H, dk, dv = state_mlstm_C.shape[2:]
    W = conv_w.shape[2]
    taps = conv_w.shape[1]
    c_g = 2 * H * dk + 2 * H * dv

    xp, xs = x_prompt, x_sample
    p_states, s_states = [], []
    R = Bp + Bs
    Rpad = -(-R // SUBLANES) * SUBLANES
    c_all = jnp.concatenate([c_prompt, c_sample, jnp.zeros((Rpad - R, D), F32)], axis=0)
    gf = normf_g.reshape(1, 1, D)
    qk_scale = jnp.concatenate([jnp.ones((1, H * dk), F32), jnp.full((1, H * dk), dk ** -0.5, F32)], axis=1)
    for l in range(depth):
        w_main, w_xg = _cast_w_in(w_in[l], c_g, 2 * H, 2 * W)
        p = {
            "norm1_g": norm1_g[l].reshape(1, 1, D),
            "w_in": w_main,
            "w_xg": w_xg,
            "qk_scale": qk_scale,
            "b_gates": jnp.pad(b_gates_a[l].reshape(1, 2 * H), ((0, 0), (0, LANES - 2 * H))),
            "head_norm_g": head_norm_g[l],
            "conv_w": conv_w[l], "conv_b": conv_b[l],
            "w_r": w_r[l], "w_i": w_i[l], "b_r": b_r[l], "b_i": b_i[l],
            "lru_lambda": lru_lambda[l],
        }
        steps = (Bp * Tp) // _inproj_tile(Bp * Tp)
        planned = {"w_g": _cast_job(w_gu[l], steps, 2, 0), "w_u": _cast_job(w_gu[l], steps, 2, 1),
                   "w_down": _cast_job(w_down[l], steps), "w_out": _cast_job(w_out[l], steps)}
        cast_jobs = {k: j for k, j in planned.items() if j is not None}
        bf16_weights = {}
        if "w_g" not in cast_jobs or "w_u" not in cast_jobs:
            cast_jobs.pop("w_g", None), cast_jobs.pop("w_u", None)
            bf16_weights["w_g"] = bf16_weights["w_u"] = w_gu[l].astype(BF16)
        if "w_down" not in cast_jobs:
            bf16_weights["w_down"] = w_down[l].astype(BF16)
        if "w_out" not in cast_jobs:
            bf16_weights["w_out"] = w_out[l].astype(BF16)
        norm2 = norm2_g[l].reshape(1, 1, D)
        ada = _ada(c_all, w_ada[l], b_ada[l].reshape(1, -1))
        ada_p = [ada[:Bp, i * D:(i + 1) * D].reshape(Bp, 1, D) for i in range(6)]
        ada_s = [ada[Bp:R, i * D:(i + 1) * D].reshape(Bs, 1, D) for i in range(6)]

        zero_state = (jnp.zeros((Bp, H, dk, dv), F32), jnp.zeros((Bp, H, dk), F32),
                      jnp.zeros((Bp, H), F32), jnp.zeros((Bp, W), F32),
                      jnp.zeros((Bp, taps - 1, W), F32))
        cache_state = (state_mlstm_C[l], state_mlstm_n[l], state_mlstm_m[l], state_lru_h[l],
                       state_conv[l])
        last = l == depth - 1
        new = []
        for x, (sh1, sc1, g1, sh2, sc2, g2), st in ((xp, ada_p, zero_state), (xs, ada_s, cache_state)):
            x1, st1 = _mixer(x, (sh1, sc1, g1), st, p, bf16_weights, cast_jobs)
            cast_jobs = {}
            y = _ffn(x1, norm2, sc2, sh2, g2, bf16_weights["w_g"], bf16_weights["w_u"],
                     bf16_weights["w_down"], gf, final_norm=last)
            new.append((y, st1))
        (xp, st_p), (xs, st_s) = new
        p_states.append(st_p)
        s_states.append(st_s)

    stack = lambda sts, i: jnp.stack([s[i] for s in sts])
    return (xp, xs) + tuple(stack(p_states, i) for i in range(5)) + tuple(stack(s_states, i) for i in range(5))
```

```python
import functools

import jax
import jax.numpy as jnp
from jax import lax
from jax.experimental import pallas as pl
from jax.experimental.pallas import tpu as pltpu

F32 = jnp.float32
BF16 = jnp.bfloat16
EPS = 1e-6
LRU_C = 8.0
LANES = 128
SUBLANES = 8
VMEM_LIMIT_BYTES = 60 * 1024 * 1024


def _cparams(*sem):
    return pltpu.CompilerParams(dimension_semantics=sem, vmem_limit_bytes=VMEM_LIMIT_BYTES)


def _pick(total, target, quantum):
    if total <= target:
        return total
    best = None
    for cand in range(quantum, target + 1, quantum):
        if total % cand == 0:
            best = cand
    assert best is not None, (total, target, quantum)
    return best


def _row_tiling(B, T, target_rows):
    if T >= target_rows:
        return 1, _pick(T, target_rows, 16)
    bb = _pick(B, max(1, target_rows // T), 1)
    return bb, T


def _mlstm_chunk(T):
    return _pick(T, LANES, 16)


def _log_sigmoid(x):
    return jnp.minimum(x, 0.0) - jnp.log1p(jnp.exp(-jnp.abs(x)))


def _sigmoid(x):
    return 0.5 * (1.0 + jnp.tanh(0.5 * x))


def _gelu_tanh(x):
    c = 0.7978845608028654
    return 0.5 * x * (1.0 + jnp.tanh(c * (x + 0.044715 * (x * x * x))))


def _dot(a, b):
    return jnp.dot(a, b, preferred_element_type=F32)


def _dot_nt(a, b):
    return lax.dot_general(a, b, (((1,), (1,)), ((), ())), preferred_element_type=F32)


def _dot_tn(a, b):
    return lax.dot_general(a, b, (((0,), (0,)), ((), ())), preferred_element_type=F32)


def _split_bf16(x, terms):
    out = []
    for _ in range(terms - 1):
        t = x.astype(BF16)
        out.append(t)
        x = x - t.astype(F32)
    out.append(x.astype(BF16))
    return out


def _ada_kernel(c_ref, w_ref, b_ref, o_ref):
    c = c_ref[...]
    s = (c * _sigmoid(c)).astype(BF16)
    o_ref[...] = _dot(s, w_ref[...].astype(BF16)) + b_ref[...]


def _ada(c, w, b):
    R, D = c.shape
    N = w.shape[1]
    tn = _pick(N, 512, LANES)
    return pl.pallas_call(
        _ada_kernel,
        grid=(N // tn,),
        in_specs=[pl.BlockSpec((R, D), lambda j: (0, 0)),
                  pl.BlockSpec((D, tn), lambda j: (0, j)),
                  pl.BlockSpec((1, tn), lambda j: (0, j))],
        out_specs=pl.BlockSpec((R, tn), lambda j: (0, j)),
        out_shape=jax.ShapeDtypeStruct((R, N), F32),
        compiler_params=_cparams("arbitrary"),
        name="ada",
    )(c, w, b)


def _normmod(x, geff, sh):
    r = lax.rsqrt(jnp.mean(x * x, axis=-1, keepdims=True) + EPS)
    return x * r * geff + sh


def _normmod_gates_kernel(x_ref, g_ref, sc_ref, sh_ref, wg_ref, bg_ref, h_ref, gr_ref, gt_s, *, n_heads, chunk):
    bb, tt, D = x_ref.shape
    H = n_heads
    hb = _normmod(x_ref[...], g_ref[...] * (1.0 + sc_ref[...]), sh_ref[...]).astype(BF16)
    h_ref[...] = hb
    pre = _dot_nt(hb.reshape(bb * tt, D), wg_ref[...]) + bg_ref[...]
    lane = lax.broadcasted_iota(jnp.int32, pre.shape, 1)
    gates = jnp.where(lane < H, pre, _log_sigmoid(pre))
    rows, lanes_out = gt_s.shape[2], gt_s.shape[3]
    if lanes_out > chunk:
        gt_s[...] = jnp.zeros_like(gt_s)
    blocks = [(i, c) for i in range(bb) for c in range(tt // chunk)]
    for i, c in blocks:
        r0 = i * tt + c * chunk
        gt_s[i, c, :, 0:chunk] = gates[r0:r0 + chunk, :].T[:rows, :]
    ig = jnp.concatenate([gt_s[i, c, 0:H, :] for i, c in blocks], axis=0)
    lf = jnp.concatenate([gt_s[i, c, H:2 * H, :] for i, c in blocks], axis=0)
    lane_t = lax.broadcasted_iota(jnp.int32, lf.shape, 1)
    bcum = _lane_scan(lf, jnp.add, lane_t, 0.0)
    a = ig - bcum
    cm = _lane_scan(a, jnp.maximum, lane_t, -jnp.inf)
    if gr_ref.shape[2] > 3 * H:
        gr_ref[...] = jnp.zeros_like(gr_ref)
    for k, (i, c) in enumerate(blocks):
        for r, val in enumerate((a, cm, bcum)):
            gr_ref[i, c, r * H:(r + 1) * H, :] = val[k * H:(k + 1) * H, :]


def _normmod_gates(x, g, sc, sh, wg, gate_blk, bg, n_heads):
    B, T, D = x.shape
    L = _mlstm_chunk(T)
    bb, tt = _row_tiling(B, T, 512)
    assert tt % L == 0
    GR = -(-2 * n_heads // SUBLANES) * SUBLANES
    GR3 = -(-3 * n_heads // SUBLANES) * SUBLANES
    LP = max(L, LANES)
    return pl.pallas_call(
        functools.partial(_normmod_gates_kernel, n_heads=n_heads, chunk=L),
        grid=(B // bb, T // tt),
        in_specs=[pl.BlockSpec((bb, tt, D), lambda b, t: (b, t, 0)),
                  pl.BlockSpec((1, 1, D), lambda b, t: (0, 0, 0)),
                  pl.BlockSpec((bb, 1, D), lambda b, t: (b, 0, 0)),
                  pl.BlockSpec((bb, 1, D), lambda b, t: (b, 0, 0)),
                  pl.BlockSpec((LANES, D), lambda b, t: (gate_blk, 0)),
                  pl.BlockSpec((1, LANES), lambda b, t: (0, 0))],
        out_specs=[pl.BlockSpec((bb, tt, D), lambda b, t: (b, t, 0)),
                   pl.BlockSpec((bb, tt // L, GR3, LP), lambda b, t: (b, t, 0, 0))],
        out_shape=[jax.ShapeDtypeStruct((B, T, D), BF16),
                   jax.ShapeDtypeStruct((B, T // L, GR3, LP), F32)],
        scratch_shapes=[pltpu.VMEM((bb, tt // L, GR, LP), F32)],
        compiler_params=_cparams("arbitrary", "arbitrary"),
        name="normmod_gates",
    )(x, g, sc, sh, wg, bg)


def _inproj_kernel(a_ref, w_ref, *rest, epilogue, with_cast):
    rest = list(rest)
    if with_cast:
        cast_out = rest.pop()
        o_ref = rest.pop()
        cast_out[...] = rest.pop()[...].astype(BF16)
    else:
        o_ref = rest.pop()
    acc = _dot_nt(a_ref[...], w_ref[...])
    if epilogue == "colscale":
        acc = acc * rest[0][...]
    elif epilogue == "sigmoid":
        acc = _sigmoid(acc)
    elif epilogue == "gelu":
        acc = _gelu_tanh(acc)
    o_ref[...] = acc.astype(o_ref.dtype)


def _inproj_tile(M):
    return _pick(M, 512, 16)


def _cast_job(w, steps, col_blocks=1, col_block=0):
    R, C = w.shape
    Cb = C // col_blocks
    if C % col_blocks or (col_blocks > 1 and Cb % LANES):
        return None
    if R % (steps * 2 * SUBLANES) == 0:
        return w, (R // steps, Cb), (lambda i: (i, col_block)), (lambda i: (i, 0)), (R, Cb)
    if col_blocks == 1 and C % (steps * LANES) == 0:
        return w, (R, C // steps), (lambda i: (0, i)), (lambda i: (0, i)), (R, C)
    return None


def _inproj(a, wt, row_block, N, out_dtype, epilogue, colscale=None, cast=None):
    M, K = a.shape
    tm = _inproj_tile(M)
    in_specs = [pl.BlockSpec((tm, K), lambda i: (i, 0)),
                pl.BlockSpec((N, K), lambda i: (row_block, 0), pipeline_mode=pl.Buffered(1))]
    args = [a, wt]
    if epilogue == "colscale":
        in_specs.append(pl.BlockSpec((1, N), lambda i: (0, 0)))
        args.append(colscale)
    out_specs = [pl.BlockSpec((tm, N), lambda i: (i, 0))]
    out_shape = [jax.ShapeDtypeStruct((M, N), out_dtype)]
    if cast is not None:
        src, blk, src_map, dst_map, shape = cast
        in_specs.append(pl.BlockSpec(blk, src_map))
        args.append(src)
        out_specs.append(pl.BlockSpec(blk, dst_map))
        out_shape.append(jax.ShapeDtypeStruct(shape, BF16))
    outs = pl.pallas_call(
        functools.partial(_inproj_kernel, epilogue=epilogue, with_cast=cast is not None),
        grid=(M // tm,),
        in_specs=in_specs,
        out_specs=out_specs,
        out_shape=out_shape,
        compiler_params=_cparams("arbitrary"),
        name="inproj_" + epilogue,
    )(*args)
    return outs if cast is not None else outs[0]


def _lane_scan(x, op, lane, identity):
    d = 1
    while d < x.shape[1]:
        x = op(x, jnp.where(lane >= d, pltpu.roll(x, d, axis=1), identity))
        d *= 2
    return x


def _mlstm_kernel(qk_ref, v_ref, og_ref, gr_ref, c0_ref, n0_ref, m0_ref, hg_ref,
                  y_ref, c_ref, n_ref, m_ref, cn_s, m_s, *, chunk, n_heads, dk, dv, head_group):
    tb = pl.program_id(1)
    L, H = chunk, n_heads
    Tb = qk_ref.shape[1]
    nc = Tb // L
    LP = gr_ref.shape[3]
    assert dv % LANES == 0 and L <= LANES
    rep = lambda x, n: jnp.concatenate([x] * n, axis=1)

    eye_k = (lax.broadcasted_iota(jnp.int32, (dk, dk), 0) == lax.broadcasted_iota(jnp.int32, (dk, dk), 1))

    @pl.when(tb == 0)
    def _():
        for h in range(H):
            cn_s[h, :, 0:dv] = c0_ref[0, h]
            n_col = jnp.sum(jnp.where(eye_k, n0_ref[0, h:h + 1, :], 0.0), axis=1, keepdims=True)
            cn_s[h, :, dv:dv + LANES] = jnp.broadcast_to(n_col, (dk, LANES))
        m_s[...] = jnp.broadcast_to(m0_ref[0], m_s.shape)

    row = lax.broadcasted_iota(jnp.int32, (L, L), 0)
    col = lax.broadcasted_iota(jnp.int32, (L, L), 1)
    tril = col <= row
    nr = 3 * H
    spread = ((lax.broadcasted_iota(jnp.int32, (3 * nr, nr * LANES), 1) // LANES)
              == (lax.broadcasted_iota(jnp.int32, (3 * nr, nr * LANES), 0) % nr)).astype(BF16)
    ones_2l = jnp.ones((2 * L, LANES), BF16)
    mean_2v = jnp.full((2 * dv, LANES), 1.0 / dv, BF16)
    assert dv & (dv - 1) == 0

    def chunk_step(c, carry):
        sl = pl.ds(pl.multiple_of(c * L, L), L)
        rows = gr_ref[0, tb * nc + c][0:nr]
        a = rows[0:H]
        cols = _dot_tn(jnp.concatenate(_split_bf16(rows, 3), axis=0), spread)
        blk = lambda r: cols[0:L, r * LANES:(r + 1) * LANES]
        m_all = m_s[...]
        nv = dv // LANES
        m_new = {}
        for h0 in range(0, H, head_group):
            heads = range(h0, min(h0 + head_group, H))
            qb = {h: qk_ref[0, sl, h * dk:(h + 1) * dk] for h in heads}
            kb = {h: qk_ref[0, sl, (H + h) * dk:(H + h + 1) * dk] for h in heads}
            vb = {h: v_ref[0, sl, h * dv:(h + 1) * dv] for h in heads}
            cn = {h: cn_s[h] for h in heads}
            qk = {h: _dot_nt(qb[h], kb[h]) for h in heads}
            qc = {h: _dot(qb[h], cn[h].astype(BF16)) for h in heads}

            g, s_inter, sv, rs = {}, {}, {}, {}
            for h in heads:
                g[h] = jnp.maximum(m_all[h:h + 1, :], blk(H + h))
                s_inter[h] = jnp.exp(m_all[h:h + 1, :] - g[h])
                w = jnp.where(tril, jnp.exp(jnp.minimum(a[h:h + 1, 0:L] - g[h][:, 0:L], 0.0)), 0.0)
                s2 = _split_bf16(qk[h] * w, 2)
                sv[h] = _dot(s2[0], vb[h])
                rs[h] = _dot(jnp.concatenate(s2, axis=1), ones_2l)

            hh, ms = {}, {}
            for h in heads:
                num = rep(s_inter[h], nv) * qc[h][:, 0:dv] + sv[h]
                den = s_inter[h] * qc[h][:, dv:dv + LANES] + rs[h]
                inv = 1.0 / jnp.maximum(jnp.abs(den), jnp.exp(-(blk(2 * H + h) + g[h])))
                hh[h] = num * rep(inv, nv)
                ms[h] = _dot(jnp.concatenate(_split_bf16(hh[h] * hh[h], 2), axis=1), mean_2v)

            for h in heads:
                hn = hh[h] * rep(lax.rsqrt(ms[h] + EPS), nv) * hg_ref[h:h + 1, :]
                y_ref[0, sl, h * dv:(h + 1) * dv] = (hn * og_ref[0, sl, h * dv:(h + 1) * dv]).astype(y_ref.dtype)
                g_last = g[h][L - 1:L, :]
                w_last = jnp.exp(blk(h) - g_last)
                upd_rhs = jnp.concatenate([rep(w_last, nv) * vb[h].astype(F32), w_last], axis=1)
                decay = jnp.exp(m_all[h:h + 1, :] - g_last)
                cn_s[h] = rep(decay, nv + 1) * cn[h] + _dot_tn(kb[h], upd_rhs.astype(BF16))
                m_new[h] = blk(2 * H + h)[L - 1:L, :] + g_last
        m_s[...] = jnp.concatenate([m_new[h] for h in range(H)], axis=0)
        return carry

    lax.fori_loop(0, nc, chunk_step, 0)

    @pl.when(tb == pl.num_programs(1) - 1)
    def _():
        for h in range(H):
            c_ref[0, h] = cn_s[h, :, 0:dv]
            n_ref[0, h:h + 1, :] = jnp.sum(jnp.where(eye_k, cn_s[h, :, dv:dv + 1], 0.0), axis=0, keepdims=True)
        m_ref[0] = m_s[...]


def _mlstm(qk, v, og, gates, c0, n0, m0, head_g):
    B, T, _ = qk.shape
    _, H, dk, dv = c0.shape
    L = _mlstm_chunk(T)
    Tb = _pick(T, 512, L)
    _, NC, GR, LP = gates.shape
    y, c1, n1, m1 = pl.pallas_call(
        functools.partial(_mlstm_kernel, chunk=L, n_heads=H, dk=dk, dv=dv, head_group=4),
        grid=(B, T // Tb),
        in_specs=[pl.BlockSpec((1, Tb, 2 * H * dk), lambda b, t: (b, t, 0)),
                  pl.BlockSpec((1, Tb, H * dv), lambda b, t: (b, t, 0)),
                  pl.BlockSpec((1, Tb, H * dv), lambda b, t: (b, t, 0)),
                  pl.BlockSpec((1, NC, GR, LP), lambda b, t: (b, 0, 0, 0)),
                  pl.BlockSpec((1, H, dk, dv), lambda b, t: (b, 0, 0, 0)),
                  pl.BlockSpec((1, H, dk), lambda b, t: (b, 0, 0)),
                  pl.BlockSpec((1, H, 1), lambda b, t: (b, 0, 0)),
                  pl.BlockSpec((H, dv), lambda b, t: (0, 0))],
        out_specs=[pl.BlockSpec((1, Tb, H * dv), lambda b, t: (b, t, 0)),
                   pl.BlockSpec((1, H, dk, dv), lambda b, t: (b, 0, 0, 0)),
                   pl.BlockSpec((1, H, dk), lambda b, t: (b, 0, 0)),
                   pl.BlockSpec((1, H, LANES), lambda b, t: (b, 0, 0))],
        out_shape=[jax.ShapeDtypeStruct((B, T, H * dv), BF16),
                   jax.ShapeDtypeStruct((B, H, dk, dv), F32),
                   jax.ShapeDtypeStruct((B, H, dk), F32),
                   jax.ShapeDtypeStruct((B, H, LANES), F32)],
        scratch_shapes=[pltpu.VMEM((H, dk, dv + LANES), F32),
                        pltpu.VMEM((H, LANES), F32)],
        compiler_params=_cparams("arbitrary", "arbitrary"),
        name="mlstm",
    )(qk, v, og, gates, c0, n0, m0.reshape(B, H, 1), head_g)
    return y, c1, n1, m1[:, :, 0]


def _scan_pitch(rows):
    quads = -(-(-(-rows // SUBLANES)) // 4)
    return 4 * (quads + 1 - quads % 2)


def _rglru_kernel(xb_ref, gg_ref, cv0_ref, h0_ref, cw_ref, cb_ref, wr_ref, wi_ref, br_ref, bi_ref,
                  lam_ref, y_ref, h_ref, cv_ref, xext, a_s, u_s, *, pitch, group):
    tb = pl.program_id(1)
    Tb, W = xb_ref.shape[1], xb_ref.shape[2]
    nb, bw = wr_ref.shape[0], wr_ref.shape[1]
    taps = cw_ref.shape[0]
    rows = a_s.shape[1]
    pad = SUBLANES

    @pl.when(tb == 0)
    def _():
        h_ref[...] = h0_ref[...]
        cv_ref[...] = cv0_ref[...]

    x = xb_ref[0]
    xext[pl.ds(pad, Tb), :] = x
    xext[pl.ds(pad - (taps - 1), taps - 1), :] = cv_ref[0]
    xc = cb_ref[...] + cw_ref[taps - 1:taps, :] * x
    for j in range(taps - 1):
        xc = xc + cw_ref[j:j + 1, :] * xext[pl.ds(pad - (taps - 1) + j, Tb), :]
    cv_ref[0] = xext[pl.ds(pad + Tb - (taps - 1), taps - 1), :]

    log_sig_lam = _log_sigmoid(lam_ref[...])
    for n in range(nb):
        cols = slice(n * bw, (n + 1) * bw)
        xn = xc[:, cols]
        xnb = xn.astype(BF16)
        r = _sigmoid(_dot(xnb, wr_ref[n]) + br_ref[:, cols])
        i = _sigmoid(_dot(xnb, wi_ref[n]) + bi_ref[:, cols])
        log_a = LRU_C * r * log_sig_lam[:, cols]
        a = jnp.exp(log_a)
        a_s[n, 0:Tb, :] = a
        z = -jnp.tanh(log_a) * (1.0 + a * a)
        u_s[n, 0:Tb, :] = jnp.where(z > 0.0, z * lax.rsqrt(z), 0.0) * (i * xn)
        if rows > Tb:
            a_s[n, Tb:rows, :] = jnp.ones((rows - Tb, bw), F32)
            u_s[n, Tb:rows, :] = jnp.zeros((rows - Tb, bw), F32)

    for n0 in range(0, nb, group):
        slabs = list(range(n0, min(n0 + group, nb)))

        def pass1(i, carry):
            idx = pl.ds(i, SUBLANES, stride=pitch)
            out = []
            for (hz, ac), n in zip(carry, slabs):
                av = a_s[n, idx, :]
                hz = av * hz + u_s[n, idx, :]
                ac = av * ac
                u_s[n, idx, :] = hz
                a_s[n, idx, :] = ac
                out.append((hz, ac))
            return tuple(out)

        init = tuple((jnp.zeros((SUBLANES, bw), F32), jnp.ones((SUBLANES, bw), F32)) for _ in slabs)
        ends = lax.fori_loop(0, pitch, pass1, init, unroll=2)

        starts = []
        for (hz, ac), n in zip(ends, slabs):
            cols = slice(n * bw, (n + 1) * bw)
            h = h_ref[0, :, cols]
            per_block = []
            for j in range(SUBLANES):
                per_block.append(h)
                h = hz[j:j + 1, :] + ac[j:j + 1, :] * h
            h_ref[0, :, cols] = h
            starts.append(jnp.concatenate(per_block, axis=0))

        def pass2(i, carry):
            idx = pl.ds(i, SUBLANES, stride=pitch)
            for st, n in zip(starts, slabs):
                u_s[n, idx, :] = u_s[n, idx, :] + a_s[n, idx, :] * st
            return carry

        lax.fori_loop(0, pitch, pass2, 0, unroll=2)

    for n in range(nb):
        cols = slice(n * bw, (n + 1) * bw)
        y_ref[0, :, cols] = (u_s[n, 0:Tb, :] * gg_ref[0, :, cols]).astype(y_ref.dtype)


def _rglru(xb, gg, cv0, h0, conv_w, conv_b, w_r, w_i, b_r, b_i, lam):
    B, T, W = xb.shape
    taps = conv_w.shape[0]
    nb, bw = w_r.shape[0], w_r.shape[1]
    assert T >= taps - 1 and bw == LANES and nb * bw == W
    Tb = _pick(T, 256, SUBLANES)
    pitch = _scan_pitch(Tb)
    row = lambda a: a.reshape(1, W)
    y, h1, cv1 = pl.pallas_call(
        functools.partial(_rglru_kernel, pitch=pitch, group=8),
        grid=(B, T // Tb),
        in_specs=[pl.BlockSpec((1, Tb, W), lambda b, t: (b, t, 0)),
                  pl.BlockSpec((1, Tb, W), lambda b, t: (b, t, 0)),
                  pl.BlockSpec((1, taps - 1, W), lambda b, t: (b, 0, 0)),
                  pl.BlockSpec((1, 1, W), lambda b, t: (b, 0, 0)),
                  pl.BlockSpec((taps, W), lambda b, t: (0, 0)),
                  pl.BlockSpec((1, W), lambda b, t: (0, 0)),
                  pl.BlockSpec((nb, bw, bw), lambda b, t: (0, 0, 0)),
                  pl.BlockSpec((nb, bw, bw), lambda b, t: (0, 0, 0)),
                  pl.BlockSpec((1, W), lambda b, t: (0, 0)),
                  pl.BlockSpec((1, W), lambda b, t: (0, 0)),
                  pl.BlockSpec((1, W), lambda b, t: (0, 0))],
        out_specs=[pl.BlockSpec((1, Tb, W), lambda b, t: (b, t, 0)),
                   pl.BlockSpec((1, 1, W), lambda b, t: (b, 0, 0)),
                   pl.BlockSpec((1, taps - 1, W), lambda b, t: (b, 0, 0))],
        out_shape=[jax.ShapeDtypeStruct((B, T, W), BF16),
                   jax.ShapeDtypeStruct((B, 1, W), F32),
                   jax.ShapeDtypeStruct((B, taps - 1, W), F32)],
        scratch_shapes=[pltpu.VMEM((Tb + SUBLANES, W), F32),
                        pltpu.VMEM((nb, SUBLANES * pitch, bw), F32),
                        pltpu.VMEM((nb, SUBLANES * pitch, bw), F32)],
        compiler_params=_cparams("arbitrary", "arbitrary"),
        name="rglru",
    )(xb, gg, cv0, h0.reshape(B, 1, W), conv_w, row(conv_b), w_r.astype(BF16), w_i.astype(BF16),
      row(b_r), row(b_i), row(lam))
    return y, h1.reshape(B, W), cv1


def _outproj_kernel(ya_ref, yb_ref, wt_ref, wb_ref, x_ref, g_ref, o_ref):
    bb, tt, wa = ya_ref.shape
    wb = yb_ref.shape[2]
    mix = (_dot(ya_ref[...].reshape(bb * tt, wa), wt_ref[...])
           + _dot(yb_ref[...].reshape(bb * tt, wb), wb_ref[...]))
    o_ref[...] = x_ref[...] + g_ref[...] * mix.reshape(bb, tt, mix.shape[-1])


def _outproj(ya, yb, w_out, x, g1):
    B, T, D = x.shape
    wa, wb = ya.shape[2], yb.shape[2]
    assert wa == wb
    bb, tt = _row_tiling(B, T, 1024)
    tn = _pick(D, 1024, LANES)
    return pl.pallas_call(
        _outproj_kernel,
        grid=(B // bb, T // tt, D // tn),
        in_specs=[pl.BlockSpec((bb, tt, wa), lambda b, t, j: (b, t, 0)),
                  pl.BlockSpec((bb, tt, wb), lambda b, t, j: (b, t, 0)),
                  pl.BlockSpec((wa, tn), lambda b, t, j: (0, j)),
                  pl.BlockSpec((wb, tn), lambda b, t, j: (1, j)),
                  pl.BlockSpec((bb, tt, tn), lambda b, t, j: (b, t, j)),
                  pl.BlockSpec((bb, 1, tn), lambda b, t, j: (b, 0, j))],
        out_specs=pl.BlockSpec((bb, tt, tn), lambda b, t, j: (b, t, j)),
        out_shape=jax.ShapeDtypeStruct((B, T, D), F32),
        compiler_params=_cparams("arbitrary", "arbitrary", "arbitrary"),
        name="outproj",
    )(ya, yb, w_out, w_out, x, g1)


def _ffn_kernel(x_ref, ng_ref, sc_ref, sh_ref, g2_ref, *rest, n_col_chunks, final_norm, tiles_per_step, n_tiles):
    w_refs, (gf_ref, o_ref, h_s) = rest[:3 * tiles_per_step], rest[3 * tiles_per_step:]
    j = pl.program_id(2)
    last = pl.num_programs(2) - 1
    bb, tt, D = x_ref.shape

    def row_loop(body, rc):
        assert tt % rc == 0
        for b in range(bb):
            for i in range(tt // rc):
                body(b, i * rc)

    @pl.when(j == 0)
    def _():
        rc = 2 * SUBLANES

        def half(b, r0):
            x = x_ref[b, pl.ds(r0, SUBLANES), :]
            r = lax.rsqrt(jnp.mean(x * x, axis=-1, keepdims=True) + EPS)
            return x * r * (ng_ref[0] * (1.0 + sc_ref[b])) + sh_ref[b]

        def body(b, r0):
            h = jnp.concatenate([half(b, r0), half(b, r0 + SUBLANES)], axis=0)
            h_s[pl.ds(b * tt + r0, rc), :] = h.astype(BF16)
            o_ref[b, pl.ds(r0, rc), :] = jnp.zeros((rc, D), F32)
        row_loop(body, rc)

    def hidden_tiles(count):
        hb = h_s[...]
        cw = D // n_col_chunks
        acts = []
        for k in range(count):
            gate = _dot(hb, w_refs[3 * k][...])
            up = _dot(hb, w_refs[3 * k + 1][...])
            acts.append((gate * _sigmoid(gate) * up).astype(BF16))
        for c in range(n_col_chunks):
            cols = slice(c * cw, (c + 1) * cw)
            down = _dot(acts[0], w_refs[2][:, cols])
            for k in range(1, count):
                down = down + _dot(acts[k], w_refs[3 * k + 2][:, cols])
            o_ref[:, :, cols] += down.reshape(bb, tt, cw)

    tail = n_tiles - (pl.cdiv(n_tiles, tiles_per_step) - 1) * tiles_per_step
    if tail == tiles_per_step:
        hidden_tiles(tiles_per_step)
    else:
        @pl.when(j < last)
        def _():
            hidden_tiles(tiles_per_step)

        @pl.when(j == last)
        def _():
            hidden_tiles(tail)

    @pl.when(j == last)
    def _():
        def body(b, r0):
            rows = pl.ds(r0, SUBLANES)
            x2 = x_ref[b, rows, :] + g2_ref[b] * o_ref[b, rows, :]
            if final_norm:
                x2 = x2 * lax.rsqrt(jnp.mean(x2 * x2, axis=-1, keepdims=True) + EPS) * gf_ref[0]
            o_ref[b, rows, :] = x2
        row_loop(body, SUBLANES)


def _ffn(x, ng, sc, sh, g2, w_g, w_u, w_down, gf, final_norm):
    B, T, D = x.shape
    F = w_down.shape[0]
    tf = _pick(F, 256, LANES)
    nf = F // tf
    u_off = nf if w_u.shape[1] == 2 * F else 0
    bb, tt = _row_tiling(B, T, 512)
    tps = 2
    w_specs, w_args = [], []
    for k in range(tps):
        tile = lambda j, k=k: jnp.minimum(j * tps + k, nf - 1)
        w_specs += [pl.BlockSpec((D, tf), lambda b, t, j, tile=tile: (0, tile(j))),
                    pl.BlockSpec((D, tf), lambda b, t, j, tile=tile: (0, u_off + tile(j))),
                    pl.BlockSpec((tf, D), lambda b, t, j, tile=tile: (tile(j), 0))]
        w_args += [w_g, w_u, w_down]
    return pl.pallas_call(
        functools.partial(_ffn_kernel, n_col_chunks=max(1, D // 1024), final_norm=final_norm,
                          tiles_per_step=tps, n_tiles=nf),
        grid=(B // bb, T // tt, pl.cdiv(nf, tps)),
        in_specs=[pl.BlockSpec((bb, tt, D), lambda b, t, j: (b, t, 0), pipeline_mode=pl.Buffered(1)),
                  pl.BlockSpec((1, 1, D), lambda b, t, j: (0, 0, 0)),
                  pl.BlockSpec((bb, 1, D), lambda b, t, j: (b, 0, 0)),
                  pl.BlockSpec((bb, 1, D), lambda b, t, j: (b, 0, 0)),
                  pl.BlockSpec((bb, 1, D), lambda b, t, j: (b, 0, 0))] + w_specs + [
                  pl.BlockSpec((1, 1, D), lambda b, t, j: (0, 0, 0))],
        out_specs=pl.BlockSpec((bb, tt, D), lambda b, t, j: (b, t, 0)),
        out_shape=jax.ShapeDtypeStruct((B, T, D), F32),
        scratch_shapes=[pltpu.VMEM((bb * tt, D), BF16)],
        compiler_params=_cparams("arbitrary", "arbitrary", "arbitrary"),
        name="ffn",
    )(x, ng, sc, sh, g2, *w_args, gf)


def _mixer(x, ada, state, p, bf16_weights, cast_jobs):
    sh1, sc1, g1 = ada
    c0, n0, m0, hl0, cv0 = state
    B, T, D = x.shape
    _, H, dk, dv = c0.shape
    W = cv0.shape[2]
    n_qk, n_v = 2 * H * dk, H * dv
    assert n_qk == n_v and (3 * n_v) % LANES == 0
    h1, gates = _normmod_gates(x, p["norm1_g"], sc1, sh1, p["w_in"], 3 * n_v // LANES, p["b_gates"], H)
    h1 = h1.reshape(B * T, D)
    jobs = list(cast_jobs.items())

    def proj(w, blk, n, dt, ep, cs=None):
        name, job = jobs.pop() if jobs else (None, None)
        out = _inproj(h1, w, blk, n, dt, ep, cs, job)
        if job is not None:
            out, bf16_weights[name] = out
        return out.reshape(B, T, n)

    qk = proj(p["w_in"], 0, n_qk, BF16, "colscale", p["qk_scale"])
    v = proj(p["w_in"], 1, n_v, BF16, "none")
    og = proj(p["w_in"], 2, n_v, F32, "sigmoid")
    xb = proj(p["w_xg"], 0, W, F32, "none")
    gg = proj(p["w_xg"], 1, W, F32, "gelu")
    assert not jobs
    ya, c1, n1, m1 = _mlstm(qk, v, og, gates, c0, n0, m0, p["head_norm_g"])
    yb, hl1, cv1 = _rglru(xb, gg, cv0, hl0, p["conv_w"], p["conv_b"],
                          p["w_r"], p["w_i"], p["b_r"], p["b_i"], p["lru_lambda"])
    return _outproj(ya, yb, bf16_weights["w_out"], x, g1), (c1, n1, m1, hl1, cv1)


def kernel(x_prompt, x_sample, c_prompt, c_sample, state_mlstm_C, state_mlstm_n, state_mlstm_m, state_lru_h, state_conv, w_ada, b_ada, norm1_g, norm2_g, w_in, b_gates_a, head_norm_g, conv_w, conv_b, w_r, b_r, w_i, b_i, lru_lambda, w_out, w_gu, w_down, normf_g):
    depth = w_in.shape[0]
    Bp, Tp, D = x_prompt.shape
    Bs = x_sample.shape[0]
    H, dk, dv = state_mlstm_C.shape[2:]
    W = conv_w.shape[2]
    taps = conv_w.shape[1]
    c_x = 2 * H * dk + 2 * H * dv + 2 * H

    xp, xs = x_prompt, x_sample
    p_states, s_states = [], []
    R = Bp + Bs
    Rpad = -(-R // SUBLANES) * SUBLANES
    c_all = jnp.concatenate([c_prompt, c_sample, jnp.zeros((Rpad - R, D), F32)], axis=0)
    gf = normf_g.reshape(1, 1, D)
    qk_scale = jnp.concatenate([jnp.ones((1, H * dk), F32), jnp.full((1, H * dk), dk ** -0.5, F32)], axis=1)
    for l in range(depth):
        w_in_l = w_in[l]
        p = {
            "norm1_g": norm1_g[l].reshape(1, 1, D),
            "w_in": w_in_l.T.astype(BF16),
            "w_xg": w_in_l.T[c_x:].astype(BF16),
            "qk_scale": qk_scale,
            "b_gates": jnp.pad(b_gates_a[l].reshape(1, 2 * H), ((0, 0), (0, LANES - 2 * H))),
            "head_norm_g": head_norm_g[l],
            "conv_w": conv_w[l], "conv_b": conv_b[l],
            "w_r": w_r[l], "w_i": w_i[l], "b_r": b_r[l], "b_i": b_i[l],
            "lru_lambda": lru_lambda[l],
        }
        steps = (Bp * Tp) // _inproj_tile(Bp * Tp)
        planned = {"w_g": _cast_job(w_gu[l], steps, 2, 0), "w_u": _cast_job(w_gu[l], steps, 2, 1),
                   "w_down": _cast_job(w_down[l], steps), "w_out": _cast_job(w_out[l], steps)}
        cast_jobs = {k: j for k, j in planned.items() if j is not None}
        bf16_weights = {}
        if "w_g" not in cast_jobs or "w_u" not in cast_jobs:
            cast_jobs.pop("w_g", None), cast_jobs.pop("w_u", None)
            bf16_weights["w_g"] = bf16_weights["w_u"] = w_gu[l].astype(BF16)
        if "w_down" not in cast_jobs:
            bf16_weights["w_down"] = w_down[l].astype(BF16)
        if "w_out" not in cast_jobs:
            bf16_weights["w_out"] = w_out[l].astype(BF16)
        norm2 = norm2_g[l].reshape(1, 1, D)
        ada = _ada(c_all, w_ada[l], b_ada[l].reshape(1, -1))
        ada_p = [ada[:Bp, i * D:(i + 1) * D].reshape(Bp, 1, D) for i in range(6)]
        ada_s = [ada[Bp:R, i * D:(i + 1) * D].reshape(Bs, 1, D) for i in range(6)]

        zero_state = (jnp.zeros((Bp, H, dk, dv), F32), jnp.zeros((Bp, H, dk), F32),
                      jnp.zeros((Bp, H), F32), jnp.zeros((Bp, W), F32),
                      jnp.zeros((Bp, taps - 1, W), F32))
        cache_state = (state_mlstm_C[l], state_mlstm_n[l], state_mlstm_m[l], state_lru_h[l],
                       state_conv[l])
        last = l == depth - 1
        new = []
        for x, (sh1, sc1, g1, sh2, sc2, g2), st in ((xp, ada_p, zero_state), (xs, ada_s, cache_state)):
            x1, st1 = _mixer(x, (sh1, sc1, g1), st, p, bf16_weights, cast_jobs)
            cast_jobs = {}
            y = _ffn(x1, norm2, sc2, sh2, g2, bf16_weights["w_g"], bf16_weights["w_u"],
                     bf16_weights["w_down"], gf, final_norm=last)
            new.append((y, st1))
        (xp, st_p), (xs, st_s) = new
        p_states.append(st_p)
        s_states.append(st_s)

    stack = lambda sts, i: jnp.stack([s[i] for s in sts])
    return (xp, xs) + tuple(stack(p_states, i) for i in range(5)) + tuple(stack(s_states, i) for i in range(5))
```

```python
import functools

import jax
import jax.numpy as jnp
from jax import lax
from jax.experimental import pallas as pl
from jax.experimental.pallas import tpu as pltpu

F32 = jnp.float32
BF16 = jnp.bfloat16
EPS = 1e-6
LRU_C = 8.0
LANES = 128
SUBLANES = 8
VMEM_LIMIT_BYTES = 60 * 1024 * 1024


def _cparams(*sem):
    return pltpu.CompilerParams(dimension_semantics=sem, vmem_limit_bytes=VMEM_LIMIT_BYTES)


def _pick(total, target, quantum):
    if total <= target:
        return total
    best = None
    for cand in range(quantum, target + 1, quantum):
        if total % cand == 0:
            best = cand
    assert best is not None, (total, target, quantum)
    return best


def _row_tiling(B, T, target_rows):
    if T >= target_rows:
        return 1, _pick(T, target_rows, 16)
    bb = _pick(B, max(1, target_rows // T), 1)
    return bb, T


def _mlstm_chunk(T):
    return _pick(T, LANES, 16)


def _log_sigmoid(x):
    return jnp.minimum(x, 0.0) - jnp.log1p(jnp.exp(-jnp.abs(x)))


def _sigmoid(x):
    return 0.5 * (1.0 + jnp.tanh(0.5 * x))


def _gelu_tanh(x):
    c = 0.7978845608028654
    return 0.5 * x * (1.0 + jnp.tanh(c * (x + 0.044715 * (x * x * x))))


def _dot(a, b):
    return jnp.dot(a, b, preferred_element_type=F32)


def _dot_nt(a, b):
    return lax.dot_general(a, b, (((1,), (1,)), ((), ())), preferred_element_type=F32)


def _dot_tn(a, b):
    return lax.dot_general(a, b, (((0,), (0,)), ((), ())), preferred_element_type=F32)


def _split_bf16(x, terms):
    out = []
    for _ in range(terms - 1):
        t = x.astype(BF16)
        out.append(t)
        x = x - t.astype(F32)
    out.append(x.astype(BF16))
    return out


def _ada_kernel(c_ref, w_ref, b_ref, o_ref):
    c = c_ref[...]
    s = (c * _sigmoid(c)).astype(BF16)
    o_ref[...] = _dot(s, w_ref[...].astype(BF16)) + b_ref[...]


def _ada(c, w, b):
    R, D = c.shape
    N = w.shape[1]
    tn = _pick(N, 512, LANES)
    return pl.pallas_call(
        _ada_kernel,
        grid=(N // tn,),
        in_specs=[pl.BlockSpec((R, D), lambda j: (0, 0)),
                  pl.BlockSpec((D, tn), lambda j: (0, j)),
                  pl.BlockSpec((1, tn), lambda j: (0, j))],
        out_specs=pl.BlockSpec((R, tn), lambda j: (0, j)),
        out_shape=jax.ShapeDtypeStruct((R, N), F32),
        compiler_params=_cparams("arbitrary"),
        name="ada",
    )(c, w, b)


def _normmod(x, geff, sh):
    r = lax.rsqrt(jnp.mean(x * x, axis=-1, keepdims=True) + EPS)
    return x * r * geff + sh


def _normmod_gates_kernel(x_ref, g_ref, sc_ref, sh_ref, wg_ref, bg_ref, h_ref, gr_ref, gt_s, *, n_heads, chunk):
    bb, tt, D = x_ref.shape
    H = n_heads
    hb = _normmod(x_ref[...], g_ref[...] * (1.0 + sc_ref[...]), sh_ref[...]).astype(BF16)
    h_ref[...] = hb
    pre = _dot_nt(hb.reshape(bb * tt, D), wg_ref[...]) + bg_ref[...]
    lane = lax.broadcasted_iota(jnp.int32, pre.shape, 1)
    gates = jnp.where(lane < H, pre, _log_sigmoid(pre))
    rows, lanes_out = gt_s.shape[2], gt_s.shape[3]
    if lanes_out > chunk:
        gt_s[...] = jnp.zeros_like(gt_s)
    blocks = [(i, c) for i in range(bb) for c in range(tt // chunk)]
    for i, c in blocks:
        r0 = i * tt + c * chunk
        gt_s[i, c, :, 0:chunk] = gates[r0:r0 + chunk, :].T[:rows, :]
    ig = jnp.concatenate([gt_s[i, c, 0:H, :] for i, c in blocks], axis=0)
    lf = jnp.concatenate([gt_s[i, c, H:2 * H, :] for i, c in blocks], axis=0)
    lane_t = lax.broadcasted_iota(jnp.int32, lf.shape, 1)
    bcum = _lane_scan(lf, jnp.add, lane_t, 0.0)
    a = ig - bcum
    cm = _lane_scan(a, jnp.maximum, lane_t, -jnp.inf)
    if gr_ref.shape[2] > 3 * H:
        gr_ref[...] = jnp.zeros_like(gr_ref)
    for k, (i, c) in enumerate(blocks):
        for r, val in enumerate((a, cm, bcum)):
            gr_ref[i, c, r * H:(r + 1) * H, :] = val[k * H:(k + 1) * H, :]


def _normmod_gates(x, g, sc, sh, wg, gate_blk, bg, n_heads):
    B, T, D = x.shape
    L = _mlstm_chunk(T)
    bb, tt = _row_tiling(B, T, 512)
    assert tt % L == 0
    GR = -(-2 * n_heads // SUBLANES) * SUBLANES
    GR3 = -(-3 * n_heads // SUBLANES) * SUBLANES
    LP = max(L, LANES)
    return pl.pallas_call(
        functools.partial(_normmod_gates_kernel, n_heads=n_heads, chunk=L),
        grid=(B // bb, T // tt),
        in_specs=[pl.BlockSpec((bb, tt, D), lambda b, t: (b, t, 0)),
                  pl.BlockSpec((1, 1, D), lambda b, t: (0, 0, 0)),
                  pl.BlockSpec((bb, 1, D), lambda b, t: (b, 0, 0)),
                  pl.BlockSpec((bb, 1, D), lambda b, t: (b, 0, 0)),
                  pl.BlockSpec((LANES, D), lambda b, t: (gate_blk, 0)),
                  pl.BlockSpec((1, LANES), lambda b, t: (0, 0))],
        out_specs=[pl.BlockSpec((bb, tt, D), lambda b, t: (b, t, 0)),
                   pl.BlockSpec((bb, tt // L, GR3, LP), lambda b, t: (b, t, 0, 0))],
        out_shape=[jax.ShapeDtypeStruct((B, T, D), BF16),
                   jax.ShapeDtypeStruct((B, T // L, GR3, LP), F32)],
        scratch_shapes=[pltpu.VMEM((bb, tt // L, GR, LP), F32)],
        compiler_params=_cparams("arbitrary", "arbitrary"),
        name="normmod_gates",
    )(x, g, sc, sh, wg, bg)


def _inproj_kernel(a_ref, w_ref, *rest, epilogue, with_cast):
    rest = list(rest)
    if with_cast:
        cast_out = rest.pop()
        o_ref = rest.pop()
        cast_out[...] = rest.pop()[...].astype(BF16)
    else:
        o_ref = rest.pop()
    acc = _dot_nt(a_ref[...], w_ref[...])
    if epilogue == "colscale":
        acc = acc * rest[0][...]
    elif epilogue == "sigmoid":
        acc = _sigmoid(acc)
    elif epilogue == "gelu":
        acc = _gelu_tanh(acc)
    o_ref[...] = acc.astype(o_ref.dtype)


def _inproj_tile(M):
    return _pick(M, 512, 16)


def _cast_job(w, steps, col_blocks=1, col_block=0):
    R, C = w.shape
    Cb = C // col_blocks
    if C % col_blocks or (col_blocks > 1 and Cb % LANES):
        return None
    if R % (steps * 2 * SUBLANES) == 0:
        return w, (R // steps, Cb), (lambda i: (i, col_block)), (lambda i: (i, 0)), (R, Cb)
    if col_blocks == 1 and C % (steps * LANES) == 0:
        return w, (R, C // steps), (lambda i: (0, i)), (lambda i: (0, i)), (R, C)
    return None


def _inproj(a, wt, row_block, N, out_dtype, epilogue, colscale=None, cast=None):
    M, K = a.shape
    tm = _inproj_tile(M)
    in_specs = [pl.BlockSpec((tm, K), lambda i: (i, 0)),
                pl.BlockSpec((N, K), lambda i: (row_block, 0), pipeline_mode=pl.Buffered(1))]
    args = [a, wt]
    if epilogue == "colscale":
        in_specs.append(pl.BlockSpec((1, N), lambda i: (0, 0)))
        args.append(colscale)
    out_specs = [pl.BlockSpec((tm, N), lambda i: (i, 0))]
    out_shape = [jax.ShapeDtypeStruct((M, N), out_dtype)]
    if cast is not None:
        src, blk, src_map, dst_map, shape = cast
        in_specs.append(pl.BlockSpec(blk, src_map))
        args.append(src)
        out_specs.append(pl.BlockSpec(blk, dst_map))
        out_shape.append(jax.ShapeDtypeStruct(shape, BF16))
    outs = pl.pallas_call(
        functools.partial(_inproj_kernel, epilogue=epilogue, with_cast=cast is not None),
        grid=(M // tm,),
        in_specs=in_specs,
        out_specs=out_specs,
        out_shape=out_shape,
        compiler_params=_cparams("arbitrary"),
        name="inproj_" + epilogue,
    )(*args)
    return outs if cast is not None else outs[0]


def _lane_scan(x, op, lane, identity):
    d = 1
    while d < x.shape[1]:
        x = op(x, jnp.where(lane >= d, pltpu.roll(x, d, axis=1), identity))
        d *= 2
    return x


def _mlstm_kernel(qk_ref, v_ref, og_ref, gr_ref, c0_ref, n0_ref, m0_ref, hg_ref,
                  y_ref, c_ref, n_ref, m_ref, cn_s, m_s, *, chunk, n_heads, dk, dv, head_group):
    tb = pl.program_id(1)
    L, H = chunk, n_heads
    Tb = qk_ref.shape[1]
    nc = Tb // L
    LP = gr_ref.shape[3]
    assert dv % LANES == 0 and L <= LANES
    rep = lambda x, n: jnp.concatenate([x] * n, axis=1)

    eye_k = (lax.broadcasted_iota(jnp.int32, (dk, dk), 0) == lax.broadcasted_iota(jnp.int32, (dk, dk), 1))

    @pl.when(tb == 0)
    def _():
        for h in range(H):
            cn_s[h, :, 0:dv] = c0_ref[0, h]
            n_col = jnp.sum(jnp.where(eye_k, n0_ref[0, h:h + 1, :], 0.0), axis=1, keepdims=True)
            cn_s[h, :, dv:dv + LANES] = jnp.broadcast_to(n_col, (dk, LANES))
        m_s[...] = jnp.broadcast_to(m0_ref[0], m_s.shape)

    row = lax.broadcasted_iota(jnp.int32, (L, L), 0)
    col = lax.broadcasted_iota(jnp.int32, (L, L), 1)
    tril = col <= row
    nr = 3 * H
    spread = ((lax.broadcasted_iota(jnp.int32, (3 * nr, nr * LANES), 1) // LANES)
              == (lax.broadcasted_iota(jnp.int32, (3 * nr, nr * LANES), 0) % nr)).astype(BF16)
    ones_2l = jnp.ones((2 * L, LANES), BF16)
    mean_2v = jnp.full((2 * dv, LANES), 1.0 / dv, BF16)
    assert dv & (dv - 1) == 0

    def chunk_step(c, carry):
        sl = pl.ds(pl.multiple_of(c * L, L), L)
        rows = gr_ref[0, tb * nc + c][0:nr]
        a = rows[0:H]
        cols = _dot_tn(jnp.concatenate(_split_bf16(rows, 3), axis=0), spread)
        blk = lambda r: cols[0:L, r * LANES:(r + 1) * LANES]
        m_all = m_s[...]
        nv = dv // LANES
        m_new = {}
        for h0 in range(0, H, head_group):
            heads = range(h0, min(h0 + head_group, H))
            qb = {h: qk_ref[0, sl, h * dk:(h + 1) * dk] for h in heads}
            kb = {h: qk_ref[0, sl, (H + h) * dk:(H + h + 1) * dk] for h in heads}
            vb = {h: v_ref[0, sl, h * dv:(h + 1) * dv] for h in heads}
            cn = {h: cn_s[h] for h in heads}
            qk = {h: _dot_nt(qb[h], kb[h]) for h in heads}
            qc = {h: _dot(qb[h], cn[h].astype(BF16)) for h in heads}

            g, s_inter, sv, rs = {}, {}, {}, {}
            for h in heads:
                g[h] = jnp.maximum(m_all[h:h + 1, :], blk(H + h))
                s_inter[h] = jnp.exp(m_all[h:h + 1, :] - g[h])
                w = jnp.where(tril, jnp.exp(jnp.minimum(a[h:h + 1, 0:L] - g[h][:, 0:L], 0.0)), 0.0)
                s2 = _split_bf16(qk[h] * w, 2)
                sv[h] = _dot(s2[0], vb[h])
                rs[h] = _dot(jnp.concatenate(s2, axis=1), ones_2l)

            hh, ms = {}, {}
            for h in heads:
                num = rep(s_inter[h], nv) * qc[h][:, 0:dv] + sv[h]
                den = s_inter[h] * qc[h][:, dv:dv + LANES] + rs[h]
                inv = 1.0 / jnp.maximum(jnp.abs(den), jnp.exp(-(blk(2 * H + h) + g[h])))
                hh[h] = num * rep(inv, nv)
                ms[h] = _dot(jnp.concatenate(_split_bf16(hh[h] * hh[h], 2), axis=1), mean_2v)

            for h in heads:
                hn = hh[h] * rep(lax.rsqrt(ms[h] + EPS), nv) * hg_ref[h:h + 1, :]
                y_ref[0, sl, h * dv:(h + 1) * dv] = (hn * og_ref[0, sl, h * dv:(h + 1) * dv]).astype(y_ref.dtype)
                g_last = g[h][L - 1:L, :]
                w_last = jnp.exp(blk(h) - g_last)
                upd_rhs = jnp.concatenate([rep(w_last, nv) * vb[h].astype(F32), w_last], axis=1)
                decay = jnp.exp(m_all[h:h + 1, :] - g_last)
                cn_s[h] = rep(decay, nv + 1) * cn[h] + _dot_tn(kb[h], upd_rhs.astype(BF16))
                m_new[h] = blk(2 * H + h)[L - 1:L, :] + g_last
        m_s[...] = jnp.concatenate([m_new[h] for h in range(H)], axis=0)
        return carry

    lax.fori_loop(0, nc, chunk_step, 0)

    @pl.when(tb == pl.num_programs(1) - 1)
    def _():
        for h in range(H):
            c_ref[0, h] = cn_s[h, :, 0:dv]
            n_ref[0, h:h + 1, :] = jnp.sum(jnp.where(eye_k, cn_s[h, :, dv:dv + 1], 0.0), axis=0, keepdims=True)
        m_ref[0] = m_s[...]


def _mlstm(qk, v, og, gates, c0, n0, m0, head_g):
    B, T, _ = qk.shape
    _, H, dk, dv = c0.shape
    L = _mlstm_chunk(T)
    Tb = _pick(T, 512, L)
    _, NC, GR, LP = gates.shape
    y, c1, n1, m1 = pl.pallas_call(
        functools.partial(_mlstm_kernel, chunk=L, n_heads=H, dk=dk, dv=dv, head_group=4),
        grid=(B, T // Tb),
        in_specs=[pl.BlockSpec((1, Tb, 2 * H * dk), lambda b, t: (b, t, 0)),
                  pl.BlockSpec((1, Tb, H * dv), lambda b, t: (b, t, 0)),
                  pl.BlockSpec((1, Tb, H * dv), lambda b, t: (b, t, 0)),
                  pl.BlockSpec((1, NC, GR, LP), lambda b, t: (b, 0, 0, 0)),
                  pl.BlockSpec((1, H, dk, dv), lambda b, t: (b, 0, 0, 0)),
                  pl.BlockSpec((1, H, dk), lambda b, t: (b, 0, 0)),
                  pl.BlockSpec((1, H, 1), lambda b, t: (b, 0, 0)),
                  pl.BlockSpec((H, dv), lambda b, t: (0, 0))],
        out_specs=[pl.BlockSpec((1, Tb, H * dv), lambda b, t: (b, t, 0)),
                   pl.BlockSpec((1, H, dk, dv), lambda b, t: (b, 0, 0, 0)),
                   pl.BlockSpec((1, H, dk), lambda b, t: (b, 0, 0)),
                   pl.BlockSpec((1, H, LANES), lambda b, t: (b, 0, 0))],
        out_shape=[jax.ShapeDtypeStruct((B, T, H * dv), BF16),
                   jax.ShapeDtypeStruct((B, H, dk, dv), F32),
                   jax.ShapeDtypeStruct((B, H, dk), F32),
                   jax.ShapeDtypeStruct((B, H, LANES), F32)],
        scratch_shapes=[pltpu.VMEM((H, dk, dv + LANES), F32),
                        pltpu.VMEM((H, LANES), F32)],
        compiler_params=_cparams("arbitrary", "arbitrary"),
        name="mlstm",
    )(qk, v, og, gates, c0, n0, m0.reshape(B, H, 1), head_g)
    return y, c1, n1, m1[:, :, 0]


def _scan_pitch(rows):
    quads = -(-(-(-rows // SUBLANES)) // 4)
    return 4 * (quads + 1 - quads % 2)


def _rglru_kernel(xb_ref, gg_ref, cv0_ref, h0_ref, cw_ref, cb_ref, wr_ref, wi_ref, br_ref, bi_ref,
                  lam_ref, y_ref, h_ref, cv_ref, xext, a_s, u_s, *, pitch, group):
    tb = pl.program_id(1)
    Tb, W = xb_ref.shape[1], xb_ref.shape[2]
    nb, bw = wr_ref.shape[0], wr_ref.shape[1]
    taps = cw_ref.shape[0]
    rows = a_s.shape[1]
    pad = SUBLANES

    @pl.when(tb == 0)
    def _():
        h_ref[...] = h0_ref[...]
        cv_ref[...] = cv0_ref[...]

    x = xb_ref[0]
    xext[pl.ds(pad, Tb), :] = x
    xext[pl.ds(pad - (taps - 1), taps - 1), :] = cv_ref[0]
    xc = cb_ref[...] + cw_ref[taps - 1:taps, :] * x
    for j in range(taps - 1):
        xc = xc + cw_ref[j:j + 1, :] * xext[pl.ds(pad - (taps - 1) + j, Tb), :]
    cv_ref[0] = xext[pl.ds(pad + Tb - (taps - 1), taps - 1), :]

    log_sig_lam = _log_sigmoid(lam_ref[...])
    for n in range(nb):
        cols = slice(n * bw, (n + 1) * bw)
        xn = xc[:, cols]
        xnb = xn.astype(BF16)
        r = _sigmoid(_dot(xnb, wr_ref[n]) + br_ref[:, cols])
        i = _sigmoid(_dot(xnb, wi_ref[n]) + bi_ref[:, cols])
        log_a = LRU_C * r * log_sig_lam[:, cols]
        a = jnp.exp(log_a)
        a_s[n, 0:Tb, :] = a
        z = -jnp.tanh(log_a) * (1.0 + a * a)
        u_s[n, 0:Tb, :] = jnp.where(z > 0.0, z * lax.rsqrt(z), 0.0) * (i * xn)
        if rows > Tb:
            a_s[n, Tb:rows, :] = jnp.ones((rows - Tb, bw), F32)
            u_s[n, Tb:rows, :] = jnp.zeros((rows - Tb, bw), F32)

    for n0 in range(0, nb, group):
        slabs = list(range(n0, min(n0 + group, nb)))

        def pass1(i, carry):
            idx = pl.ds(i, SUBLANES, stride=pitch)
            out = []
            for (hz, ac), n in zip(carry, slabs):
                av = a_s[n, idx, :]
                hz = av * hz + u_s[n, idx, :]
                ac = av * ac
                u_s[n, idx, :] = hz
                a_s[n, idx, :] = ac
                out.append((hz, ac))
            return tuple(out)

        init = tuple((jnp.zeros((SUBLANES, bw), F32), jnp.ones((SUBLANES, bw), F32)) for _ in slabs)
        ends = lax.fori_loop(0, pitch, pass1, init, unroll=2)

        starts = []
        for (hz, ac), n in zip(ends, slabs):
            cols = slice(n * bw, (n + 1) * bw)
            h = h_ref[0, :, cols]
            per_block = []
            for j in range(SUBLANES):
                per_block.append(h)
                h = hz[j:j + 1, :] + ac[j:j + 1, :] * h
            h_ref[0, :, cols] = h
            starts.append(jnp.concatenate(per_block, axis=0))

        def pass2(i, carry):
            idx = pl.ds(i, SUBLANES, stride=pitch)
            for st, n in zip(starts, slabs):
                u_s[n, idx, :] = u_s[n, idx, :] + a_s[n, idx, :] * st
            return carry

        lax.fori_loop(0, pitch, pass2, 0, unroll=2)

    for n in range(nb):
        cols = slice(n * bw, (n + 1) * bw)
        y_ref[0, :, cols] = (u_s[n, 0:Tb, :] * gg_ref[0, :, cols]).astype(y_ref.dtype)


def _rglru(xb, gg, cv0, h0, conv_w, conv_b, w_r, w_i, b_r, b_i, lam):
    B, T, W = xb.shape
    taps = conv_w.shape[0]
    nb, bw = w_r.shape[0], w_r.shape[1]
    assert T >= taps - 1 and bw == LANES and nb * bw == W
    Tb = _pick(T, 256, SUBLANES)
    pitch = _scan_pitch(Tb)
    row = lambda a: a.reshape(1, W)
    y, h1, cv1 = pl.pallas_call(
        functools.partial(_rglru_kernel, pitch=pitch, group=8),
        grid=(B, T // Tb),
        in_specs=[pl.BlockSpec((1, Tb, W), lambda b, t: (b, t, 0)),
                  pl.BlockSpec((1, Tb, W), lambda b, t: (b, t, 0)),
                  pl.BlockSpec((1, taps - 1, W), lambda b, t: (b, 0, 0)),
                  pl.BlockSpec((1, 1, W), lambda b, t: (b, 0, 0)),
                  pl.BlockSpec((taps, W), lambda b, t: (0, 0)),
                  pl.BlockSpec((1, W), lambda b, t: (0, 0)),
                  pl.BlockSpec((nb, bw, bw), lambda b, t: (0, 0, 0)),
                  pl.BlockSpec((nb, bw, bw), lambda b, t: (0, 0, 0)),
                  pl.BlockSpec((1, W), lambda b, t: (0, 0)),
                  pl.BlockSpec((1, W), lambda b, t: (0, 0)),
                  pl.BlockSpec((1, W), lambda b, t: (0, 0))],
        out_specs=[pl.BlockSpec((1, Tb, W), lambda b, t: (b, t, 0)),
                   pl.BlockSpec((1, 1, W), lambda b, t: (b, 0, 0)),
                   pl.BlockSpec((1, taps - 1, W), lambda b, t: (b, 0, 0))],
        out_shape=[jax.ShapeDtypeStruct((B, T, W), BF16),
                   jax.ShapeDtypeStruct((B, 1, W), F32),
                   jax.ShapeDtypeStruct((B, taps - 1, W), F32)],
        scratch_shapes=[pltpu.VMEM((Tb + SUBLANES, W), F32),
                        pltpu.VMEM((nb, SUBLANES * pitch, bw), F32),
                        pltpu.VMEM((nb, SUBLANES * pitch, bw), F32)],
        compiler_params=_cparams("arbitrary", "arbitrary"),
        name="rglru",
    )(xb, gg, cv0, h0.reshape(B, 1, W), conv_w, row(conv_b), w_r.astype(BF16), w_i.astype(BF16),
      row(b_r), row(b_i), row(lam))
    return y, h1.reshape(B, W), cv1


def _outproj_kernel(ya_ref, yb_ref, wt_ref, wb_ref, x_ref, g_ref, o_ref):
    bb, tt, wa = ya_ref.shape
    wb = yb_ref.shape[2]
    mix = (_dot(ya_ref[...].reshape(bb * tt, wa), wt_ref[...])
           + _dot(yb_ref[...].reshape(bb * tt, wb), wb_ref[...]))
    o_ref[...] = x_ref[...] + g_ref[...] * mix.reshape(bb, tt, mix.shape[-1])


def _outproj(ya, yb, w_out, x, g1):
    B, T, D = x.shape
    wa, wb = ya.shape[2], yb.shape[2]
    assert wa == wb
    bb, tt = _row_tiling(B, T, 1024)
    tn = _pick(D, 1024, LANES)
    return pl.pallas_call(
        _outproj_kernel,
        grid=(B // bb, T // tt, D // tn),
        in_specs=[pl.BlockSpec((bb, tt, wa), lambda b, t, j: (b, t, 0)),
                  pl.BlockSpec((bb, tt, wb), lambda b, t, j: (b, t, 0)),
                  pl.BlockSpec((wa, tn), lambda b, t, j: (0, j)),
                  pl.BlockSpec((wb, tn), lambda b, t, j: (1, j)),
                  pl.BlockSpec((bb, tt, tn), lambda b, t, j: (b, t, j)),
                  pl.BlockSpec((bb, 1, tn), lambda b, t, j: (b, 0, j))],
        out_specs=pl.BlockSpec((bb, tt, tn), lambda b, t, j: (b, t, j)),
        out_shape=jax.ShapeDtypeStruct((B, T, D), F32),
        compiler_params=_cparams("arbitrary", "arbitrary", "arbitrary"),
        name="outproj",
    )(ya, yb, w_out, w_out, x, g1)


def _ffn_kernel(x_ref, ng_ref, sc_ref, sh_ref, g2_ref, *rest, n_col_chunks, final_norm, tiles_per_step, n_tiles):
    w_refs, (gf_ref, o_ref, h_s) = rest[:3 * tiles_per_step], rest[3 * tiles_per_step:]
    j = pl.program_id(2)
    last = pl.num_programs(2) - 1
    bb, tt, D = x_ref.shape
    n_steps = pl.cdiv(n_tiles, tiles_per_step)
    tail = n_tiles - (n_steps - 1) * tiles_per_step
    rc = 2 * SUBLANES

    whole = (0, bb, 0, tt)
    if bb > 1 and bb % 2 == 0:
        halves = [(0, bb // 2, 0, tt), (bb // 2, bb, 0, tt)]
    elif bb == 1 and tt % (2 * rc) == 0:
        halves = [(0, 1, 0, tt // 2), (0, 1, tt // 2, tt)]
    else:
        halves = [whole]

    def row_groups(slab, size):
        b0, b1, r0, r1 = slab
        return [(b, r) for b in range(b0, b1) for r in range(r0, r1, size)]

    def normmod_rows(slab):
        def half(b, r):
            x = x_ref[b, pl.ds(r, SUBLANES), :]
            scale = lax.rsqrt(jnp.mean(x * x, axis=-1, keepdims=True) + EPS)
            return x * scale * (ng_ref[0] * (1.0 + sc_ref[b])) + sh_ref[b]

        for b, r in row_groups(slab, rc):
            h = jnp.concatenate([half(b, r), half(b, r + SUBLANES)], axis=0)
            h_s[pl.ds(b * tt + r, rc), :] = h.astype(BF16)

    def hidden_tiles(count, slab, first):
        b0, b1, r0, r1 = slab
        hb = h_s[pl.ds(b0 * tt + r0, (b1 - b0 - 1) * tt + r1 - r0), :]
        cw = D // n_col_chunks
        acts = []
        for k in range(count):
            gate = _dot(hb, w_refs[3 * k][...])
            up = _dot(hb, w_refs[3 * k + 1][...])
            acts.append((gate * _sigmoid(gate) * up).astype(BF16))
        for c in range(n_col_chunks):
            cols = slice(c * cw, (c + 1) * cw)
            down = _dot(acts[0], w_refs[2][:, cols])
            for k in range(1, count):
                down = down + _dot(acts[k], w_refs[3 * k + 2][:, cols])
            down = down.reshape(b1 - b0, r1 - r0, cw)
            if first:
                o_ref[b0:b1, r0:r1, cols] = down
            else:
                o_ref[b0:b1, r0:r1, cols] += down

    def finish_rows(slab):
        for b, r in row_groups(slab, SUBLANES):
            rows = pl.ds(r, SUBLANES)
            x2 = x_ref[b, rows, :] + g2_ref[b] * o_ref[b, rows, :]
            if final_norm:
                x2 = x2 * lax.rsqrt(jnp.mean(x2 * x2, axis=-1, keepdims=True) + EPS) * gf_ref[0]
            o_ref[b, rows, :] = x2

    def first_step(count, finish):
        for slab in halves:
            normmod_rows(slab)
            hidden_tiles(count, slab, first=True)
            if finish:
                finish_rows(slab)

    def last_step():
        for slab in halves:
            hidden_tiles(tail, slab, first=False)
            finish_rows(slab)

    if n_steps == 1:
        first_step(tail, finish=True)
    else:
        pl.when(j == 0)(lambda: first_step(tiles_per_step, finish=False))
        if n_steps > 2:
            pl.when(jnp.logical_and(j > 0, j < last))(lambda: hidden_tiles(tiles_per_step, whole, first=False))
        pl.when(j == last)(last_step)


def _ffn(x, ng, sc, sh, g2, w_g, w_u, w_down, gf, final_norm):
    B, T, D = x.shape
    F = w_down.shape[0]
    tf = _pick(F, 256, LANES)
    nf = F // tf
    u_off = nf if w_u.shape[1] == 2 * F else 0
    bb, tt = _row_tiling(B, T, 512)
    tps = 2
    w_specs, w_args = [], []
    for k in range(tps):
        tile = lambda j, k=k: jnp.minimum(j * tps + k, nf - 1)
        w_specs += [pl.BlockSpec((D, tf), lambda b, t, j, tile=tile: (0, tile(j))),
                    pl.BlockSpec((D, tf), lambda b, t, j, tile=tile: (0, u_off + tile(j))),
                    pl.BlockSpec((tf, D), lambda b, t, j, tile=tile: (tile(j), 0))]
        w_args += [w_g, w_u, w_down]
    return pl.pallas_call(
        functools.partial(_ffn_kernel, n_col_chunks=max(1, D // 1024), final_norm=final_norm,
                          tiles_per_step=tps, n_tiles=nf),
        grid=(B // bb, T // tt, pl.cdiv(nf, tps)),
        in_specs=[pl.BlockSpec((bb, tt, D), lambda b, t, j: (b, t, 0), pipeline_mode=pl.Buffered(1)),
                  pl.BlockSpec((1, 1, D), lambda b, t, j: (0, 0, 0)),
                  pl.BlockSpec((bb, 1, D), lambda b, t, j: (b, 0, 0)),
                  pl.BlockSpec((bb, 1, D), lambda b, t, j: (b, 0, 0)),
                  pl.BlockSpec((bb, 1, D), lambda b, t, j: (b, 0, 0))] + w_specs + [
                  pl.BlockSpec((1, 1, D), lambda b, t, j: (0, 0, 0))],
        out_specs=pl.BlockSpec((bb, tt, D), lambda b, t, j: (b, t, 0)),
        out_shape=jax.ShapeDtypeStruct((B, T, D), F32),
        scratch_shapes=[pltpu.VMEM((bb * tt, D), BF16)],
        compiler_params=_cparams("arbitrary", "arbitrary", "arbitrary"),
        name="ffn",
    )(x, ng, sc, sh, g2, *w_args, gf)


def _mixer(x, ada, state, p, bf16_weights, cast_jobs):
    sh1, sc1, g1 = ada
    c0, n0, m0, hl0, cv0 = state
    B, T, D = x.shape
    _, H, dk, dv = c0.shape
    W = cv0.shape[2]
    n_qk, n_v = 2 * H * dk, H * dv
    assert n_qk == n_v and (3 * n_v) % LANES == 0
    h1, gates = _normmod_gates(x, p["norm1_g"], sc1, sh1, p["w_in"], 3 * n_v // LANES, p["b_gates"], H)
    h1 = h1.reshape(B * T, D)
    jobs = list(cast_jobs.items())

    def proj(w, blk, n, dt, ep, cs=None):
        name, job = jobs.pop() if jobs else (None, None)
        out = _inproj(h1, w, blk, n, dt, ep, cs, job)
        if job is not None:
            out, bf16_weights[name] = out
        return out.reshape(B, T, n)

    qk = proj(p["w_in"], 0, n_qk, BF16, "colscale", p["qk_scale"])
    v = proj(p["w_in"], 1, n_v, BF16, "none")
    og = proj(p["w_in"], 2, n_v, F32, "sigmoid")
    xb = proj(p["w_xg"], 0, W, F32, "none")
    gg = proj(p["w_xg"], 1, W, F32, "gelu")
    assert not jobs
    ya, c1, n1, m1 = _mlstm(qk, v, og, gates, c0, n0, m0, p["head_norm_g"])
    yb, hl1, cv1 = _rglru(xb, gg, cv0, hl0, p["conv_w"], p["conv_b"],
                          p["w_r"], p["w_i"], p["b_r"], p["b_i"], p["lru_lambda"])
    return _outproj(ya, yb, bf16_weights["w_out"], x, g1), (c1, n1, m1, hl1, cv1)


def kernel(x_prompt, x_sample, c_prompt, c_sample, state_mlstm_C, state_mlstm_n, state_mlstm_m, state_lru_h, state_conv, w_ada, b_ada, norm1_g, norm2_g, w_in, b_gates_a, head_norm_g, conv_w, conv_b, w_r, b_r, w_i, b_i, lru_lambda, w_out, w_gu, w_down, normf_g):
    depth = w_in.shape[0]
    Bp, Tp, D = x_prompt.shape
    Bs = x_sample.shape[0]
    H, dk, dv = state_mlstm_C.shape[2:]
    W = conv_w.shape[2]
    taps = conv_w.shape[1]
    c_x = 2 * H * dk + 2 * H * dv + 2 * H

    xp, xs = x_prompt, x_sample
    p_states, s_states = [], []
    R = Bp + Bs
    Rpad = -(-R // SUBLANES) * SUBLANES
    c_all = jnp.concatenate([c_prompt, c_sample, jnp.zeros((Rpad - R, D), F32)], axis=0)
    gf = normf_g.reshape(1, 1, D)
    qk_scale = jnp.concatenate([jnp.ones((1, H * dk), F32), jnp.full((1, H * dk), dk ** -0.5, F32)], axis=1)
    for l in range(depth):
        w_in_l = w_in[l]
        p = {
            "norm1_g": norm1_g[l].reshape(1, 1, D),
            "w_in": w_in_l.T.astype(BF16),
            "w_xg": w_in_l.T[c_x:].astype(BF16),
            "qk_scale": qk_scale,
            "b_gates": jnp.pad(b_gates_a[l].reshape(1, 2 * H), ((0, 0), (0, LANES - 2 * H))),
            "head_norm_g": head_norm_g[l],
            "conv_w": conv_w[l], "conv_b": conv_b[l],
            "w_r": w_r[l], "w_i": w_i[l], "b_r": b_r[l], "b_i": b_i[l],
            "lru_lambda": lru_lambda[l],
        }
        steps = (Bp * Tp) // _inproj_tile(Bp * Tp)
        planned = {"w_g": _cast_job(w_gu[l], steps, 2, 0), "w_u": _cast_job(w_gu[l], steps, 2, 1),
                   "w_down": _cast_job(w_down[l], steps), "w_out": _cast_job(w_out[l], steps)}
        cast_jobs = {k: j for k, j in planned.items() if j is not None}
        bf16_weights = {}
        if "w_g" not in cast_jobs or "w_u" not in cast_jobs:
            cast_jobs.pop("w_g", None), cast_jobs.pop("w_u", None)
            bf16_weights["w_g"] = bf16_weights["w_u"] = w_gu[l].astype(BF16)
        if "w_down" not in cast_jobs:
            bf16_weights["w_down"] = w_down[l].astype(BF16)
        if "w_out" not in cast_jobs:
            bf16_weights["w_out"] = w_out[l].astype(BF16)
        norm2 = norm2_g[l].reshape(1, 1, D)
        ada = _ada(c_all, w_ada[l], b_ada[l].reshape(1, -1))
        ada_p = [ada[:Bp, i * D:(i + 1) * D].reshape(Bp, 1, D) for i in range(6)]
        ada_s = [ada[Bp:R, i * D:(i + 1) * D].reshape(Bs, 1, D) for i in range(6)]

        zero_state = (jnp.zeros((Bp, H, dk, dv), F32), jnp.zeros((Bp, H, dk), F32),
                      jnp.zeros((Bp, H), F32), jnp.zeros((Bp, W), F32),
                      jnp.zeros((Bp, taps - 1, W), F32))
        cache_state = (state_mlstm_C[l], state_mlstm_n[l], state_mlstm_m[l], state_lru_h[l],
                       state_conv[l])
        last = l == depth - 1
        new = []
        for x, (sh1, sc1, g1, sh2, sc2, g2), st in ((xp, ada_p, zero_state), (xs, ada_s, cache_state)):
            x1, st1 = _mixer(x, (sh1, sc1, g1), st, p, bf16_weights, cast_jobs)
            cast_jobs = {}
            y = _ffn(x1, norm2, sc2, sh2, g2, bf16_weights["w_g"], bf16_weights["w_u"],
                     bf16_weights["w_down"], gf, final_norm=last)
            new.append((y, st1))
        (xp, st_p), (xs, st_s) = new
        p_states.append(st_p)
        s_states.append(st_s)

    stack = lambda sts, i: jnp.stack([s[i] for s in sts])
    return (xp, xs) + tuple(stack(p_states, i) for i in range(5)) + tuple(stack(s_states, i) for i in range(5))
```

```python
import functools

import jax
import jax.numpy as jnp
from jax import lax
from jax.experimental import pallas as pl
from jax.experimental.pallas import tpu as pltpu

F32 = jnp.float32
BF16 = jnp.bfloat16
EPS = 1e-6
LRU_C = 8.0
LANES = 128
SUBLANES = 8
VMEM_LIMIT_BYTES = 60 * 1024 * 1024


def _cparams(*sem):
    return pltpu.CompilerParams(dimension_semantics=sem, vmem_limit_bytes=VMEM_LIMIT_BYTES)


def _pick(total, target, quantum):
    if total <= target:
        return total
    best = None
    for cand in range(quantum, target + 1, quantum):
        if total % cand == 0:
            best = cand
    assert best is not None, (total, target, quantum)
    return best


def _row_tiling(B, T, target_rows):
    if T >= target_rows:
        return 1, _pick(T, target_rows, 16)
    bb = _pick(B, max(1, target_rows // T), 1)
    return bb, T


def _mlstm_chunk(T):
    return _pick(T, LANES, 16)


def _log_sigmoid(x):
    return jnp.minimum(x, 0.0) - jnp.log1p(jnp.exp(-jnp.abs(x)))


def _sigmoid(x):
    return 0.5 * (1.0 + jnp.tanh(0.5 * x))


def _gelu_tanh(x):
    c = 0.7978845608028654
    return 0.5 * x * (1.0 + jnp.tanh(c * (x + 0.044715 * (x * x * x))))


def _dot(a, b):
    return jnp.dot(a, b, preferred_element_type=F32)


def _dot_nt(a, b):
    return lax.dot_general(a, b, (((1,), (1,)), ((), ())), preferred_element_type=F32)


def _dot_tn(a, b):
    return lax.dot_general(a, b, (((0,), (0,)), ((), ())), preferred_element_type=F32)


def _split_bf16(x, terms):
    out = []
    for _ in range(terms - 1):
        t = x.astype(BF16)
        out.append(t)
        x = x - t.astype(F32)
    out.append(x.astype(BF16))
    return out


def _ada_kernel(c_ref, w_ref, b_ref, o_ref):
    c = c_ref[...]
    s = (c * _sigmoid(c)).astype(BF16)
    o_ref[...] = _dot(s, w_ref[...].astype(BF16)) + b_ref[...]


def _ada(c, w, b):
    R, D = c.shape
    N = w.shape[1]
    tn = _pick(N, 512, LANES)
    return pl.pallas_call(
        _ada_kernel,
        grid=(N // tn,),
        in_specs=[pl.BlockSpec((R, D), lambda j: (0, 0)),
                  pl.BlockSpec((D, tn), lambda j: (0, j)),
                  pl.BlockSpec((1, tn), lambda j: (0, j))],
        out_specs=pl.BlockSpec((R, tn), lambda j: (0, j)),
        out_shape=jax.ShapeDtypeStruct((R, N), F32),
        compiler_params=_cparams("arbitrary"),
        name="ada",
    )(c, w, b)


def _normmod(x, geff, sh):
    r = lax.rsqrt(jnp.mean(x * x, axis=-1, keepdims=True) + EPS)
    return x * r * geff + sh


def _normmod_gates_kernel(x_ref, g_ref, sc_ref, sh_ref, wg_ref, bg_ref, h_ref, gr_ref, gt_s, *, n_heads, chunk):
    bb, tt, D = x_ref.shape
    H = n_heads
    hb = _normmod(x_ref[...], g_ref[...] * (1.0 + sc_ref[...]), sh_ref[...]).astype(BF16)
    h_ref[...] = hb
    pre = _dot_nt(hb.reshape(bb * tt, D), wg_ref[...]) + bg_ref[...]
    lane = lax.broadcasted_iota(jnp.int32, pre.shape, 1)
    gates = jnp.where(lane < H, pre, _log_sigmoid(pre))
    rows, lanes_out = gt_s.shape[2], gt_s.shape[3]
    if lanes_out > chunk:
        gt_s[...] = jnp.zeros_like(gt_s)
    blocks = [(i, c) for i in range(bb) for c in range(tt // chunk)]
    for i, c in blocks:
        r0 = i * tt + c * chunk
        gt_s[i, c, :, 0:chunk] = gates[r0:r0 + chunk, :].T[:rows, :]
    ig = jnp.concatenate([gt_s[i, c, 0:H, :] for i, c in blocks], axis=0)
    lf = jnp.concatenate([gt_s[i, c, H:2 * H, :] for i, c in blocks], axis=0)
    lane_t = lax.broadcasted_iota(jnp.int32, lf.shape, 1)
    bcum = _lane_scan(lf, jnp.add, lane_t, 0.0)
    a = ig - bcum
    cm = _lane_scan(a, jnp.maximum, lane_t, -jnp.inf)
    if gr_ref.shape[2] > 3 * H:
        gr_ref[...] = jnp.zeros_like(gr_ref)
    for k, (i, c) in enumerate(blocks):
        for r, val in enumerate((a, cm, bcum)):
            gr_ref[i, c, r * H:(r + 1) * H, :] = val[k * H:(k + 1) * H, :]


def _normmod_gates(x, g, sc, sh, wg, gate_blk, bg, n_heads):
    B, T, D = x.shape
    L = _mlstm_chunk(T)
    bb, tt = _row_tiling(B, T, 512)
    assert tt % L == 0
    GR = -(-2 * n_heads // SUBLANES) * SUBLANES
    GR3 = -(-3 * n_heads // SUBLANES) * SUBLANES
    LP = max(L, LANES)
    return pl.pallas_call(
        functools.partial(_normmod_gates_kernel, n_heads=n_heads, chunk=L),
        grid=(B // bb, T // tt),
        in_specs=[pl.BlockSpec((bb, tt, D), lambda b, t: (b, t, 0)),
                  pl.BlockSpec((1, 1, D), lambda b, t: (0, 0, 0)),
                  pl.BlockSpec((bb, 1, D), lambda b, t: (b, 0, 0)),
                  pl.BlockSpec((bb, 1, D), lambda b, t: (b, 0, 0)),
                  pl.BlockSpec((LANES, D), lambda b, t: (gate_blk, 0)),
                  pl.BlockSpec((1, LANES), lambda b, t: (0, 0))],
        out_specs=[pl.BlockSpec((bb, tt, D), lambda b, t: (b, t, 0)),
                   pl.BlockSpec((bb, tt // L, GR3, LP), lambda b, t: (b, t, 0, 0))],
        out_shape=[jax.ShapeDtypeStruct((B, T, D), BF16),
                   jax.ShapeDtypeStruct((B, T // L, GR3, LP), F32)],
        scratch_shapes=[pltpu.VMEM((bb, tt // L, GR, LP), F32)],
        compiler_params=_cparams("arbitrary", "arbitrary"),
        name="normmod_gates",
    )(x, g, sc, sh, wg, bg)


def _inproj_kernel(a_ref, w_ref, *rest, epilogue, with_cast):
    rest = list(rest)
    if with_cast:
        cast_out = rest.pop()
        o_ref = rest.pop()
        cast_out[...] = rest.pop()[...].astype(BF16)
    else:
        o_ref = rest.pop()
    acc = _dot_nt(a_ref[...], w_ref[...])
    if epilogue == "colscale":
        acc = acc * rest[0][...]
    elif epilogue == "sigmoid":
        acc = _sigmoid(acc)
    elif epilogue == "gelu":
        acc = _gelu_tanh(acc)
    o_ref[...] = acc.astype(o_ref.dtype)


def _inproj_tile(M):
    return _pick(M, 512, 16)


def _cast_job(w, steps, col_blocks=1, col_block=0):
    R, C = w.shape
    Cb = C // col_blocks
    if C % col_blocks or (col_blocks > 1 and Cb % LANES):
        return None
    if R % (steps * 2 * SUBLANES) == 0:
        return w, (R // steps, Cb), (lambda i: (i, col_block)), (lambda i: (i, 0)), (R, Cb)
    if col_blocks == 1 and C % (steps * LANES) == 0:
        return w, (R, C // steps), (lambda i: (0, i)), (lambda i: (0, i)), (R, C)
    return None


def _inproj(a, wt, row_block, N, out_dtype, epilogue, colscale=None, cast=None):
    M, K = a.shape
    tm = _inproj_tile(M)
    in_specs = [pl.BlockSpec((tm, K), lambda i: (i, 0)),
                pl.BlockSpec((N, K), lambda i: (row_block, 0), pipeline_mode=pl.Buffered(1))]
    args = [a, wt]
    if epilogue == "colscale":
        in_specs.append(pl.BlockSpec((1, N), lambda i: (0, 0)))
        args.append(colscale)
    out_specs = [pl.BlockSpec((tm, N), lambda i: (i, 0))]
    out_shape = [jax.ShapeDtypeStruct((M, N), out_dtype)]
    if cast is not None:
        src, blk, src_map, dst_map, shape = cast
        in_specs.append(pl.BlockSpec(blk, src_map))
        args.append(src)
        out_specs.append(pl.BlockSpec(blk, dst_map))
        out_shape.append(jax.ShapeDtypeStruct(shape, BF16))
    outs = pl.pallas_call(
        functools.partial(_inproj_kernel, epilogue=epilogue, with_cast=cast is not None),
        grid=(M // tm,),
        in_specs=in_specs,
        out_specs=out_specs,
        out_shape=out_shape,
        compiler_params=_cparams("arbitrary"),
        name="inproj_" + epilogue,
    )(*args)
    return outs if cast is not None else outs[0]


def _lane_scan(x, op, lane, identity):
    d = 1
    while d < x.shape[1]:
        x = op(x, jnp.where(lane >= d, pltpu.roll(x, d, axis=1), identity))
        d *= 2
    return x


def _mlstm_kernel(qk_ref, v_ref, og_ref, gr_ref, c0_ref, n0_ref, m0_ref, hg_ref,
                  y_ref, c_ref, n_ref, m_ref, cn_s, m_s, *, chunk, n_heads, dk, dv, head_group):
    tb = pl.program_id(1)
    L, H = chunk, n_heads
    Tb = qk_ref.shape[1]
    nc = Tb // L
    LP = gr_ref.shape[3]
    assert dv % LANES == 0 and L <= LANES
    rep = lambda x, n: jnp.concatenate([x] * n, axis=1)

    eye_k = (lax.broadcasted_iota(jnp.int32, (dk, dk), 0) == lax.broadcasted_iota(jnp.int32, (dk, dk), 1))

    @pl.when(tb == 0)
    def _():
        for h in range(H):
            cn_s[h, :, 0:dv] = c0_ref[0, h]
            n_col = jnp.sum(jnp.where(eye_k, n0_ref[0, h:h + 1, :], 0.0), axis=1, keepdims=True)
            cn_s[h, :, dv:dv + LANES] = jnp.broadcast_to(n_col, (dk, LANES))
        m_s[...] = jnp.broadcast_to(m0_ref[0], m_s.shape)

    row = lax.broadcasted_iota(jnp.int32, (L, L), 0)
    col = lax.broadcasted_iota(jnp.int32, (L, L), 1)
    tril = col <= row
    nr = 3 * H
    spread = ((lax.broadcasted_iota(jnp.int32, (3 * nr, nr * LANES), 1) // LANES)
              == (lax.broadcasted_iota(jnp.int32, (3 * nr, nr * LANES), 0) % nr)).astype(BF16)
    ones_2l = jnp.ones((2 * L, LANES), BF16)
    mean_2v = jnp.full((2 * dv, LANES), 1.0 / dv, BF16)
    assert dv & (dv - 1) == 0

    def chunk_step(c, carry):
        sl = pl.ds(pl.multiple_of(c * L, L), L)
        rows = gr_ref[0, tb * nc + c][0:nr]
        a = rows[0:H]
        cols = _dot_tn(jnp.concatenate(_split_bf16(rows, 3), axis=0), spread)
        blk = lambda r: cols[0:L, r * LANES:(r + 1) * LANES]
        m_all = m_s[...]
        nv = dv // LANES
        m_new = {}
        for h0 in range(0, H, head_group):
            heads = range(h0, min(h0 + head_group, H))
            qb = {h: qk_ref[0, sl, h * dk:(h + 1) * dk] for h in heads}
            kb = {h: qk_ref[0, sl, (H + h) * dk:(H + h + 1) * dk] for h in heads}
            vb = {h: v_ref[0, sl, h * dv:(h + 1) * dv] for h in heads}
            cn = {h: cn_s[h] for h in heads}
            qk = {h: _dot_nt(qb[h], kb[h]) for h in heads}
            qc = {h: _dot(qb[h], cn[h].astype(BF16)) for h in heads}

            g, s_inter, sv, rs = {}, {}, {}, {}
            for h in heads:
                g[h] = jnp.maximum(m_all[h:h + 1, :], blk(H + h))
                s_inter[h] = jnp.exp(m_all[h:h + 1, :] - g[h])
                w = jnp.where(tril, jnp.exp(jnp.minimum(a[h:h + 1, 0:L] - g[h][:, 0:L], 0.0)), 0.0)
                s2 = _split_bf16(qk[h] * w, 2)
                sv[h] = _dot(s2[0], vb[h])
                rs[h] = _dot(jnp.concatenate(s2, axis=1), ones_2l)

            hh, ms = {}, {}
            for h in heads:
                num = rep(s_inter[h], nv) * qc[h][:, 0:dv] + sv[h]
                den = s_inter[h] * qc[h][:, dv:dv + LANES] + rs[h]
                inv = 1.0 / jnp.maximum(jnp.abs(den), jnp.exp(-(blk(2 * H + h) + g[h])))
                hh[h] = num * rep(inv, nv)
                ms[h] = _dot(jnp.concatenate(_split_bf16(hh[h] * hh[h], 2), axis=1), mean_2v)

            for h in heads:
                hn = hh[h] * rep(lax.rsqrt(ms[h] + EPS), nv) * hg_ref[h:h + 1, :]
                y_ref[0, sl, h * dv:(h + 1) * dv] = (hn * og_ref[0, sl, h * dv:(h + 1) * dv]).astype(y_ref.dtype)
                g_last = g[h][L - 1:L, :]
                w_last = jnp.exp(blk(h) - g_last)
                upd_rhs = jnp.concatenate([rep(w_last, nv) * vb[h].astype(F32), w_last], axis=1)
                decay = jnp.exp(m_all[h:h + 1, :] - g_last)
                cn_s[h] = rep(decay, nv + 1) * cn[h] + _dot_tn(kb[h], upd_rhs.astype(BF16))
                m_new[h] = blk(2 * H + h)[L - 1:L, :] + g_last
        m_s[...] = jnp.concatenate([m_new[h] for h in range(H)], axis=0)
        return carry

    lax.fori_loop(0, nc, chunk_step, 0)

    @pl.when(tb == pl.num_programs(1) - 1)
    def _():
        for h in range(H):
            c_ref[0, h] = cn_s[h, :, 0:dv]
            n_ref[0, h:h + 1, :] = jnp.sum(jnp.where(eye_k, cn_s[h, :, dv:dv + 1], 0.0), axis=0, keepdims=True)
        m_ref[0] = m_s[...]


def _mlstm(qk, v, og, gates, c0, n0, m0, head_g):
    B, T, _ = qk.shape
    _, H, dk, dv = c0.shape
    L = _mlstm_chunk(T)
    Tb = _pick(T, 1024, L)
    _, NC, GR, LP = gates.shape
    y, c1, n1, m1 = pl.pallas_call(
        functools.partial(_mlstm_kernel, chunk=L, n_heads=H, dk=dk, dv=dv, head_group=4),
        grid=(B, T // Tb),
        in_specs=[pl.BlockSpec((1, Tb, 2 * H * dk), lambda b, t: (b, t, 0)),
                  pl.BlockSpec((1, Tb, H * dv), lambda b, t: (b, t, 0)),
                  pl.BlockSpec((1, Tb, H * dv), lambda b, t: (b, t, 0)),
                  pl.BlockSpec((1, NC, GR, LP), lambda b, t: (b, 0, 0, 0)),
                  pl.BlockSpec((1, H, dk, dv), lambda b, t: (b, 0, 0, 0)),
                  pl.BlockSpec((1, H, dk), lambda b, t: (b, 0, 0)),
                  pl.BlockSpec((1, H, 1), lambda b, t: (b, 0, 0)),
                  pl.BlockSpec((H, dv), lambda b, t: (0, 0))],
        out_specs=[pl.BlockSpec((1, Tb, H * dv), lambda b, t: (b, t, 0)),
                   pl.BlockSpec((1, H, dk, dv), lambda b, t: (b, 0, 0, 0)),
                   pl.BlockSpec((1, H, dk), lambda b, t: (b, 0, 0)),
                   pl.BlockSpec((1, H, LANES), lambda b, t: (b, 0, 0))],
        out_shape=[jax.ShapeDtypeStruct((B, T, H * dv), BF16),
                   jax.ShapeDtypeStruct((B, H, dk, dv), F32),
                   jax.ShapeDtypeStruct((B, H, dk), F32),
                   jax.ShapeDtypeStruct((B, H, LANES), F32)],
        scratch_shapes=[pltpu.VMEM((H, dk, dv + LANES), F32),
                        pltpu.VMEM((H, LANES), F32)],
        compiler_params=_cparams("arbitrary", "arbitrary"),
        name="mlstm",
    )(qk, v, og, gates, c0, n0, m0.reshape(B, H, 1), head_g)
    return y, c1, n1, m1[:, :, 0]


def _scan_pitch(rows):
    quads = -(-(-(-rows // SUBLANES)) // 4)
    return 4 * (quads + 1 - quads % 2)


def _rglru_kernel(xb_ref, gg_ref, cv0_ref, h0_ref, cw_ref, cb_ref, wr_ref, wi_ref, br_ref, bi_ref,
                  lam_ref, y_ref, h_ref, cv_ref, xext, a_s, u_s, *, pitch, group):
    tb = pl.program_id(1)
    Tb, W = xb_ref.shape[1], xb_ref.shape[2]
    nb, bw = wr_ref.shape[0], wr_ref.shape[1]
    taps = cw_ref.shape[0]
    rows = a_s.shape[1]
    pad = SUBLANES

    @pl.when(tb == 0)
    def _():
        h_ref[...] = h0_ref[...]
        cv_ref[...] = cv0_ref[...]

    x = xb_ref[0]
    xext[pl.ds(pad, Tb), :] = x
    xext[pl.ds(pad - (taps - 1), taps - 1), :] = cv_ref[0]
    xc = cb_ref[...] + cw_ref[taps - 1:taps, :] * x
    for j in range(taps - 1):
        xc = xc + cw_ref[j:j + 1, :] * xext[pl.ds(pad - (taps - 1) + j, Tb), :]
    cv_ref[0] = xext[pl.ds(pad + Tb - (taps - 1), taps - 1), :]

    log_sig_lam = _log_sigmoid(lam_ref[...])
    for n in range(nb):
        cols = slice(n * bw, (n + 1) * bw)
        xn = xc[:, cols]
        xnb = xn.astype(BF16)
        r = _sigmoid(_dot(xnb, wr_ref[n]) + br_ref[:, cols])
        i = _sigmoid(_dot(xnb, wi_ref[n]) + bi_ref[:, cols])
        log_a = LRU_C * r * log_sig_lam[:, cols]
        a = jnp.exp(log_a)
        a_s[n, 0:Tb, :] = a
        z = -jnp.tanh(log_a) * (1.0 + a * a)
        u_s[n, 0:Tb, :] = jnp.where(z > 0.0, z * lax.rsqrt(z), 0.0) * (i * xn)
        if rows > Tb:
            a_s[n, Tb:rows, :] = jnp.ones((rows - Tb, bw), F32)
            u_s[n, Tb:rows, :] = jnp.zeros((rows - Tb, bw), F32)

    for n0 in range(0, nb, group):
        slabs = list(range(n0, min(n0 + group, nb)))

        def pass1(i, carry):
            idx = pl.ds(i, SUBLANES, stride=pitch)
            out = []
            for (hz, ac), n in zip(carry, slabs):
                av = a_s[n, idx, :]
                hz = av * hz + u_s[n, idx, :]
                ac = av * ac
                u_s[n, idx, :] = hz
                a_s[n, idx, :] = ac
                out.append((hz, ac))
            return tuple(out)

        init = tuple((jnp.zeros((SUBLANES, bw), F32), jnp.ones((SUBLANES, bw), F32)) for _ in slabs)
        ends = lax.fori_loop(0, pitch, pass1, init, unroll=2)

        starts = []
        for (hz, ac), n in zip(ends, slabs):
            cols = slice(n * bw, (n + 1) * bw)
            h = h_ref[0, :, cols]
            per_block = []
            for j in range(SUBLANES):
                per_block.append(h)
                h = hz[j:j + 1, :] + ac[j:j + 1, :] * h
            h_ref[0, :, cols] = h
            starts.append(jnp.concatenate(per_block, axis=0))

        def pass2(i, carry):
            idx = pl.ds(i, SUBLANES, stride=pitch)
            for st, n in zip(starts, slabs):
                u_s[n, idx, :] = u_s[n, idx, :] + a_s[n, idx, :] * st
            return carry

        lax.fori_loop(0, pitch, pass2, 0, unroll=2)

    for n in range(nb):
        cols = slice(n * bw, (n + 1) * bw)
        y_ref[0, :, cols] = (u_s[n, 0:Tb, :] * gg_ref[0, :, cols]).astype(y_ref.dtype)


def _rglru(xb, gg, cv0, h0, conv_w, conv_b, w_r, w_i, b_r, b_i, lam):
    B, T, W = xb.shape
    taps = conv_w.shape[0]
    nb, bw = w_r.shape[0], w_r.shape[1]
    assert T >= taps - 1 and bw == LANES and nb * bw == W
    Tb = _pick(T, 512, SUBLANES)
    pitch = _scan_pitch(Tb)
    row = lambda a: a.reshape(1, W)
    y, h1, cv1 = pl.pallas_call(
        functools.partial(_rglru_kernel, pitch=pitch, group=8),
        grid=(B, T // Tb),
        in_specs=[pl.BlockSpec((1, Tb, W), lambda b, t: (b, t, 0)),
                  pl.BlockSpec((1, Tb, W), lambda b, t: (b, t, 0)),
                  pl.BlockSpec((1, taps - 1, W), lambda b, t: (b, 0, 0)),
                  pl.BlockSpec((1, 1, W), lambda b, t: (b, 0, 0)),
                  pl.BlockSpec((taps, W), lambda b, t: (0, 0)),
                  pl.BlockSpec((1, W), lambda b, t: (0, 0)),
                  pl.BlockSpec((nb, bw, bw), lambda b, t: (0, 0, 0)),
                  pl.BlockSpec((nb, bw, bw), lambda b, t: (0, 0, 0)),
                  pl.BlockSpec((1, W), lambda b, t: (0, 0)),
                  pl.BlockSpec((1, W), lambda b, t: (0, 0)),
                  pl.BlockSpec((1, W), lambda b, t: (0, 0))],
        out_specs=[pl.BlockSpec((1, Tb, W), lambda b, t: (b, t, 0)),
                   pl.BlockSpec((1, 1, W), lambda b, t: (b, 0, 0)),
                   pl.BlockSpec((1, taps - 1, W), lambda b, t: (b, 0, 0))],
        out_shape=[jax.ShapeDtypeStruct((B, T, W), BF16),
                   jax.ShapeDtypeStruct((B, 1, W), F32),
                   jax.ShapeDtypeStruct((B, taps - 1, W), F32)],
        scratch_shapes=[pltpu.VMEM((Tb + SUBLANES, W), F32),
                        pltpu.VMEM((nb, SUBLANES * pitch, bw), F32),
                        pltpu.VMEM((nb, SUBLANES * pitch, bw), F32)],
        compiler_params=_cparams("arbitrary", "arbitrary"),
        name="rglru",
    )(xb, gg, cv0, h0.reshape(B, 1, W), conv_w, row(conv_b), w_r.astype(BF16), w_i.astype(BF16),
      row(b_r), row(b_i), row(lam))
    return y, h1.reshape(B, W), cv1


def _outproj_kernel(ya_ref, yb_ref, wt_ref, wb_ref, x_ref, g_ref, o_ref):
    bb, tt, wa = ya_ref.shape
    wb = yb_ref.shape[2]
    mix = (_dot(ya_ref[...].reshape(bb * tt, wa), wt_ref[...])
           + _dot(yb_ref[...].reshape(bb * tt, wb), wb_ref[...]))
    o_ref[...] = x_ref[...] + g_ref[...] * mix.reshape(bb, tt, mix.shape[-1])


def _outproj(ya, yb, w_out, x, g1):
    B, T, D = x.shape
    wa, wb = ya.shape[2], yb.shape[2]
    assert wa == wb
    bb, tt = _row_tiling(B, T, 1024)
    tn = _pick(D, 1024, LANES)
    return pl.pallas_call(
        _outproj_kernel,
        grid=(B // bb, T // tt, D // tn),
        in_specs=[pl.BlockSpec((bb, tt, wa), lambda b, t, j: (b, t, 0)),
                  pl.BlockSpec((bb, tt, wb), lambda b, t, j: (b, t, 0)),
                  pl.BlockSpec((wa, tn), lambda b, t, j: (0, j)),
                  pl.BlockSpec((wb, tn), lambda b, t, j: (1, j)),
                  pl.BlockSpec((bb, tt, tn), lambda b, t, j: (b, t, j)),
                  pl.BlockSpec((bb, 1, tn), lambda b, t, j: (b, 0, j))],
        out_specs=pl.BlockSpec((bb, tt, tn), lambda b, t, j: (b, t, j)),
        out_shape=jax.ShapeDtypeStruct((B, T, D), F32),
        compiler_params=_cparams("arbitrary", "arbitrary", "arbitrary"),
        name="outproj",
    )(ya, yb, w_out, w_out, x, g1)


def _ffn_kernel(x_ref, ng_ref, sc_ref, sh_ref, g2_ref, *rest, n_col_chunks, final_norm, tiles_per_step, n_tiles):
    w_refs, (gf_ref, o_ref, h_s) = rest[:3 * tiles_per_step], rest[3 * tiles_per_step:]
    j = pl.program_id(2)
    last = pl.num_programs(2) - 1
    bb, tt, D = x_ref.shape
    n_steps = pl.cdiv(n_tiles, tiles_per_step)
    tail = n_tiles - (n_steps - 1) * tiles_per_step
    rc = 2 * SUBLANES

    whole = (0, bb, 0, tt)
    if bb > 1 and bb % 2 == 0:
        halves = [(0, bb // 2, 0, tt), (bb // 2, bb, 0, tt)]
    elif bb == 1 and tt % (2 * rc) == 0:
        halves = [(0, 1, 0, tt // 2), (0, 1, tt // 2, tt)]
    else:
        halves = [whole]

    def row_groups(slab, size):
        b0, b1, r0, r1 = slab
        return [(b, r) for b in range(b0, b1) for r in range(r0, r1, size)]

    def normmod_rows(slab):
        def half(b, r):
            x = x_ref[b, pl.ds(r, SUBLANES), :]
            scale = lax.rsqrt(jnp.mean(x * x, axis=-1, keepdims=True) + EPS)
            return x * scale * (ng_ref[0] * (1.0 + sc_ref[b])) + sh_ref[b]

        for b, r in row_groups(slab, rc):
            h = jnp.concatenate([half(b, r), half(b, r + SUBLANES)], axis=0)
            h_s[pl.ds(b * tt + r, rc), :] = h.astype(BF16)

    def hidden_tiles(count, slab, first):
        b0, b1, r0, r1 = slab
        hb = h_s[pl.ds(b0 * tt + r0, (b1 - b0 - 1) * tt + r1 - r0), :]
        cw = D // n_col_chunks
        acts = []
        for k in range(count):
            gate = _dot(hb, w_refs[3 * k][...])
            up = _dot(hb, w_refs[3 * k + 1][...])
            acts.append((gate * _sigmoid(gate) * up).astype(BF16))
        for c in range(n_col_chunks):
            cols = slice(c * cw, (c + 1) * cw)
            down = _dot(acts[0], w_refs[2][:, cols])
            for k in range(1, count):
                down = down + _dot(acts[k], w_refs[3 * k + 2][:, cols])
            down = down.reshape(b1 - b0, r1 - r0, cw)
            if first:
                o_ref[b0:b1, r0:r1, cols] = down
            else:
                o_ref[b0:b1, r0:r1, cols] += down

    def finish_rows(slab):
        for b, r in row_groups(slab, SUBLANES):
            rows = pl.ds(r, SUBLANES)
            x2 = x_ref[b, rows, :] + g2_ref[b] * o_ref[b, rows, :]
            if final_norm:
                x2 = x2 * lax.rsqrt(jnp.mean(x2 * x2, axis=-1, keepdims=True) + EPS) * gf_ref[0]
            o_ref[b, rows, :] = x2

    def first_step(count, finish):
        for slab in halves:
            normmod_rows(slab)
            hidden_tiles(count, slab, first=True)
            if finish:
                finish_rows(slab)

    def last_step():
        for slab in halves:
            hidden_tiles(tail, slab, first=False)
            finish_rows(slab)

    if n_steps == 1:
        first_step(tail, finish=True)
    else:
        pl.when(j == 0)(lambda: first_step(tiles_per_step, finish=False))
        if n_steps > 2:
            pl.when(jnp.logical_and(j > 0, j < last))(lambda: hidden_tiles(tiles_per_step, whole, first=False))
        pl.when(j == last)(last_step)


def _ffn(x, ng, sc, sh, g2, w_g, w_u, w_down, gf, final_norm):
    B, T, D = x.shape
    F = w_down.shape[0]
    tf = _pick(F, 256, LANES)
    nf = F // tf
    u_off = nf if w_u.shape[1] == 2 * F else 0
    bb, tt = _row_tiling(B, T, 512)
    tps = 2
    w_specs, w_args = [], []
    for k in range(tps):
        tile = lambda j, k=k: jnp.minimum(j * tps + k, nf - 1)
        w_specs += [pl.BlockSpec((D, tf), lambda b, t, j, tile=tile: (0, tile(j))),
                    pl.BlockSpec((D, tf), lambda b, t, j, tile=tile: (0, u_off + tile(j))),
                    pl.BlockSpec((tf, D), lambda b, t, j, tile=tile: (tile(j), 0))]
        w_args += [w_g, w_u, w_down]
    return pl.pallas_call(
        functools.partial(_ffn_kernel, n_col_chunks=max(1, D // 1024), final_norm=final_norm,
                          tiles_per_step=tps, n_tiles=nf),
        grid=(B // bb, T // tt, pl.cdiv(nf, tps)),
        in_specs=[pl.BlockSpec((bb, tt, D), lambda b, t, j: (b, t, 0), pipeline_mode=pl.Buffered(1)),
                  pl.BlockSpec((1, 1, D), lambda b, t, j: (0, 0, 0)),
                  pl.BlockSpec((bb, 1, D), lambda b, t, j: (b, 0, 0)),
                  pl.BlockSpec((bb, 1, D), lambda b, t, j: (b, 0, 0)),
                  pl.BlockSpec((bb, 1, D), lambda b, t, j: (b, 0, 0))] + w_specs + [
                  pl.BlockSpec((1, 1, D), lambda b, t, j: (0, 0, 0))],
        out_specs=pl.BlockSpec((bb, tt, D), lambda b, t, j: (b, t, 0)),
        out_shape=jax.ShapeDtypeStruct((B, T, D), F32),
        scratch_shapes=[pltpu.VMEM((bb * tt, D), BF16)],
        compiler_params=_cparams("arbitrary", "arbitrary", "arbitrary"),
        name="ffn",
    )(x, ng, sc, sh, g2, *w_args, gf)


def _mixer(x, ada, state, p, bf16_weights, cast_jobs):
    sh1, sc1, g1 = ada
    c0, n0, m0, hl0, cv0 = state
    B, T, D = x.shape
    _, H, dk, dv = c0.shape
    W = cv0.shape[2]
    n_qk, n_v = 2 * H * dk, H * dv
    assert n_qk == n_v and (3 * n_v) % LANES == 0
    h1, gates = _normmod_gates(x, p["norm1_g"], sc1, sh1, p["w_in"], 3 * n_v // LANES, p["b_gates"], H)
    h1 = h1.reshape(B * T, D)
    jobs = list(cast_jobs.items())

    def proj(w, blk, n, dt, ep, cs=None):
        name, job = jobs.pop() if jobs else (None, None)
        out = _inproj(h1, w, blk, n, dt, ep, cs, job)
        if job is not None:
            out, bf16_weights[name] = out
        return out.reshape(B, T, n)

    qk = proj(p["w_in"], 0, n_qk, BF16, "colscale", p["qk_scale"])
    v = proj(p["w_in"], 1, n_v, BF16, "none")
    og = proj(p["w_in"], 2, n_v, F32, "sigmoid")
    xb = proj(p["w_xg"], 0, W, F32, "none")
    gg = proj(p["w_xg"], 1, W, F32, "gelu")
    assert not jobs
    ya, c1, n1, m1 = _mlstm(qk, v, og, gates, c0, n0, m0, p["head_norm_g"])
    yb, hl1, cv1 = _rglru(xb, gg, cv0, hl0, p["conv_w"], p["conv_b"],
                          p["w_r"], p["w_i"], p["b_r"], p["b_i"], p["lru_lambda"])
    return _outproj(ya, yb, bf16_weights["w_out"], x, g1), (c1, n1, m1, hl1, cv1)


def kernel(x_prompt, x_sample, c_prompt, c_sample, state_mlstm_C, state_mlstm_n, state_mlstm_m, state_lru_h, state_conv, w_ada, b_ada, norm1_g, norm2_g, w_in, b_gates_a, head_norm_g, conv_w, conv_b, w_r, b_r, w_i, b_i, lru_lambda, w_out, w_gu, w_down, normf_g):
    depth = w_in.shape[0]
    Bp, Tp, D = x_prompt.shape
    Bs = x_sample.shape[0]
    H, dk, dv = state_mlstm_C.shape[2:]
    W = conv_w.shape[2]
    taps = conv_w.shape[1]
    c_x = 2 * H * dk + 2 * H * dv + 2 * H

    xp, xs = x_prompt, x_sample
    p_states, s_states = [], []
    R = Bp + Bs
    Rpad = -(-R // SUBLANES) * SUBLANES
    c_all = jnp.concatenate([c_prompt, c_sample, jnp.zeros((Rpad - R, D), F32)], axis=0)
    gf = normf_g.reshape(1, 1, D)
    qk_scale = jnp.concatenate([jnp.ones((1, H * dk), F32), jnp.full((1, H * dk), dk ** -0.5, F32)], axis=1)
    for l in range(depth):
        w_in_l = w_in[l]
        p = {
            "norm1_g": norm1_g[l].reshape(1, 1, D),
            "w_in": w_in_l.T.astype(BF16),
            "w_xg": w_in_l.T[c_x:].astype(BF16),
            "qk_scale": qk_scale,
            "b_gates": jnp.pad(b_gates_a[l].reshape(1, 2 * H), ((0, 0), (0, LANES - 2 * H))),
            "head_norm_g": head_norm_g[l],
            "conv_w": conv_w[l], "conv_b": conv_b[l],
            "w_r": w_r[l], "w_i": w_i[l], "b_r": b_r[l], "b_i": b_i[l],
            "lru_lambda": lru_lambda[l],
        }
        steps = (Bp * Tp) // _inproj_tile(Bp * Tp)
        planned = {"w_g": _cast_job(w_gu[l], steps, 2, 0), "w_u": _cast_job(w_gu[l], steps, 2, 1),
                   "w_down": _cast_job(w_down[l], steps), "w_out": _cast_job(w_out[l], steps)}
        cast_jobs = {k: j for k, j in planned.items() if j is not None}
        bf16_weights = {}
        if "w_g" not in cast_jobs or "w_u" not in cast_jobs:
            cast_jobs.pop("w_g", None), cast_jobs.pop("w_u", None)
            bf16_weights["w_g"] = bf16_weights["w_u"] = w_gu[l].astype(BF16)
        if "w_down" not in cast_jobs:
            bf16_weights["w_down"] = w_down[l].astype(BF16)
        if "w_out" not in cast_jobs:
            bf16_weights["w_out"] = w_out[l].astype(BF16)
        norm2 = norm2_g[l].reshape(1, 1, D)
        ada = _ada(c_all, w_ada[l], b_ada[l].reshape(1, -1))
        ada_p = [ada[:Bp, i * D:(i + 1) * D].reshape(Bp, 1, D) for i in range(6)]
        ada_s = [ada[Bp:R, i * D:(i + 1) * D].reshape(Bs, 1, D) for i in range(6)]

        zero_state = (jnp.zeros((Bp, H, dk, dv), F32), jnp.zeros((Bp, H, dk), F32),
                      jnp.zeros((Bp, H), F32), jnp.zeros((Bp, W), F32),
                      jnp.zeros((Bp, taps - 1, W), F32))
        cache_state = (state_mlstm_C[l], state_mlstm_n[l], state_mlstm_m[l], state_lru_h[l],
                       state_conv[l])
        last = l == depth - 1
        new = []
        for x, (sh1, sc1, g1, sh2, sc2, g2), st in ((xp, ada_p, zero_state), (xs, ada_s, cache_state)):
            x1, st1 = _mixer(x, (sh1, sc1, g1), st, p, bf16_weights, cast_jobs)
            cast_jobs = {}
            y = _ffn(x1, norm2, sc2, sh2, g2, bf16_weights["w_g"], bf16_weights["w_u"],
                     bf16_weights["w_down"], gf, final_norm=last)
            new.append((y, st1))
        (xp, st_p), (xs, st_s) = new
        p_states.append(st_p)
        s_states.append(st_s)

    stack = lambda sts, i: jnp.stack([s[i] for s in sts])
    return (xp, xs) + tuple(stack(p_states, i) for i in range(5)) + tuple(stack(s_states, i) for i in range(5))
```

```python
import functools

import jax
import jax.numpy as jnp
from jax import lax
from jax.experimental import pallas as pl
from jax.experimental.pallas import tpu as pltpu

F32 = jnp.float32
BF16 = jnp.bfloat16
EPS = 1e-6
LRU_C = 8.0
LANES = 128
SUBLANES = 8
VMEM_LIMIT_BYTES = 60 * 1024 * 1024


def _cparams(*sem):
    return pltpu.CompilerParams(dimension_semantics=sem, vmem_limit_bytes=VMEM_LIMIT_BYTES)


def _pick(total, target, quantum):
    if total <= target:
        return total
    best = None
    for cand in range(quantum, target + 1, quantum):
        if total % cand == 0:
            best = cand
    assert best is not None, (total, target, quantum)
    return best


def _row_tiling(B, T, target_rows):
    if T >= target_rows:
        return 1, _pick(T, target_rows, 16)
    bb = _pick(B, max(1, target_rows // T), 1)
    return bb, T


def _mlstm_chunk(T):
    return _pick(T, LANES, 16)


def _log_sigmoid(x):
    return jnp.minimum(x, 0.0) - jnp.log1p(jnp.exp(-jnp.abs(x)))


def _sigmoid(x):
    return 0.5 * (1.0 + jnp.tanh(0.5 * x))


def _gelu_tanh(x):
    c = 0.7978845608028654
    return 0.5 * x * (1.0 + jnp.tanh(c * (x + 0.044715 * (x * x * x))))


def _dot(a, b):
    return jnp.dot(a, b, preferred_element_type=F32)


def _dot_nt(a, b):
    return lax.dot_general(a, b, (((1,), (1,)), ((), ())), preferred_element_type=F32)


def _dot_tn(a, b):
    return lax.dot_general(a, b, (((0,), (0,)), ((), ())), preferred_element_type=F32)


def _split_bf16(x, terms):
    out = []
    for _ in range(terms - 1):
        t = x.astype(BF16)
        out.append(t)
        x = x - t.astype(F32)
    out.append(x.astype(BF16))
    return out


def _ada_kernel(c_ref, w_ref, b_ref, o_ref):
    c = c_ref[...]
    s = (c * _sigmoid(c)).astype(BF16)
    o_ref[...] = _dot(s, w_ref[...].astype(BF16)) + b_ref[...]


def _ada(c, w, b):
    R, D = c.shape
    N = w.shape[1]
    tn = _pick(N, 512, LANES)
    return pl.pallas_call(
        _ada_kernel,
        grid=(N // tn,),
        in_specs=[pl.BlockSpec((R, D), lambda j: (0, 0)),
                  pl.BlockSpec((D, tn), lambda j: (0, j)),
                  pl.BlockSpec((1, tn), lambda j: (0, j))],
        out_specs=pl.BlockSpec((R, tn), lambda j: (0, j)),
        out_shape=jax.ShapeDtypeStruct((R, N), F32),
        compiler_params=_cparams("arbitrary"),
        name="ada",
    )(c, w, b)


def _normmod(x, geff, sh):
    r = lax.rsqrt(jnp.mean(x * x, axis=-1, keepdims=True) + EPS)
    return x * r * geff + sh


def _normmod_gates_kernel(x_ref, g_ref, sc_ref, sh_ref, wg_ref, bg_ref, h_ref, gr_ref, gt_s, *, n_heads, chunk):
    bb, tt, D = x_ref.shape
    H = n_heads
    hb = _normmod(x_ref[...], g_ref[...] * (1.0 + sc_ref[...]), sh_ref[...]).astype(BF16)
    h_ref[...] = hb
    pre = _dot_nt(hb.reshape(bb * tt, D), wg_ref[...]) + bg_ref[...]
    lane = lax.broadcasted_iota(jnp.int32, pre.shape, 1)
    gates = jnp.where(lane < H, pre, _log_sigmoid(pre))
    rows, lanes_out = gt_s.shape[2], gt_s.shape[3]
    if lanes_out > chunk:
        gt_s[...] = jnp.zeros_like(gt_s)
    blocks = [(i, c) for i in range(bb) for c in range(tt // chunk)]
    for i, c in blocks:
        r0 = i * tt + c * chunk
        gt_s[i, c, :, 0:chunk] = gates[r0:r0 + chunk, :].T[:rows, :]
    ig = jnp.concatenate([gt_s[i, c, 0:H, :] for i, c in blocks], axis=0)
    lf = jnp.concatenate([gt_s[i, c, H:2 * H, :] for i, c in blocks], axis=0)
    lane_t = lax.broadcasted_iota(jnp.int32, lf.shape, 1)
    bcum = _lane_scan(lf, jnp.add, lane_t, 0.0)
    a = ig - bcum
    cm = _lane_scan(a, jnp.maximum, lane_t, -jnp.inf)
    if gr_ref.shape[2] > 3 * H:
        gr_ref[...] = jnp.zeros_like(gr_ref)
    for k, (i, c) in enumerate(blocks):
        for r, val in enumerate((a, cm, bcum)):
            gr_ref[i, c, r * H:(r + 1) * H, :] = val[k * H:(k + 1) * H, :]


def _normmod_gates(x, g, sc, sh, wg, gate_blk, bg, n_heads):
    B, T, D = x.shape
    L = _mlstm_chunk(T)
    bb, tt = _row_tiling(B, T, 512)
    assert tt % L == 0
    GR = -(-2 * n_heads // SUBLANES) * SUBLANES
    GR3 = -(-3 * n_heads // SUBLANES) * SUBLANES
    LP = max(L, LANES)
    return pl.pallas_call(
        functools.partial(_normmod_gates_kernel, n_heads=n_heads, chunk=L),
        grid=(B // bb, T // tt),
        in_specs=[pl.BlockSpec((bb, tt, D), lambda b, t: (b, t, 0)),
                  pl.BlockSpec((1, 1, D), lambda b, t: (0, 0, 0)),
                  pl.BlockSpec((bb, 1, D), lambda b, t: (b, 0, 0)),
                  pl.BlockSpec((bb, 1, D), lambda b, t: (b, 0, 0)),
                  pl.BlockSpec((LANES, D), lambda b, t: (gate_blk, 0)),
                  pl.BlockSpec((1, LANES), lambda b, t: (0, 0))],
        out_specs=[pl.BlockSpec((bb, tt, D), lambda b, t: (b, t, 0)),
                   pl.BlockSpec((bb, tt // L, GR3, LP), lambda b, t: (b, t, 0, 0))],
        out_shape=[jax.ShapeDtypeStruct((B, T, D), BF16),
                   jax.ShapeDtypeStruct((B, T // L, GR3, LP), F32)],
        scratch_shapes=[pltpu.VMEM((bb, tt // L, GR, LP), F32)],
        compiler_params=_cparams("arbitrary", "arbitrary"),
        name="normmod_gates",
    )(x, g, sc, sh, wg, bg)


def _inproj_kernel(a_ref, w_ref, *rest, epilogue, with_cast):
    rest = list(rest)
    if with_cast:
        cast_out = rest.pop()
        o_ref = rest.pop()
        cast_out[...] = rest.pop()[...].astype(BF16)
    else:
        o_ref = rest.pop()
    acc = _dot_nt(a_ref[...], w_ref[...])
    if epilogue == "colscale":
        acc = acc * rest[0][...]
    elif epilogue == "sigmoid":
        acc = _sigmoid(acc)
    elif epilogue == "gelu":
        acc = _gelu_tanh(acc)
    o_ref[...] = acc.astype(o_ref.dtype)


def _inproj_tile(M):
    return _pick(M, 512, 16)


def _cast_job(w, steps, col_blocks=1, col_block=0):
    R, C = w.shape
    Cb = C // col_blocks
    if C % col_blocks or (col_blocks > 1 and Cb % LANES):
        return None
    if R % (steps * 2 * SUBLANES) == 0:
        return w, (R // steps, Cb), (lambda i: (i, col_block)), (lambda i: (i, 0)), (R, Cb)
    if col_blocks == 1 and C % (steps * LANES) == 0:
        return w, (R, C // steps), (lambda i: (0, i)), (lambda i: (0, i)), (R, C)
    return None


def _inproj(a, wt, row_block, N, out_dtype, epilogue, colscale=None, cast=None):
    M, K = a.shape
    tm = _inproj_tile(M)
    in_specs = [pl.BlockSpec((tm, K), lambda i: (i, 0)),
                pl.BlockSpec((N, K), lambda i: (row_block, 0), pipeline_mode=pl.Buffered(1))]
    args = [a, wt]
    if epilogue == "colscale":
        in_specs.append(pl.BlockSpec((1, N), lambda i: (0, 0)))
        args.append(colscale)
    out_specs = [pl.BlockSpec((tm, N), lambda i: (i, 0))]
    out_shape = [jax.ShapeDtypeStruct((M, N), out_dtype)]
    if cast is not None:
        src, blk, src_map, dst_map, shape = cast
        in_specs.append(pl.BlockSpec(blk, src_map))
        args.append(src)
        out_specs.append(pl.BlockSpec(blk, dst_map))
        out_shape.append(jax.ShapeDtypeStruct(shape, BF16))
    outs = pl.pallas_call(
        functools.partial(_inproj_kernel, epilogue=epilogue, with_cast=cast is not None),
        grid=(M // tm,),
        in_specs=in_specs,
        out_specs=out_specs,
        out_shape=out_shape,
        compiler_params=_cparams("arbitrary"),
        name="inproj_" + epilogue,
    )(*args)
    return outs if cast is not None else outs[0]


def _lane_scan(x, op, lane, identity):
    d = 1
    while d < x.shape[1]:
        x = op(x, jnp.where(lane >= d, pltpu.roll(x, d, axis=1), identity))
        d *= 2
    return x


def _mlstm_kernel(qk_ref, v_ref, og_ref, gr_ref, c0_ref, n0_ref, m0_ref, hg_ref,
                  y_ref, c_ref, n_ref, m_ref, cn_s, m_s, *, chunk, n_heads, dk, dv, head_group):
    tb = pl.program_id(1)
    L, H = chunk, n_heads
    Tb = qk_ref.shape[1]
    nc = Tb // L
    LP = gr_ref.shape[3]
    assert dv % LANES == 0 and L <= LANES
    rep = lambda x, n: jnp.concatenate([x] * n, axis=1)

    eye_k = (lax.broadcasted_iota(jnp.int32, (dk, dk), 0) == lax.broadcasted_iota(jnp.int32, (dk, dk), 1))

    @pl.when(tb == 0)
    def _():
        for h in range(H):
            cn_s[h, :, 0:dv] = c0_ref[0, h]
            n_col = jnp.sum(jnp.where(eye_k, n0_ref[0, h:h + 1, :], 0.0), axis=1, keepdims=True)
            cn_s[h, :, dv:dv + LANES] = jnp.broadcast_to(n_col, (dk, LANES))
        m_s[...] = jnp.broadcast_to(m0_ref[0], m_s.shape)

    row = lax.broadcasted_iota(jnp.int32, (L, L), 0)
    col = lax.broadcasted_iota(jnp.int32, (L, L), 1)
    tril = col <= row
    nr = 3 * H
    spread = ((lax.broadcasted_iota(jnp.int32, (3 * nr, nr * LANES), 1) // LANES)
              == (lax.broadcasted_iota(jnp.int32, (3 * nr, nr * LANES), 0) % nr)).astype(BF16)
    ones_2l = jnp.ones((2 * L, LANES), BF16)
    mean_2v = jnp.full((2 * dv, LANES), 1.0 / dv, BF16)
    assert dv & (dv - 1) == 0

    def chunk_step(c, carry):
        sl = pl.ds(pl.multiple_of(c * L, L), L)
        rows = gr_ref[0, tb * nc + c][0:nr]
        a = rows[0:H]
        cols = _dot_tn(jnp.concatenate(_split_bf16(rows, 3), axis=0), spread)
        blk = lambda r: cols[0:L, r * LANES:(r + 1) * LANES]
        m_all = m_s[...]
        nv = dv // LANES
        m_new = {}
        for h0 in range(0, H, head_group):
            heads = range(h0, min(h0 + head_group, H))
            qb = {h: qk_ref[0, sl, h * dk:(h + 1) * dk] for h in heads}
            kb = {h: qk_ref[0, sl, (H + h) * dk:(H + h + 1) * dk] for h in heads}
            vb = {h: v_ref[0, sl, h * dv:(h + 1) * dv] for h in heads}
            cn = {h: cn_s[h] for h in heads}
            qk = {h: _dot_nt(qb[h], kb[h]) for h in heads}
            qc = {h: _dot(qb[h], cn[h].astype(BF16)) for h in heads}

            g, s_inter, sv, rs = {}, {}, {}, {}
            for h in heads:
                g[h] = jnp.maximum(m_all[h:h + 1, :], blk(H + h))
                s_inter[h] = jnp.exp(m_all[h:h + 1, :] - g[h])
                w = jnp.where(tril, jnp.exp(jnp.minimum(a[h:h + 1, 0:L] - g[h][:, 0:L], 0.0)), 0.0)
                s2 = _split_bf16(qk[h] * w, 2)
                sv[h] = _dot(s2[0], vb[h])
                rs[h] = _dot(jnp.concatenate(s2, axis=1), ones_2l)

            hh, ms = {}, {}
            for h in heads:
                num = rep(s_inter[h], nv) * qc[h][:, 0:dv] + sv[h]
                den = s_inter[h] * qc[h][:, dv:dv + LANES] + rs[h]
                inv = 1.0 / jnp.maximum(jnp.abs(den), jnp.exp(-(blk(2 * H + h) + g[h])))
                hh[h] = num * rep(inv, nv)
                ms[h] = _dot(jnp.concatenate(_split_bf16(hh[h] * hh[h], 2), axis=1), mean_2v)

            for h in heads:
                hn = hh[h] * rep(lax.rsqrt(ms[h] + EPS), nv) * hg_ref[h:h + 1, :]
                y_ref[0, sl, h * dv:(h + 1) * dv] = (hn * og_ref[0, sl, h * dv:(h + 1) * dv]).astype(y_ref.dtype)
                g_last = g[h][L - 1:L, :]
                w_last = jnp.exp(blk(h) - g_last)
                upd_rhs = jnp.concatenate([rep(w_last, nv) * vb[h].astype(F32), w_last], axis=1)
                decay = jnp.exp(m_all[h:h + 1, :] - g_last)
                cn_s[h] = rep(decay, nv + 1) * cn[h] + _dot_tn(kb[h], upd_rhs.astype(BF16))
                m_new[h] = blk(2 * H + h)[L - 1:L, :] + g_last
        m_s[...] = jnp.concatenate([m_new[h] for h in range(H)], axis=0)
        return carry

    lax.fori_loop(0, nc, chunk_step, 0)

    @pl.when(tb == pl.num_programs(1) - 1)
    def _():
        for h in range(H):
            c_ref[0, h] = cn_s[h, :, 0:dv]
            n_ref[0, h:h + 1, :] = jnp.sum(jnp.where(eye_k, cn_s[h, :, dv:dv + 1], 0.0), axis=0, keepdims=True)
        m_ref[0] = m_s[...]


def _mlstm(qk, v, og, gates, c0, n0, m0, head_g):
    B, T, _ = qk.shape
    _, H, dk, dv = c0.shape
    L = _mlstm_chunk(T)
    Tb = _pick(T, 512, L)
    _, NC, GR, LP = gates.shape
    y, c1, n1, m1 = pl.pallas_call(
        functools.partial(_mlstm_kernel, chunk=L, n_heads=H, dk=dk, dv=dv, head_group=4),
        grid=(B, T // Tb),
        in_specs=[pl.BlockSpec((1, Tb, 2 * H * dk), lambda b, t: (b, t, 0)),
                  pl.BlockSpec((1, Tb, H * dv), lambda b, t: (b, t, 0)),
                  pl.BlockSpec((1, Tb, H * dv), lambda b, t: (b, t, 0)),
                  pl.BlockSpec((1, NC, GR, LP), lambda b, t: (b, 0, 0, 0)),
                  pl.BlockSpec((1, H, dk, dv), lambda b, t: (b, 0, 0, 0)),
                  pl.BlockSpec((1, H, dk), lambda b, t: (b, 0, 0)),
                  pl.BlockSpec((1, H, 1), lambda b, t: (b, 0, 0)),
                  pl.BlockSpec((H, dv), lambda b, t: (0, 0))],
        out_specs=[pl.BlockSpec((1, Tb, H * dv), lambda b, t: (b, t, 0)),
                   pl.BlockSpec((1, H, dk, dv), lambda b, t: (b, 0, 0, 0)),
                   pl.BlockSpec((1, H, dk), lambda b, t: (b, 0, 0)),
                   pl.BlockSpec((1, H, LANES), lambda b, t: (b, 0, 0))],
        out_shape=[jax.ShapeDtypeStruct((B, T, H * dv), BF16),
                   jax.ShapeDtypeStruct((B, H, dk, dv), F32),
                   jax.ShapeDtypeStruct((B, H, dk), F32),
                   jax.ShapeDtypeStruct((B, H, LANES), F32)],
        scratch_shapes=[pltpu.VMEM((H, dk, dv + LANES), F32),
                        pltpu.VMEM((H, LANES), F32)],
        compiler_params=_cparams("arbitrary", "arbitrary"),
        name="mlstm",
    )(qk, v, og, gates, c0, n0, m0.reshape(B, H, 1), head_g)
    return y, c1, n1, m1[:, :, 0]


def _scan_pitch(rows):
    quads = -(-(-(-rows // SUBLANES)) // 4)
    return 4 * (quads + 1 - quads % 2)


def _rglru_kernel(xb_ref, gg_ref, cv0_ref, h0_ref, cw_ref, cb_ref, wr_ref, wi_ref, br_ref, bi_ref,
                  lam_ref, y_ref, h_ref, cv_ref, xext, a_s, u_s, *, pitch, group):
    tb = pl.program_id(1)
    Tb, W = xb_ref.shape[1], xb_ref.shape[2]
    nb, bw = wr_ref.shape[0], wr_ref.shape[1]
    taps = cw_ref.shape[0]
    rows = a_s.shape[1]
    pad = SUBLANES

    @pl.when(tb == 0)
    def _():
        h_ref[...] = h0_ref[...]
        cv_ref[...] = cv0_ref[...]

    x = xb_ref[0]
    xext[pl.ds(pad, Tb), :] = x
    xext[pl.ds(pad - (taps - 1), taps - 1), :] = cv_ref[0]
    xc = cb_ref[...] + cw_ref[taps - 1:taps, :] * x
    for j in range(taps - 1):
        xc = xc + cw_ref[j:j + 1, :] * xext[pl.ds(pad - (taps - 1) + j, Tb), :]
    cv_ref[0] = xext[pl.ds(pad + Tb - (taps - 1), taps - 1), :]

    log_sig_lam = _log_sigmoid(lam_ref[...])
    for n in range(nb):
        cols = slice(n * bw, (n + 1) * bw)
        xn = xc[:, cols]
        xnb = xn.astype(BF16)
        r = _sigmoid(_dot(xnb, wr_ref[n]) + br_ref[:, cols])
        i = _sigmoid(_dot(xnb, wi_ref[n]) + bi_ref[:, cols])
        log_a = LRU_C * r * log_sig_lam[:, cols]
        a = jnp.exp(log_a)
        a_s[n, 0:Tb, :] = a
        z = -jnp.tanh(log_a) * (1.0 + a * a)
        u_s[n, 0:Tb, :] = jnp.where(z > 0.0, z * lax.rsqrt(z), 0.0) * (i * xn)
        if rows > Tb:
            a_s[n, Tb:rows, :] = jnp.ones((rows - Tb, bw), F32)
            u_s[n, Tb:rows, :] = jnp.zeros((rows - Tb, bw), F32)

    for n0 in range(0, nb, group):
        slabs = list(range(n0, min(n0 + group, nb)))

        def pass1(i, carry):
            idx = pl.ds(i, SUBLANES, stride=pitch)
            out = []
            for (hz, ac), n in zip(carry, slabs):
                av = a_s[n, idx, :]
                hz = av * hz + u_s[n, idx, :]
                ac = av * ac
                u_s[n, idx, :] = hz
                a_s[n, idx, :] = ac
                out.append((hz, ac))
            return tuple(out)

        init = tuple((jnp.zeros((SUBLANES, bw), F32), jnp.ones((SUBLANES, bw), F32)) for _ in slabs)
        ends = lax.fori_loop(0, pitch, pass1, init, unroll=2)

        starts = []
        for (hz, ac), n in zip(ends, slabs):
            cols = slice(n * bw, (n + 1) * bw)
            h = h_ref[0, :, cols]
            per_block = []
            for j in range(SUBLANES):
                per_block.append(h)
                h = hz[j:j + 1, :] + ac[j:j + 1, :] * h
            h_ref[0, :, cols] = h
            starts.append(jnp.concatenate(per_block, axis=0))

        def pass2(i, carry):
            idx = pl.ds(i, SUBLANES, stride=pitch)
            for st, n in zip(starts, slabs):
                u_s[n, idx, :] = u_s[n, idx, :] + a_s[n, idx, :] * st
            return carry

        lax.fori_loop(0, pitch, pass2, 0, unroll=2)

    for n in range(nb):
        cols = slice(n * bw, (n + 1) * bw)
        y_ref[0, :, cols] = (u_s[n, 0:Tb, :] * gg_ref[0, :, cols]).astype(y_ref.dtype)


def _rglru(xb, gg, cv0, h0, conv_w, conv_b, w_r, w_i, b_r, b_i, lam):
    B, T, W = xb.shape
    taps = conv_w.shape[0]
    nb, bw = w_r.shape[0], w_r.shape[1]
    assert T >= taps - 1 and bw == LANES and nb * bw == W
    Tb = _pick(T, 256, SUBLANES)
    pitch = _scan_pitch(Tb)
    row = lambda a: a.reshape(1, W)
    y, h1, cv1 = pl.pallas_call(
        functools.partial(_rglru_kernel, pitch=pitch, group=8),
        grid=(B, T // Tb),
        in_specs=[pl.BlockSpec((1, Tb, W), lambda b, t: (b, t, 0)),
                  pl.BlockSpec((1, Tb, W), lambda b, t: (b, t, 0)),
                  pl.BlockSpec((1, taps - 1, W), lambda b, t: (b, 0, 0)),
                  pl.BlockSpec((1, 1, W), lambda b, t: (b, 0, 0)),
                  pl.BlockSpec((taps, W), lambda b, t: (0, 0)),
                  pl.BlockSpec((1, W), lambda b, t: (0, 0)),
                  pl.BlockSpec((nb, bw, bw), lambda b, t: (0, 0, 0)),
                  pl.BlockSpec((nb, bw, bw), lambda b, t: (0, 0, 0)),
                  pl.BlockSpec((1, W), lambda b, t: (0, 0)),
                  pl.BlockSpec((1, W), lambda b, t: (0, 0)),
                  pl.BlockSpec((1, W), lambda b, t: (0, 0))],
        out_specs=[pl.BlockSpec((1, Tb, W), lambda b, t: (b, t, 0)),
                   pl.BlockSpec((1, 1, W), lambda b, t: (b, 0, 0)),
                   pl.BlockSpec((1, taps - 1, W), lambda b, t: (b, 0, 0))],
        out_shape=[jax.ShapeDtypeStruct((B, T, W), BF16),
                   jax.ShapeDtypeStruct((B, 1, W), F32),
                   jax.ShapeDtypeStruct((B, taps - 1, W), F32)],
        scratch_shapes=[pltpu.VMEM((Tb + SUBLANES, W), F32),
                        pltpu.VMEM((nb, SUBLANES * pitch, bw), F32),
                        pltpu.VMEM((nb, SUBLANES * pitch, bw), F32)],
        compiler_params=_cparams("arbitrary", "arbitrary"),
        name="rglru",
    )(xb, gg, cv0, h0.reshape(B, 1, W), conv_w, row(conv_b), w_r.astype(BF16), w_i.astype(BF16),
      row(b_r), row(b_i), row(lam))
    return y, h1.reshape(B, W), cv1


def _outproj_kernel(ya_ref, yb_ref, wt_ref, wb_ref, x_ref, g_ref, o_ref):
    bb, tt, wa = ya_ref.shape
    wb = yb_ref.shape[2]
    mix = (_dot(ya_ref[...].reshape(bb * tt, wa), wt_ref[...])
           + _dot(yb_ref[...].reshape(bb * tt, wb), wb_ref[...]))
    o_ref[...] = x_ref[...] + g_ref[...] * mix.reshape(bb, tt, mix.shape[-1])


def _outproj(ya, yb, w_out, x, g1):
    B, T, D = x.shape
    wa, wb = ya.shape[2], yb.shape[2]
    assert wa == wb
    bb, tt = _row_tiling(B, T, 1024)
    tn = _pick(D, 1024, LANES)
    return pl.pallas_call(
        _outproj_kernel,
        grid=(B // bb, T // tt, D // tn),
        in_specs=[pl.BlockSpec((bb, tt, wa), lambda b, t, j: (b, t, 0)),
                  pl.BlockSpec((bb, tt, wb), lambda b, t, j: (b, t, 0)),
                  pl.BlockSpec((wa, tn), lambda b, t, j: (0, j)),
                  pl.BlockSpec((wb, tn), lambda b, t, j: (1, j)),
                  pl.BlockSpec((bb, tt, tn), lambda b, t, j: (b, t, j)),
                  pl.BlockSpec((bb, 1, tn), lambda b, t, j: (b, 0, j))],
        out_specs=pl.BlockSpec((bb, tt, tn), lambda b, t, j: (b, t, j)),
        out_shape=jax.ShapeDtypeStruct((B, T, D), F32),
        compiler_params=_cparams("arbitrary", "arbitrary", "arbitrary"),
        name="outproj",
    )(ya, yb, w_out, w_out, x, g1)


def _ffn_kernel(x_ref, ng_ref, sc_ref, sh_ref, g2_ref, *rest, n_col_chunks, final_norm, tiles_per_step, n_tiles):
    w_refs, (gf_ref, o_ref, h_s, x_buf, x_sem) = rest[:3 * tiles_per_step], rest[3 * tiles_per_step:]
    x_hbm, x_ref = x_ref, x_buf
    gb, gt, j = pl.program_id(0), pl.program_id(1), pl.program_id(2)
    n_gb, n_gt = pl.num_programs(0), pl.num_programs(1)
    last = pl.num_programs(2) - 1
    bb, tt, D = x_buf.shape
    n_steps = pl.cdiv(n_tiles, tiles_per_step)
    tail = n_tiles - (n_steps - 1) * tiles_per_step
    rc = 2 * SUBLANES

    whole = (0, bb, 0, tt)
    if bb > 1 and bb % 2 == 0:
        halves = [(0, bb // 2, 0, tt), (bb // 2, bb, 0, tt)]
    elif bb == 1 and tt % (2 * rc) == 0:
        halves = [(0, 1, 0, tt // 2), (0, 1, tt // 2, tt)]
    else:
        halves = [whole]

    def row_groups(slab, size):
        b0, b1, r0, r1 = slab
        return [(b, r) for b in range(b0, b1) for r in range(r0, r1, size)]

    def normmod_rows(slab):
        def half(b, r):
            x = x_ref[b, pl.ds(r, SUBLANES), :]
            scale = lax.rsqrt(jnp.mean(x * x, axis=-1, keepdims=True) + EPS)
            return x * scale * (ng_ref[0] * (1.0 + sc_ref[b])) + sh_ref[b]

        for b, r in row_groups(slab, rc):
            h = jnp.concatenate([half(b, r), half(b, r + SUBLANES)], axis=0)
            h_s[pl.ds(b * tt + r, rc), :] = h.astype(BF16)

    def hidden_tiles(count, slab, first):
        b0, b1, r0, r1 = slab
        hb = h_s[pl.ds(b0 * tt + r0, (b1 - b0 - 1) * tt + r1 - r0), :]
        cw = D // n_col_chunks
        acts = []
        for k in range(count):
            gate = _dot(hb, w_refs[3 * k][...])
            up = _dot(hb, w_refs[3 * k + 1][...])
            acts.append((gate * _sigmoid(gate) * up).astype(BF16))
        for c in range(n_col_chunks):
            cols = slice(c * cw, (c + 1) * cw)
            down = _dot(acts[0], w_refs[2][:, cols])
            for k in range(1, count):
                down = down + _dot(acts[k], w_refs[3 * k + 2][:, cols])
            down = down.reshape(b1 - b0, r1 - r0, cw)
            if first:
                o_ref[b0:b1, r0:r1, cols] = down
            else:
                o_ref[b0:b1, r0:r1, cols] += down

    def finish_rows(slab):
        for b, r in row_groups(slab, SUBLANES):
            rows = pl.ds(r, SUBLANES)
            x2 = x_ref[b, rows, :] + g2_ref[b] * o_ref[b, rows, :]
            if final_norm:
                x2 = x2 * lax.rsqrt(jnp.mean(x2 * x2, axis=-1, keepdims=True) + EPS) * gf_ref[0]
            o_ref[b, rows, :] = x2

    assert n_steps >= 2 and len(halves) == 2

    def x_copy(k, tile_b, tile_t):
        b0, b1, r0, r1 = halves[k]
        src = x_hbm.at[pl.ds(tile_b * bb + b0, b1 - b0), pl.ds(tile_t * tt + r0, r1 - r0), :]
        return pltpu.make_async_copy(src, x_buf.at[b0:b1, r0:r1, :], x_sem.at[k])

    is_first_tile = jnp.logical_and(gb == 0, gt == 0)
    has_next_tile = jnp.logical_not(jnp.logical_and(gb == n_gb - 1, gt == n_gt - 1))
    wrap = gt == n_gt - 1
    next_b = jnp.where(wrap, gb + 1, gb)
    next_t = jnp.where(wrap, 0, gt + 1)

    def first_step(count, finish):
        @pl.when(is_first_tile)
        def _():
            x_copy(0, gb, gt).start()
            x_copy(1, gb, gt).start()
        for k, slab in enumerate(halves):
            x_copy(k, gb, gt).wait()
            normmod_rows(slab)
            hidden_tiles(count, slab, first=True)

    def last_step():
        hidden_tiles(tail, halves[0], first=False)
        finish_rows(halves[0])
        hidden_tiles(tail, halves[1], first=False)
        pl.when(has_next_tile)(lambda: x_copy(0, next_b, next_t).start())
        finish_rows(halves[1])
        pl.when(has_next_tile)(lambda: x_copy(1, next_b, next_t).start())

    if n_steps == 1:
        first_step(tail, finish=True)
    else:
        pl.when(j == 0)(lambda: first_step(tiles_per_step, finish=False))
        if n_steps > 2:
            pl.when(jnp.logical_and(j > 0, j < last))(lambda: hidden_tiles(tiles_per_step, whole, first=False))
        pl.when(j == last)(last_step)


def _ffn(x, ng, sc, sh, g2, w_g, w_u, w_down, gf, final_norm):
    B, T, D = x.shape
    F = w_down.shape[0]
    tf = _pick(F, 256, LANES)
    nf = F // tf
    u_off = nf if w_u.shape[1] == 2 * F else 0
    bb, tt = _row_tiling(B, T, 512)
    tps = 2
    w_specs, w_args = [], []
    for k in range(tps):
        tile = lambda j, k=k: jnp.minimum(j * tps + k, nf - 1)
        w_specs += [pl.BlockSpec((D, tf), lambda b, t, j, tile=tile: (0, tile(j))),
                    pl.BlockSpec((D, tf), lambda b, t, j, tile=tile: (0, u_off + tile(j))),
                    pl.BlockSpec((tf, D), lambda b, t, j, tile=tile: (tile(j), 0))]
        w_args += [w_g, w_u, w_down]
    return pl.pallas_call(
        functools.partial(_ffn_kernel, n_col_chunks=max(1, D // 1024), final_norm=final_norm,
                          tiles_per_step=tps, n_tiles=nf),
        grid=(B // bb, T // tt, pl.cdiv(nf, tps)),
        in_specs=[pl.BlockSpec(memory_space=pl.ANY),
                  pl.BlockSpec((1, 1, D), lambda b, t, j: (0, 0, 0)),
                  pl.BlockSpec((bb, 1, D), lambda b, t, j: (b, 0, 0)),
                  pl.BlockSpec((bb, 1, D), lambda b, t, j: (b, 0, 0)),
                  pl.BlockSpec((bb, 1, D), lambda b, t, j: (b, 0, 0))] + w_specs + [
                  pl.BlockSpec((1, 1, D), lambda b, t, j: (0, 0, 0))],
        out_specs=pl.BlockSpec((bb, tt, D), lambda b, t, j: (b, t, 0)),
        out_shape=jax.ShapeDtypeStruct((B, T, D), F32),
        scratch_shapes=[pltpu.VMEM((bb * tt, D), BF16),
                        pltpu.VMEM((bb, tt, D), F32),
                        pltpu.SemaphoreType.DMA((2,))],
        compiler_params=_cparams("arbitrary", "arbitrary", "arbitrary"),
        name="ffn",
    )(x, ng, sc, sh, g2, *w_args, gf)


def _mixer(x, ada, state, p, bf16_weights, cast_jobs):
    sh1, sc1, g1 = ada
    c0, n0, m0, hl0, cv0 = state
    B, T, D = x.shape
    _, H, dk, dv = c0.shape
    W = cv0.shape[2]
    n_qk, n_v = 2 * H * dk, H * dv
    assert n_qk == n_v and (3 * n_v) % LANES == 0
    h1, gates = _normmod_gates(x, p["norm1_g"], sc1, sh1, p["w_in"], 3 * n_v // LANES, p["b_gates"], H)
    h1 = h1.reshape(B * T, D)
    jobs = list(cast_jobs.items())

    def proj(w, blk, n, dt, ep, cs=None):
        name, job = jobs.pop() if jobs else (None, None)
        out = _inproj(h1, w, blk, n, dt, ep, cs, job)
        if job is not None:
            out, bf16_weights[name] = out
        return out.reshape(B, T, n)

    qk = proj(p["w_in"], 0, n_qk, BF16, "colscale", p["qk_scale"])
    v = proj(p["w_in"], 1, n_v, BF16, "none")
    og = proj(p["w_in"], 2, n_v, F32, "sigmoid")
    xb = proj(p["w_xg"], 0, W, F32, "none")
    gg = proj(p["w_xg"], 1, W, F32, "gelu")
    assert not jobs
    ya, c1, n1, m1 = _mlstm(qk, v, og, gates, c0, n0, m0, p["head_norm_g"])
    yb, hl1, cv1 = _rglru(xb, gg, cv0, hl0, p["conv_w"], p["conv_b"],
                          p["w_r"], p["w_i"], p["b_r"], p["b_i"], p["lru_lambda"])
    return _outproj(ya, yb, bf16_weights["w_out"], x, g1), (c1, n1, m1, hl1, cv1)


def kernel(x_prompt, x_sample, c_prompt, c_sample, state_mlstm_C, state_mlstm_n, state_mlstm_m, state_lru_h, state_conv, w_ada, b_ada, norm1_g, norm2_g, w_in, b_gates_a, head_norm_g, conv_w, conv_b, w_r, b_r, w_i, b_i, lru_lambda, w_out, w_gu, w_down, normf_g):
    depth = w_in.shape[0]
    Bp, Tp, D = x_prompt.shape
    Bs = x_sample.shape[0]
    H, dk, dv = state_mlstm_C.shape[2:]
    W = conv_w.shape[2]
    taps = conv_w.shape[1]
    c_x = 2 * H * dk + 2 * H * dv + 2 * H

    xp, xs = x_prompt, x_sample
    p_states, s_states = [], []
    R = Bp + Bs
    Rpad = -(-R // SUBLANES) * SUBLANES
    c_all = jnp.concatenate([c_prompt, c_sample, jnp.zeros((Rpad - R, D), F32)], axis=0)
    gf = normf_g.reshape(1, 1, D)
    qk_scale = jnp.concatenate([jnp.ones((1, H * dk), F32), jnp.full((1, H * dk), dk ** -0.5, F32)], axis=1)
    for l in range(depth):
        w_in_l = w_in[l]
        p = {
            "norm1_g": norm1_g[l].reshape(1, 1, D),
            "w_in": w_in_l.T.astype(BF16),
            "w_xg": w_in_l.T[c_x:].astype(BF16),
            "qk_scale": qk_scale,
            "b_gates": jnp.pad(b_gates_a[l].reshape(1, 2 * H), ((0, 0), (0, LANES - 2 * H))),
            "head_norm_g": head_norm_g[l],
            "conv_w": conv_w[l], "conv_b": conv_b[l],
            "w_r": w_r[l], "w_i": w_i[l], "b_r": b_r[l], "b_i": b_i[l],
            "lru_lambda": lru_lambda[l],
        }
        steps = (Bp * Tp) // _inproj_tile(Bp * Tp)
        planned = {"w_g": _cast_job(w_gu[l], steps, 2, 0), "w_u": _cast_job(w_gu[l], steps, 2, 1),
                   "w_down": _cast_job(w_down[l], steps), "w_out": _cast_job(w_out[l], steps)}
        cast_jobs = {k: j for k, j in planned.items() if j is not None}
        bf16_weights = {}
        if "w_g" not in cast_jobs or "w_u" not in cast_jobs:
            cast_jobs.pop("w_g", None), cast_jobs.pop("w_u", None)
            bf16_weights["w_g"] = bf16_weights["w_u"] = w_gu[l].astype(BF16)
        if "w_down" not in cast_jobs:
            bf16_weights["w_down"] = w_down[l].astype(BF16)
        if "w_out" not in cast_jobs:
            bf16_weights["w_out"] = w_out[l].astype(BF16)
        norm2 = norm2_g[l].reshape(1, 1, D)
        ada = _ada(c_all, w_ada[l], b_ada[l].reshape(1, -1))
        ada_p = [ada[:Bp, i * D:(i + 1) * D].reshape(Bp, 1, D) for i in range(6)]
        ada_s = [ada[Bp:R, i * D:(i + 1) * D].reshape(Bs, 1, D) for i in range(6)]

        zero_state = (jnp.zeros((Bp, H, dk, dv), F32), jnp.zeros((Bp, H, dk), F32),
                      jnp.zeros((Bp, H), F32), jnp.zeros((Bp, W), F32),
                      jnp.zeros((Bp, taps - 1, W), F32))
        cache_state = (state_mlstm_C[l], state_mlstm_n[l], state_mlstm_m[l], state_lru_h[l],
                       state_conv[l])
        last = l == depth - 1
        new = []
        for x, (sh1, sc1, g1, sh2, sc2, g2), st in ((xp, ada_p, zero_state), (xs, ada_s, cache_state)):
            x1, st1 = _mixer(x, (sh1, sc1, g1), st, p, bf16_weights, cast_jobs)
            cast_jobs = {}
            y = _ffn(x1, norm2, sc2, sh2, g2, bf16_weights["w_g"], bf16_weights["w_u"],
                     bf16_weights["w_down"], gf, final_norm=last)
            new.append((y, st1))
        (xp, st_p), (xs, st_s) = new
        p_states.append(st_p)
        s_states.append(st_s)

    stack = lambda sts, i: jnp.stack([s[i] for s in sts])
    return (xp, xs) + tuple(stack(p_states, i) for i in range(5)) + tuple(stack(s_states, i) for i in range(5))
```
